```python
import jax, jax.numpy as jnp
from jax import lax
import numpy as np

D_MODEL = 1024
BATCH = 8
SEQ = 4096
DEPTH = 2

F32 = jnp.float32
EPS = 1e-6
MEM_LEN = 256
N_BRANCHES = 3
D_FF = 2816

RW_HEADS = 8
RW_HEAD_DIM = 64
RW_WIDTH = RW_HEADS * RW_HEAD_DIM
RW_DECAY_LORA = 64
RW_AAA_LORA = 64
RW_GATE_LORA = 160
RW_GN_EPS = 64e-5
RW_COLS = (RW_WIDTH, RW_WIDTH, RW_WIDTH, RW_DECAY_LORA, RW_AAA_LORA, RW_GATE_LORA)
RW_IN_COLS = sum(RW_COLS)

DIL_PATTERNS = ((128, 1), (512, 4), (2048, 16))
N_DIL = len(DIL_PATTERNS)
DIL_HEADS = 8
HEAD_DIM = 64
DIL_WIDTH = DIL_HEADS * HEAD_DIM
DIL_IN_COLS = 3 * N_DIL * DIL_WIDTH
ROPE_THETA = 500000.0
ROPE_DIM = HEAD_DIM // 4

RET_HEADS = 4
RET_QK_DIM = 64
RET_V_DIM = 128
RET_CHUNK = 128
RET_ROPE_BASE = 10000.0
RET_COLS = (RET_HEADS * RET_QK_DIM, RET_HEADS * RET_QK_DIM, RET_HEADS * RET_V_DIM, RET_HEADS * RET_V_DIM)
RET_IN_COLS = sum(RET_COLS)
RET_WIDTH = RET_HEADS * RET_V_DIM

GATE_IN_COLS = N_BRANCHES * D_MODEL
IN_COLS = RW_IN_COLS + DIL_IN_COLS + RET_IN_COLS + GATE_IN_COLS

XA_HEADS = 4
XA_HEAD_DIM = D_MODEL // XA_HEADS

kernel_name = 'hybrid_rwkv7_dilated_retention_block'


def split_cols(x, sizes):
    idx = [int(i) for i in np.cumsum(sizes)[:-1]]
    return jnp.split(x, idx, axis=-1)


def rms_norm(x, g, eps=EPS):
    xf = x.astype(F32)
    y = xf * lax.rsqrt(jnp.mean(xf * xf, axis=-1, keepdims=True) + eps)
    return (y * g.astype(F32)).astype(x.dtype)


def apply_rope(x, pos, rot_dim, base):
    half = rot_dim // 2
    inv_freq = base ** (-jnp.arange(half, dtype=F32) / half)
    ang = pos.astype(F32)[:, None] * inv_freq[None, :]
    cos = jnp.cos(ang)[None, :, None, :]
    sin = jnp.sin(ang)[None, :, None, :]
    xf = x.astype(F32)
    x1, x2, rest = xf[..., :half], xf[..., half:rot_dim], xf[..., rot_dim:]
    out = jnp.concatenate([x1 * cos - x2 * sin, x2 * cos + x1 * sin, rest], axis=-1)
    return out.astype(x.dtype)


def swiglu(x, w13, w2):
    a, b = jnp.split(x @ w13, 2, axis=-1)
    return (jax.nn.silu(a) * b) @ w2


def rwkv7_recurrence(r, decay, k, v, a, b):
    B, T, H, N = r.shape

    def step(S, inp):
        r_t, w_t, k_t, v_t, a_t, b_t = inp
        sa = jnp.einsum('bhvk,bhk->bhv', S, a_t)
        S = S * w_t[:, :, None, :] + sa[..., None] * b_t[:, :, None, :] + v_t[..., None] * k_t[:, :, None, :]
        return S, jnp.einsum('bhvk,bhk->bhv', S, r_t)

    xs = tuple(jnp.moveaxis(t, 1, 0) for t in (r, decay, k, v, a, b))
    _, y = lax.scan(step, jnp.zeros((B, H, N, N), F32), xs)
    return jnp.moveaxis(y, 0, 1)


def rwkv7_time_mix(p, mu, w0, w2, a0, a2, g2, k_k, k_a, r_k, ln_w, ln_b):
    B, T, _ = p.shape
    shifted = jnp.pad(p, ((0, 0), (1, 0), (0, 0)))[:, :-1]
    p = p + (shifted - p) * mu
    r, k, v, wd, ad, gd = split_cols(p, RW_COLS)
    w_log = -jax.nn.softplus(-(w0 + jnp.tanh(wd) @ w2).astype(F32)) - 0.5
    decay = jnp.exp(-jnp.exp(w_log))
    a = jax.nn.sigmoid((a0 + ad @ a2).astype(F32))
    g = jax.nn.sigmoid(gd) @ g2
    heads = lambda t: t.astype(F32).reshape(B, T, RW_HEADS, RW_HEAD_DIM)
    kk = heads(k * k_k)
    kk = kk / jnp.maximum(jnp.sqrt(jnp.sum(kk * kk, axis=-1, keepdims=True)), 1e-12)
    k = k.astype(F32) * (1.0 + (a - 1.0) * k_a.astype(F32))
    r_h, k_h, v_h, a_h, w_h = heads(r), heads(k), heads(v), heads(a), heads(decay)
    y = rwkv7_recurrence(r_h, w_h, k_h, v_h, -kk, kk * a_h)
    mean = jnp.mean(y, axis=-1, keepdims=True)
    var = jnp.mean(jnp.square(y - mean), axis=-1, keepdims=True)
    y = ((y - mean) * lax.rsqrt(var + RW_GN_EPS)).reshape(B, T, RW_WIDTH)
    y = y * ln_w.astype(F32) + ln_b.astype(F32)
    bonus = jnp.sum(r_h * k_h * r_k.astype(F32).reshape(RW_HEADS, RW_HEAD_DIM), axis=-1, keepdims=True) * v_h
    y = (y + bonus.reshape(B, T, RW_WIDTH)) * g.astype(F32)
    return y.astype(p.dtype)


def dilated_window_attention(q, k, v, window, dilation):
    B, T, H, Dh = q.shape
    L = window // dilation
    unit = L * dilation
    Tp = -(-T // unit) * unit
    M = Tp // dilation
    nb = M // L

    def to_blocks(t):
        t = jnp.pad(t, ((0, 0), (0, Tp - T), (0, 0), (0, 0)))
        t = t.reshape(B, M, dilation, H, Dh).transpose(0, 2, 1, 3, 4)
        return t.reshape(B, dilation, nb, L, H, Dh)

    def with_prev(t):
        prev = jnp.pad(t, ((0, 0), (0, 0), (1, 0), (0, 0), (0, 0), (0, 0)))[:, :, :-1]
        return jnp.concatenate([prev, t], axis=3)

    qb = to_blocks(q)
    kb = with_prev(to_blocks(k))
    vb = with_prev(to_blocks(v))
    s = jnp.einsum('brnqhd,brnkhd->brnhqk', qb, kb, preferred_element_type=F32) * (Dh ** -0.5)
    qi = jnp.arange(L)[:, None] + L
    ki = jnp.arange(2 * L)[None, :]
    dist = qi - ki
    band = (dist >= 0) & (dist <= L)
    valid = band[None] & ((jnp.arange(nb)[:, None, None] > 0) | (ki[None] >= L))
    s = jnp.where(valid[None, None, :, None], s, -jnp.inf)
    m = jnp.max(s, axis=-1, keepdims=True)
    e = jnp.exp(s - m)
    den = jnp.sum(e, axis=-1, keepdims=True)
    o = jnp.einsum('brnhqk,brnkhd->brnqhd', e / den, vb.astype(F32))
    lse = (m + jnp.log(den))[..., 0]
    o = o.reshape(B, dilation, M, H, Dh).transpose(0, 2, 1, 3, 4).reshape(B, Tp, H, Dh)[:, :T]
    lse = lse.transpose(0, 1, 2, 4, 3).reshape(B, dilation, M, H).transpose(0, 2, 1, 3).reshape(B, Tp, H)[:, :T]
    return o, lse


def dilated_attention_mixture(p, q_norm, k_norm, pos):
    B, T, _ = p.shape
    cols = split_cols(p, (DIL_WIDTH,) * (3 * N_DIL))
    outs, lses = [], []
    for g, (window, dilation) in enumerate(DIL_PATTERNS):
        q, k, v = (c.reshape(B, T, DIL_HEADS, HEAD_DIM) for c in cols[3 * g:3 * g + 3])
        q = apply_rope(rms_norm(q, q_norm[g]), pos, ROPE_DIM, ROPE_THETA)
        k = apply_rope(rms_norm(k, k_norm[g]), pos, ROPE_DIM, ROPE_THETA)
        o, lse = dilated_window_attention(q, k, v, window, dilation)
        outs.append(o)
        lses.append(lse)
    weights = jax.nn.softmax(jnp.stack(lses), axis=0)
    o = jnp.einsum('gbth,gbthd->bthd', weights, jnp.stack(outs))
    return o.reshape(B, T, DIL_WIDTH).astype(p.dtype)


def retention_chunkwise(q, k, v, log_gamma):
    B, T, H, dk = q.shape
    dv = v.shape[-1]
    C = RET_CHUNK
    Tp = -(-T // C) * C
    nc = Tp // C
    chunk = lambda t: jnp.pad(t, ((0, 0), (0, Tp - T), (0, 0), (0, 0))).reshape(B, nc, C, H, t.shape[-1])
    qc, kc, vc = chunk(q), chunk(k), chunk(v)
    j = jnp.arange(C, dtype=F32)
    diff = j[:, None] - j[None, :]
    decay_in = jnp.where(diff >= 0, jnp.exp(log_gamma[:, None, None] * jnp.maximum(diff, 0.0)), 0.0)
    s = jnp.einsum('bcihd,bcjhd->bchij', qc, kc) * decay_in
    inner = jnp.einsum('bchij,bcjhe->bcihe', s, vc)
    k_dec = jnp.exp(log_gamma[None, :] * (C - 1 - j)[:, None])
    kv = jnp.einsum('bcjhd,jh,bcjhe->bchde', kc, k_dec, vc)
    gamma_c = jnp.exp(log_gamma * C)[None, :, None, None]

    def step(S, kv_c):
        return gamma_c * S + kv_c, S

    _, S_prev = lax.scan(step, jnp.zeros((B, H, dk, dv), F32), jnp.moveaxis(kv, 1, 0))
    S_prev = jnp.moveaxis(S_prev, 0, 1)
    q_dec = jnp.exp(log_gamma[None, :] * (j + 1.0)[:, None])
    cross = jnp.einsum('bcihd,ih,bchde->bcihe', qc, q_dec, S_prev)
    return (inner + cross).reshape(B, Tp, H, dv)[:, :T]


def multiscale_retention(p, gain, pos):
    B, T, _ = p.shape
    q, k, v, g = split_cols(p, RET_COLS)
    q = apply_rope(q.reshape(B, T, RET_HEADS, RET_QK_DIM), pos, RET_QK_DIM, RET_ROPE_BASE)
    k = apply_rope(k.reshape(B, T, RET_HEADS, RET_QK_DIM), pos, RET_QK_DIM, RET_ROPE_BASE)
    v = v.reshape(B, T, RET_HEADS, RET_V_DIM)
    log_gamma = jnp.log(1.0 - 2.0 ** (-5.0 - jnp.arange(RET_HEADS, dtype=F32)))
    y = retention_chunkwise(q.astype(F32), k.astype(F32) * (RET_QK_DIM ** -0.5), v.astype(F32), log_gamma)
    y = rms_norm(y, gain.reshape(RET_HEADS, RET_V_DIM))
    return jax.nn.silu(g) * y.reshape(B, T, RET_WIDTH).astype(g.dtype)


def hybrid_mixer(u, w_in, rw_mu, rw_w0, rw_w2, rw_a0, rw_a2, rw_g2, rw_k_k, rw_k_a, rw_r_k,
                 rw_ln_w, rw_ln_b, dil_q_norm, dil_k_norm, ret_norm,
                 w_branch_rwkv, w_branch_dil, w_branch_ret, w_out):
    B, T, _ = u.shape
    pos = jnp.arange(T)
    proj = u @ w_in
    p_rw, p_dil, p_ret, p_gate = split_cols(proj, (RW_IN_COLS, DIL_IN_COLS, RET_IN_COLS, GATE_IN_COLS))
    y_a = rwkv7_time_mix(p_rw, rw_mu, rw_w0, rw_w2, rw_a0, rw_a2, rw_g2, rw_k_k, rw_k_a, rw_r_k, rw_ln_w, rw_ln_b)
    y_b = dilated_attention_mixture(p_dil, dil_q_norm, dil_k_norm, pos)
    y_c = multiscale_retention(p_ret, ret_norm, pos)
    gates = jax.nn.sigmoid(p_gate.reshape(B, T, N_BRANCHES, D_MODEL))
    merged = (gates[:, :, 0] * (y_a @ w_branch_rwkv)
              + gates[:, :, 1] * (y_b @ w_branch_dil)
              + gates[:, :, 2] * (y_c @ w_branch_ret))
    return merged @ w_out


def memory_cross_attention(hn, mn, wq, wkv, q_norm, k_norm, wo):
    B, T, _ = hn.shape
    Ml = mn.shape[1]
    q = (hn @ wq).reshape(B, T, XA_HEADS, XA_HEAD_DIM)
    k, v = jnp.split(mn @ wkv, 2, axis=-1)
    k = k.reshape(B, Ml, XA_HEADS, XA_HEAD_DIM)
    v = v.reshape(B, Ml, XA_HEADS, XA_HEAD_DIM)
    q = rms_norm(q, q_norm)
    k = rms_norm(k, k_norm)
    s = jnp.einsum('bthd,bmhd->bhtm', q, k, preferred_element_type=F32) * (XA_HEAD_DIM ** -0.5)
    pr = jax.nn.softmax(s, axis=-1)
    o = jnp.einsum('bhtm,bmhd->bthd', pr.astype(v.dtype), v)
    return o.reshape(B, T, D_MODEL) @ wo


def setup_inputs(seed: int = 0) -> dict:
    key = jax.random.key(seed)
    ks = iter(list(jax.random.split(key, 64)))
    L, D = DEPTH, D_MODEL

    def nrm(shape, scale):
        return jax.random.normal(next(ks), shape, F32) * scale

    def gain(shape):
        return 1.0 + 0.02 * jax.random.normal(next(ks), shape, F32)

    def unif(shape, lo, hi):
        return jax.random.uniform(next(ks), shape, F32, lo, hi)

    return {
        'x': nrm((BATCH, SEQ, D), 1.0),
        'mem': nrm((BATCH, MEM_LEN, D), 1.0),
        'norm_ffn1': gain((L, D)),
        'ffn1_w13': nrm((L, D, 2 * D_FF), D ** -0.5),
        'ffn1_w2': nrm((L, D_FF, D), D_FF ** -0.5),
        'norm_mix': gain((L, D)),
        'w_in': nrm((L, D, IN_COLS), D ** -0.5),
        'rw_mu': unif((L, RW_IN_COLS), 0.0, 1.0),
        'rw_w0': unif((L, RW_WIDTH), -6.0, 0.0),
        'rw_w2': nrm((L, RW_DECAY_LORA, RW_WIDTH), 0.5 * RW_DECAY_LORA ** -0.5),
        'rw_a0': nrm((L, RW_WIDTH), 0.1),
        'rw_a2': nrm((L, RW_AAA_LORA, RW_WIDTH), 0.5 * RW_AAA_LORA ** -0.5),
        'rw_g2': nrm((L, RW_GATE_LORA, RW_WIDTH), RW_GATE_LORA ** -0.5),
        'rw_k_k': 0.85 + 0.02 * jax.random.normal(next(ks), (L, RW_WIDTH), F32),
        'rw_k_a': gain((L, RW_WIDTH)),
        'rw_r_k': nrm((L, RW_WIDTH), 0.1),
        'rw_ln_w': gain((L, RW_WIDTH)),
        'rw_ln_b': nrm((L, RW_WIDTH), 0.02),
        'dil_q_norm': gain((L, N_DIL, HEAD_DIM)),
        'dil_k_norm': gain((L, N_DIL, HEAD_DIM)),
        'ret_norm': gain((L, RET_WIDTH)),
        'w_branch_rwkv': nrm((L, RW_WIDTH, D), RW_WIDTH ** -0.5),
        'w_branch_dil': nrm((L, DIL_WIDTH, D), DIL_WIDTH ** -0.5),
        'w_branch_ret': nrm((L, RET_WIDTH, D), RET_WIDTH ** -0.5),
        'w_out': nrm((L, D, D), D ** -0.5),
        'norm_xattn': gain((L, D)),
        'norm_mem': gain((L, D)),
        'xa_wq': nrm((L, D, D), D ** -0.5),
        'xa_wkv': nrm((L, D, 2 * D), D ** -0.5),
        'xa_q_norm': gain((L, XA_HEAD_DIM)),
        'xa_k_norm': gain((L, XA_HEAD_DIM)),
        'xa_wo': nrm((L, D, D), D ** -0.5),
        'norm_ffn2': gain((L, D)),
        'ffn2_w13': nrm((L, D, 2 * D_FF), D ** -0.5),
        'ffn2_w2': nrm((L, D_FF, D), D_FF ** -0.5),
    }


def reference(x, mem, norm_ffn1, ffn1_w13, ffn1_w2, norm_mix, w_in, rw_mu, rw_w0, rw_w2, rw_a0, rw_a2,
              rw_g2, rw_k_k, rw_k_a, rw_r_k, rw_ln_w, rw_ln_b, dil_q_norm, dil_k_norm, ret_norm,
              w_branch_rwkv, w_branch_dil, w_branch_ret, w_out, norm_xattn, norm_mem, xa_wq, xa_wkv,
              xa_q_norm, xa_k_norm, xa_wo, norm_ffn2, ffn2_w13, ffn2_w2):
    h = x
    for l in range(DEPTH):
        h = h + 0.5 * swiglu(rms_norm(h, norm_ffn1[l]), ffn1_w13[l], ffn1_w2[l])
        h = h + hybrid_mixer(rms_norm(h, norm_mix[l]), w_in[l], rw_mu[l], rw_w0[l], rw_w2[l], rw_a0[l],
                             rw_a2[l], rw_g2[l], rw_k_k[l], rw_k_a[l], rw_r_k[l], rw_ln_w[l], rw_ln_b[l],
                             dil_q_norm[l], dil_k_norm[l], ret_norm[l],
                             w_branch_rwkv[l], w_branch_dil[l], w_branch_ret[l], w_out[l])
        h = h + memory_cross_attention(rms_norm(h, norm_xattn[l]), rms_norm(mem, norm_mem[l]),
                                       xa_wq[l], xa_wkv[l], xa_q_norm[l], xa_k_norm[l], xa_wo[l])
        h = h + 0.5 * swiglu(rms_norm(h, norm_ffn2[l]), ffn2_w13[l], ffn2_w2[l])
    return h
```

```python
import functools
import math

import jax
import jax.numpy as jnp
import numpy as np
from jax import lax
from jax.experimental import pallas as pl
from jax.experimental.pallas import tpu as pltpu

F32 = jnp.float32
BF16 = jnp.bfloat16

D_MODEL = 1024
D_FF = 2816
EPS = 1e-6

RW_HEADS = 8
RW_HEAD_DIM = 64
RW_WIDTH = 512
RW_GN_EPS = 64e-5
RW_CHUNK = 64
RW_PAD_COLS = 1920

DIL_PATTERNS = ((128, 1), (512, 4), (2048, 16))
N_DIL = 3
DIL_HEADS = 8
HEAD_DIM = 64
DIL_WIDTH = 512
DIL_BLOCK = 128
ROPE_THETA = 500000.0
ROPE_DIM = 16

RET_HEADS = 4
RET_QK_DIM = 64
RET_V_DIM = 128
RET_CHUNK = 128
RET_ROPE_BASE = 10000.0
RET_WIDTH = 512

XA_HEADS = 4
XA_HEAD_DIM = 256
MEM_LEN = 256

VMEM_LIMIT = 56 * 1024 * 1024


def _cparams(*sem):
    return pltpu.CompilerParams(dimension_semantics=sem, vmem_limit_bytes=VMEM_LIMIT)


def _dot(a, b):
    return jnp.dot(a.astype(BF16), b.astype(BF16), preferred_element_type=F32)


def _split2(x):
    hi = x.astype(BF16)
    lo = (x - hi.astype(F32)).astype(BF16)
    return hi, lo


def _dot_lhs2(x, w_bf16):
    hi, lo = _split2(x)
    return (jnp.dot(hi, w_bf16, preferred_element_type=F32)
            + jnp.dot(lo, w_bf16, preferred_element_type=F32))


def _dot3(a, b):
    ah, al = _split2(a)
    bh, bl = _split2(b)
    return (jnp.dot(ah, bh, preferred_element_type=F32)
            + jnp.dot(ah, bl, preferred_element_type=F32)
            + jnp.dot(al, bh, preferred_element_type=F32))


def _rms(x, g):
    return x * lax.rsqrt(jnp.mean(x * x, axis=-1, keepdims=True) + EPS) * g


def _block_ones(n, width):
    i = lax.broadcasted_iota(jnp.int32, (n, n), 0) // width
    j = lax.broadcasted_iota(jnp.int32, (n, n), 1) // width
    return jnp.where(i == j, 1.0, 0.0).astype(BF16)


def _ffn_kernel(h_ref, g_ref, w1_ref, w3_ref, w2_ref, o_ref, nrm_ref, acc_ref):
    j = pl.program_id(1)

    @pl.when(j == 0)
    def _():
        nrm_ref[...] = _rms(h_ref[...], g_ref[...]).astype(BF16)
        acc_ref[...] = jnp.zeros_like(acc_ref)

    n = nrm_ref[...]
    a = jnp.dot(n, w1_ref[...], preferred_element_type=F32)
    b = jnp.dot(n, w3_ref[...], preferred_element_type=F32)
    mid = (a * jax.nn.sigmoid(a) * b).astype(BF16)
    acc_ref[...] += jnp.dot(mid, w2_ref[...], preferred_element_type=F32)

    @pl.when(j == pl.num_programs(1) - 1)
    def _():
        o_ref[...] = h_ref[...] + 0.5 * acc_ref[...]


def _ffn(h, g, w13, w2, tm=1024, tf=256):
    M, D = h.shape
    nff = D_FF // tf
    return pl.pallas_call(
        _ffn_kernel,
        grid=(M // tm, nff),
        in_specs=[
            pl.BlockSpec((tm, D), lambda i, j: (i, 0)),
            pl.BlockSpec((1, D), lambda i, j: (0, 0)),
            pl.BlockSpec((D, tf), lambda i, j: (0, j)),
            pl.BlockSpec((D, tf), lambda i, j: (0, j + nff)),
            pl.BlockSpec((tf, D), lambda i, j: (j, 0)),
        ],
        out_specs=pl.BlockSpec((tm, D), lambda i, j: (i, 0)),
        out_shape=jax.ShapeDtypeStruct((M, D), F32),
        scratch_shapes=[pltpu.VMEM((tm, D), BF16), pltpu.VMEM((tm, D), F32)],
        compiler_params=_cparams("parallel", "arbitrary"),
        name="ffn",
    )(h, g.reshape(1, D), w13, w13, w2)


def _nmm_kernel(h_ref, g_ref, w_ref, o_ref, nrm_ref):
    @pl.when(pl.program_id(1) == 0)
    def _():
        nrm_ref[...] = _rms(h_ref[...], g_ref[...]).astype(BF16)

    o_ref[...] = jnp.dot(nrm_ref[...], w_ref[...], preferred_element_type=F32)


def _norm_matmul(h, g, w, tm, tn):
    M, D = h.shape
    N = w.shape[1]
    return pl.pallas_call(
        _nmm_kernel,
        grid=(M // tm, N // tn),
        in_specs=[
            pl.BlockSpec((tm, D), lambda i, j: (i, 0)),
            pl.BlockSpec((1, D), lambda i, j: (0, 0)),
            pl.BlockSpec((D, tn), lambda i, j: (0, j)),
        ],
        out_specs=pl.BlockSpec((tm, tn), lambda i, j: (i, j)),
        out_shape=jax.ShapeDtypeStruct((M, N), F32),
        scratch_shapes=[pltpu.VMEM((tm, D), BF16)],
        compiler_params=_cparams("parallel", "arbitrary"),
        name="norm_matmul",
    )(h, g.reshape(1, D), w)


def _rwprep_kernel(p_ref, prev_ref, mu_ref, w0_ref, w2_ref, a0_ref, a2_ref, g2_ref,
                   kk_ref, ka_ref, rk_ref,
                   r_out, lw_out, k_out, v_out, a_out, b_out, g_out, bonus_out, *, tiles_per_seq):
    i = pl.program_id(0)
    p = p_ref[...]
    tm = p.shape[0]
    prev_row = prev_ref[7:8, :]
    prev_row = jnp.where(i % tiles_per_seq == 0, jnp.zeros_like(prev_row), prev_row)
    rows = lax.broadcasted_iota(jnp.int32, p.shape, 0)
    shifted = jnp.where(rows == 0, prev_row, pltpu.roll(p, 1, 0))
    pm = p + (shifted - p) * mu_ref[...]

    r = pm[:, 0:512]
    k = pm[:, 512:1024]
    v = pm[:, 1024:1536]
    wa = pm[:, 1536:1664]
    gd = pm[:, 1664:1920]

    w_in = w0_ref[...] + _dot3(jnp.tanh(wa), w2_ref[...])
    w_log = -jax.nn.softplus(-w_in) - 0.5
    lw = -jnp.exp(w_log)
    a_sig = jax.nn.sigmoid(a0_ref[...] + _dot3(wa, a2_ref[...]))
    g = _dot(jax.nn.sigmoid(gd), g2_ref[...])

    ones = _block_ones(RW_WIDTH, RW_HEAD_DIM)
    kk = k * kk_ref[...]
    ss = _dot_lhs2(kk * kk, ones)
    kk = kk / jnp.maximum(jnp.sqrt(ss), 1e-12)
    k2 = k * (1.0 + (a_sig - 1.0) * ka_ref[...])
    bonus = _dot_lhs2(r * k2 * rk_ref[...], ones) * v

    g_out[...] = g
    bonus_out[...] = bonus
    a_vec = -kk
    b_vec = kk * a_sig
    for h in range(RW_HEADS):
        sl = slice(h * RW_HEAD_DIM, (h + 1) * RW_HEAD_DIM)
        r_out[h] = r[:, sl]
        lw_out[h] = lw[:, sl]
        k_out[h] = k2[:, sl]
        v_out[h] = v[:, sl]
        a_out[h] = a_vec[:, sl]
        b_out[h] = b_vec[:, sl]


def _rwkv_prep(p_rw, B, T, mu, w0, w2p, a0, a2p, g2p, k_k, k_a, r_k, tm=512):
    M = B * T
    tps = T // tm
    row = lambda x: x.reshape(1, -1)
    full = lambda shape: pl.BlockSpec(shape, lambda i: (0,) * len(shape))
    head_spec = pl.BlockSpec((None, RW_HEADS, tm, RW_HEAD_DIM), lambda i: (i // tps, 0, i % tps, 0))
    head_shape = jax.ShapeDtypeStruct((B, RW_HEADS, T, RW_HEAD_DIM), F32)
    tok_spec = pl.BlockSpec((tm, RW_WIDTH), lambda i: (i, 0))
    tok_shape = jax.ShapeDtypeStruct((M, RW_WIDTH), F32)
    return pl.pallas_call(
        functools.partial(_rwprep_kernel, tiles_per_seq=tps),
        grid=(M // tm,),
        in_specs=[
            pl.BlockSpec((tm, RW_PAD_COLS), lambda i: (i, 0)),
            pl.BlockSpec((8, RW_PAD_COLS), lambda i: (jnp.maximum(i * (tm // 8) - 1, 0), 0)),
            full((1, RW_PAD_COLS)), full((1, RW_WIDTH)), full((128, RW_WIDTH)),
            full((1, RW_WIDTH)), full((128, RW_WIDTH)), full((256, RW_WIDTH)),
            full((1, RW_WIDTH)), full((1, RW_WIDTH)), full((1, RW_WIDTH)),
        ],
        out_specs=[head_spec] * 6 + [tok_spec] * 2,
        out_shape=[head_shape] * 6 + [tok_shape] * 2,
        compiler_params=_cparams("parallel"),
        name="rwkv_prep",
    )(p_rw, p_rw, row(mu), row(w0), w2p, row(a0), a2p, g2p, row(k_k), row(k_a), row(r_k))


def _bmm(a, b):
    return jnp.einsum('gij,gjk->gik', a.astype(BF16), b.astype(BF16), preferred_element_type=F32)


def _bmm3(a, b):
    ah, al = _split2(a)
    bh, bl = _split2(b)
    e = lambda x, y: jnp.einsum('gij,gjk->gik', x, y, preferred_element_type=F32)
    return e(ah, bh) + e(ah, bl) + e(al, bh)


def _bmm_nt3(a, b):
    ah, al = _split2(a)
    bh, bl = _split2(b)
    e = lambda x, y: jnp.einsum('gik,gjk->gij', x, y, preferred_element_type=F32)
    return e(ah, bh) + e(ah, bl) + e(al, bh)


def _rwkv_kernel(r_ref, lw_ref, k_ref, v_ref, a_ref, b_ref, y_ref, st_ref):
    C = RW_CHUNK
    H = RW_HEADS
    Tc = r_ref.shape[1]
    nc = Tc // C
    G = H * nc

    @pl.when(pl.program_id(1) == 0)
    def _():
        st_ref[...] = jnp.zeros_like(st_ref)

    shp = lambda ref: ref[...].reshape(G, C, RW_HEAD_DIM)
    r, lw, k, v, a, b = (shp(x) for x in (r_ref, lw_ref, k_ref, v_ref, a_ref, b_ref))

    ti = lax.broadcasted_iota(jnp.int32, (C, C), 0)
    tj = lax.broadcasted_iota(jnp.int32, (C, C), 1)
    lower = tj <= ti
    strict = tj < ti
    tri = jnp.broadcast_to(jnp.where(lower, 1.0, 0.0).astype(BF16)[None], (G, C, C))
    eye = jnp.where(ti == tj, 1.0, 0.0).astype(F32)

    l1 = lw.astype(BF16)
    rem = lw - l1.astype(F32)
    l2 = rem.astype(BF16)
    l3 = (rem - l2.astype(F32)).astype(BF16)
    e = lambda x, y: jnp.einsum('gij,gjk->gik', x, y, preferred_element_type=F32)
    L = e(tri, l1) + e(tri, l2) + e(tri, l3)
    Lprev = L - lw
    Ltot = L[:, C - 1:C, :]

    eL = jnp.exp(L)
    enL = jnp.exp(-L)
    eh = jnp.exp(Ltot - L)
    At = a * jnp.exp(Lprev)
    Rt = r * eL
    Bt = b * enL
    Kt = k * enL
    Bh = b * eh
    Kh = k * eh

    N = jnp.where(strict[None], _bmm_nt3(At, Bt), 0.0)
    AK = jnp.where(strict[None], _bmm_nt3(At, Kt), 0.0)
    RB = jnp.where(lower[None], _bmm_nt3(Rt, Bt), 0.0)
    RK = jnp.where(lower[None], _bmm_nt3(Rt, Kt), 0.0)

    X = eye[None] + N
    Pw = N
    for _ in range(int(math.log2(C)) - 1):
        Pw = _bmm3(Pw, Pw)
        X = X + _bmm3(X, Pw)

    Abar = _bmm3(X, At)
    W0 = _bmm3(X, _bmm3(AK, v))
    Y0 = _bmm3(RB, W0) + _bmm3(RK, v)
    Rbar = Rt + _bmm3(RB, Abar)
    BhT = jnp.swapaxes(Bh, 1, 2)
    KhT = jnp.swapaxes(Kh, 1, 2)
    Mtx = _bmm3(BhT, Abar) + eye[None] * jnp.exp(Ltot)
    G0 = _bmm3(BhT, W0) + _bmm3(KhT, v)

    hsplit = lambda x: x.reshape(H, nc, x.shape[1], x.shape[2])
    Rbar, Y0, Mtx, G0 = hsplit(Rbar), hsplit(Y0), hsplit(Mtx), hsplit(G0)
    St = st_ref[...]
    ys = []
    for c in range(nc):
        ys.append(_bmm3(Rbar[:, c], St) + Y0[:, c])
        St = _bmm3(Mtx[:, c], St) + G0[:, c]
    st_ref[...] = St
    y = jnp.concatenate(ys, axis=1)
    for h in range(H):
        y_ref[:, h * RW_HEAD_DIM:(h + 1) * RW_HEAD_DIM] = y[h]


def _rwkv_recurrence(r, lw, k, v, a, b, tc=256):
    B, H, T, Dh = r.shape
    spec = pl.BlockSpec((None, H, tc, Dh), lambda bi, c: (bi, 0, c, 0))
    return pl.pallas_call(
        _rwkv_kernel,
        grid=(B, T // tc),
        in_specs=[spec] * 6,
        out_specs=pl.BlockSpec((None, tc, H * Dh), lambda bi, c: (bi, c, 0)),
        out_shape=jax.ShapeDtypeStruct((B, T, H * Dh), F32),
        scratch_shapes=[pltpu.VMEM((H, Dh, Dh), F32)],
        compiler_params=_cparams("parallel", "arbitrary"),
        name="rwkv_recurrence",
    )(r, lw, k, v, a, b)


def _rope_tables(T, heads, head_dim, rot_dim, base):
    half = rot_dim // 2
    inv_freq = base ** (-jnp.arange(half, dtype=F32) / half)
    ang = jnp.arange(T).astype(F32)[:, None] * inv_freq[None, :]
    cos, sin = jnp.cos(ang), jnp.sin(ang)
    rest = head_dim - rot_dim
    c = jnp.concatenate([cos, cos, jnp.ones((T, rest), F32)], axis=1)
    s_lo = jnp.concatenate([-sin, jnp.zeros((T, half + rest), F32)], axis=1)
    s_hi = jnp.concatenate([jnp.zeros((T, half), F32), sin, jnp.zeros((T, rest), F32)], axis=1)
    tile = lambda x: jnp.tile(x, (1, heads))
    return tile(c), tile(s_lo), tile(s_hi)


def _apply_rope(x, c, s_lo, s_hi, half):
    n = x.shape[-1]
    return x * c + pltpu.roll(x, n - half, 1) * s_lo + pltpu.roll(x, half, 1) * s_hi


def _dil_kernel(q_ref, k_ref, v_ref, c_ref, slo_ref, shi_ref, gq_ref, gk_ref,
                o_ref, lse_ref, kp_ref, vp_ref):
    n = pl.program_id(2)
    Lb = DIL_BLOCK
    ones = _block_ones(DIL_WIDTH, HEAD_DIM)
    c, s_lo, s_hi = c_ref[...], slo_ref[...], shi_ref[...]

    def prep(x, g):
        ms = _dot_lhs2(x * x, ones) * (1.0 / HEAD_DIM)
        xn = x * lax.rsqrt(ms + EPS) * g
        return _apply_rope(xn, c, s_lo, s_hi, ROPE_DIM // 2)

    q = prep(q_ref[...], gq_ref[...]).astype(BF16)
    kc32 = prep(k_ref[...], gk_ref[...])
    vc32 = v_ref[...]
    kc = kc32.astype(BF16)
    vc = vc32.astype(BF16)

    @pl.when(n == 0)
    def _():
        kp_ref[...] = jnp.zeros_like(kp_ref)
        vp_ref[...] = jnp.zeros_like(vp_ref)

    kp = kp_ref[...].astype(BF16)
    vp = vp_ref[...].astype(BF16)

    qi = lax.broadcasted_iota(jnp.int32, (Lb, Lb), 0)
    ki = lax.broadcasted_iota(jnp.int32, (Lb, Lb), 1)
    valid_p = (ki >= qi) & (n > 0)
    valid_c = ki <= qi
    nt = (((1,), (1,)), ((), ()))
    scale = HEAD_DIM ** -0.5

    for h in range(DIL_HEADS):
        sl = slice(h * HEAD_DIM, (h + 1) * HEAD_DIM)
        qh = q[:, sl]
        sp = lax.dot_general(qh, kp[:, sl], nt, preferred_element_type=F32) * scale
        sc = lax.dot_general(qh, kc[:, sl], nt, preferred_element_type=F32) * scale
        sp = jnp.where(valid_p, sp, -jnp.inf)
        sc = jnp.where(valid_c, sc, -jnp.inf)
        m = jnp.maximum(jnp.max(sp, axis=-1, keepdims=True), jnp.max(sc, axis=-1, keepdims=True))
        ep = jnp.exp(sp - m)
        ec = jnp.exp(sc - m)
        den = jnp.sum(ep, axis=-1, keepdims=True) + jnp.sum(ec, axis=-1, keepdims=True)
        o = (jnp.dot(ep.astype(BF16), vp[:, sl], preferred_element_type=F32)
             + jnp.dot(ec.astype(BF16), vc[:, sl], preferred_element_type=F32)) / den
        o_ref[:, sl] = o
        lse_ref[:, sl] = jnp.broadcast_to(m + jnp.log(den), (Lb, HEAD_DIM))

    kp_ref[...] = kc32
    vp_ref[...] = vc32


def _dilated_attention(p_dil, B, T, group, tabs, gq, gk):
    d = DIL_PATTERNS[group][1]
    Lb = DIL_BLOCK
    Mr = T // d
    nb = Mr // Lb
    ncol = p_dil.shape[1] // DIL_WIDTH
    pv = p_dil.reshape(B, Mr, d * ncol * DIL_WIDTH)
    tv = [t.reshape(Mr, d * DIL_WIDTH) for t in tabs]

    def pspec(which):
        return pl.BlockSpec((None, Lb, DIL_WIDTH),
                            lambda b, c, n: (b, n, c * ncol + 3 * group + which))

    tspec = pl.BlockSpec((Lb, DIL_WIDTH), lambda b, c, n: (n, c))
    gspec = pl.BlockSpec((1, DIL_WIDTH), lambda b, c, n: (0, 0))
    ospec = pl.BlockSpec((None, Lb, DIL_WIDTH), lambda b, c, n: (b, n, c))
    oshape = jax.ShapeDtypeStruct((B, Mr, d * DIL_WIDTH), F32)
    o, lse = pl.pallas_call(
        _dil_kernel,
        grid=(B, d, nb),
        in_specs=[pspec(0), pspec(1), pspec(2), tspec, tspec, tspec, gspec, gspec],
        out_specs=[ospec, ospec],
        out_shape=[oshape, oshape],
        scratch_shapes=[pltpu.VMEM((Lb, DIL_WIDTH), F32), pltpu.VMEM((Lb, DIL_WIDTH), F32)],
        compiler_params=_cparams("parallel", "parallel", "arbitrary"),
        name=f"dilated_attention_{group}",
    )(pv, pv, pv, *tv, jnp.tile(gq, DIL_HEADS).reshape(1, -1), jnp.tile(gk, DIL_HEADS).reshape(1, -1))
    return o.reshape(B * T, DIL_WIDTH), lse.reshape(B * T, DIL_WIDTH)


def _ret_kernel(q_ref, k_ref, v_ref, g_ref, c_ref, slo_ref, shi_ref, gain_ref, o_ref, st_ref):
    C = RET_CHUNK

    @pl.when(pl.program_id(1) == 0)
    def _():
        st_ref[...] = jnp.zeros_like(st_ref)

    c, s_lo, s_hi = c_ref[...], slo_ref[...], shi_ref[...]
    q = _apply_rope(q_ref[...], c, s_lo, s_hi, RET_QK_DIM // 2)
    k = _apply_rope(k_ref[...], c, s_lo, s_hi, RET_QK_DIM // 2) * (RET_QK_DIM ** -0.5)
    v = v_ref[...]
    g = g_ref[...]
    gain = gain_ref[...]

    ji = lax.broadcasted_iota(jnp.int32, (C, C), 0)
    jj = lax.broadcasted_iota(jnp.int32, (C, C), 1)
    diff = (ji - jj).astype(F32)
    jcol = lax.broadcasted_iota(jnp.int32, (C, 1), 0).astype(F32)

    for h in range(RET_HEADS):
        lg = math.log(1.0 - 2.0 ** (-5.0 - h))
        qs = slice(h * RET_QK_DIM, (h + 1) * RET_QK_DIM)
        vs = slice(h * RET_V_DIM, (h + 1) * RET_V_DIM)
        qh, kh, vh = q[:, qs], k[:, qs], v[:, vs]
        decay_in = jnp.where(diff >= 0, jnp.exp(lg * jnp.maximum(diff, 0.0)), 0.0)
        s = lax.dot_general(qh.astype(BF16), kh.astype(BF16), (((1,), (1,)), ((), ())),
                            preferred_element_type=F32) * decay_in
        inner = _dot(s, vh)
        S = st_ref[h]
        cross = _dot(qh * jnp.exp(lg * (jcol + 1.0)), S)
        y = inner + cross
        kd = kh * jnp.exp(lg * (C - 1.0 - jcol))
        kv = lax.dot_general(kd.astype(BF16), vh.astype(BF16), (((0,), (0,)), ((), ())),
                             preferred_element_type=F32)
        st_ref[h] = math.exp(lg * C) * S + kv
        yn = y * lax.rsqrt(jnp.mean(y * y, axis=-1, keepdims=True) + EPS) * gain[:, vs]
        gh = g[:, vs]
        o_ref[:, vs] = gh * jax.nn.sigmoid(gh) * yn


def _retention(p_ret, B, T, tabs, gain):
    C = RET_CHUNK
    nc = T // C
    qk = RET_HEADS * RET_QK_DIM
    row = lambda b, c: b * nc + c
    tspec = pl.BlockSpec((C, qk), lambda b, c: (c, 0))
    return pl.pallas_call(
        _ret_kernel,
        grid=(B, nc),
        in_specs=[
            pl.BlockSpec((C, qk), lambda b, c: (row(b, c), 0)),
            pl.BlockSpec((C, qk), lambda b, c: (row(b, c), 1)),
            pl.BlockSpec((C, RET_WIDTH), lambda b, c: (row(b, c), 1)),
            pl.BlockSpec((C, RET_WIDTH), lambda b, c: (row(b, c), 2)),
            tspec, tspec, tspec,
            pl.BlockSpec((1, RET_WIDTH), lambda b, c: (0, 0)),
        ],
        out_specs=pl.BlockSpec((C, RET_WIDTH), lambda b, c: (row(b, c), 0)),
        out_shape=jax.ShapeDtypeStruct((B * T, RET_WIDTH), F32),
        scratch_shapes=[pltpu.VMEM((RET_HEADS, RET_QK_DIM, RET_V_DIM), F32)],
        compiler_params=_cparams("parallel", "arbitrary"),
        name="retention",
    )(p_ret, p_ret, p_ret, p_ret, *tabs, gain.reshape(1, -1))


def _merge_kernel(h_ref, gm_ref, wg_ref, y_ref, bonus_ref, g_ref, lnw_ref, lnb_ref,
                  o0_ref, o1_ref, o2_ref, l0_ref, l1_ref, l2_ref, yc_ref,
                  wa_ref, wb_ref, wc_ref, wo_ref, out_ref):
    h = h_ref[...]
    u = _rms(h, gm_ref[...]).astype(BF16)

    ones = _block_ones(RW_WIDTH, RW_HEAD_DIM)
    y = y_ref[...]
    mean = _dot_lhs2(y, ones) * (1.0 / RW_HEAD_DIM)
    yc = y - mean
    var = _dot_lhs2(yc * yc, ones) * (1.0 / RW_HEAD_DIM)
    ya = (yc * lax.rsqrt(var + RW_GN_EPS) * lnw_ref[...] + lnb_ref[...] + bonus_ref[...]) * g_ref[...]

    l0, l1, l2 = l0_ref[...], l1_ref[...], l2_ref[...]
    mx = jnp.maximum(jnp.maximum(l0, l1), l2)
    e0, e1, e2 = jnp.exp(l0 - mx), jnp.exp(l1 - mx), jnp.exp(l2 - mx)
    yb = (e0 * o0_ref[...] + e1 * o1_ref[...] + e2 * o2_ref[...]) / (e0 + e1 + e2)

    D = D_MODEL
    gate = lambda i: jax.nn.sigmoid(jnp.dot(u, wg_ref[:, i * D:(i + 1) * D], preferred_element_type=F32))
    merged = (gate(0) * _dot(ya, wa_ref[...])
              + gate(1) * _dot(yb, wb_ref[...])
              + gate(2) * _dot(yc_ref[...], wc_ref[...]))
    out_ref[...] = h + _dot(merged, wo_ref[...])


def _merge(h, g_mix, w_gate, y, bonus, g, ln_w, ln_b, o, lse, y_c, wa, wb, wc, wo, tm=256):
    M, D = h.shape
    tok = lambda w: pl.BlockSpec((tm, w), lambda i: (i, 0))
    full = lambda shape: pl.BlockSpec(shape, lambda i: (0, 0))
    W = RW_WIDTH
    return pl.pallas_call(
        _merge_kernel,
        grid=(M // tm,),
        in_specs=[tok(D), full((1, D)), full((D, 3 * D)),
                  tok(W), tok(W), tok(W), full((1, W)), full((1, W)),
                  tok(W), tok(W), tok(W), tok(W), tok(W), tok(W), tok(W),
                  full((W, D)), full((W, D)), full((W, D)), full((D, D))],
        out_specs=tok(D),
        out_shape=jax.ShapeDtypeStruct((M, D), F32),
        compiler_params=_cparams("parallel"),
        name="gated_merge",
    )(h, g_mix.reshape(1, D), w_gate, y, bonus, g, ln_w.reshape(1, W), ln_b.reshape(1, W),
      o[0], o[1], o[2], lse[0], lse[1], lse[2], y_c, wa, wb, wc, wo)


def _xattn_kernel(h_ref, gx_ref, wq_ref, kv_ref, qn_ref, kn_ref, wo_ref, out_ref, k_scr, v_scr):
    D = D_MODEL

    @pl.when(pl.program_id(1) == 0)
    def _():
        kv = kv_ref[...]
        for hd in range(XA_HEADS):
            sl = slice(hd * XA_HEAD_DIM, (hd + 1) * XA_HEAD_DIM)
            k_scr[:, sl] = _rms(kv[:, sl], kn_ref[...]).astype(BF16)
        v_scr[...] = kv[:, D:].astype(BF16)

    h = h_ref[...]
    hn = _rms(h, gx_ref[...]).astype(BF16)
    q = jnp.dot(hn, wq_ref[...], preferred_element_type=F32)
    outs = []
    for hd in range(XA_HEADS):
        sl = slice(hd * XA_HEAD_DIM, (hd + 1) * XA_HEAD_DIM)
        qh = _rms(q[:, sl], qn_ref[...]).astype(BF16)
        s = lax.dot_general(qh, k_scr[:, sl], (((1,), (1,)), ((), ())),
                            preferred_element_type=F32) * (XA_HEAD_DIM ** -0.5)
        m = jnp.max(s, axis=-1, keepdims=True)
        e = jnp.exp(s - m)
        pr = e / jnp.sum(e, axis=-1, keepdims=True)
        outs.append(jnp.dot(pr.astype(BF16), v_scr[:, sl], preferred_element_type=F32))
    o = jnp.concatenate(outs, axis=-1).astype(BF16)
    out_ref[...] = h + jnp.dot(o, wo_ref[...], preferred_element_type=F32)


def _cross_attention(h, B, T, g_x, wq, kv, q_norm, k_norm, wo, tm=512):
    M, D = h.shape
    tps = T // tm
    full = lambda shape: pl.BlockSpec(shape, lambda b, t: (0, 0))
    return pl.pallas_call(
        _xattn_kernel,
        grid=(B, tps),
        in_specs=[
            pl.BlockSpec((tm, D), lambda b, t: (b * tps + t, 0)),
            full((1, D)), full((D, D)),
            pl.BlockSpec((MEM_LEN, 2 * D), lambda b, t: (b, 0)),
            full((1, XA_HEAD_DIM)), full((1, XA_HEAD_DIM)), full((D, D)),
        ],
        out_specs=pl.BlockSpec((tm, D), lambda b, t: (b * tps + t, 0)),
        out_shape=jax.ShapeDtypeStruct((M, D), F32),
        scratch_shapes=[pltpu.VMEM((MEM_LEN, D), BF16), pltpu.VMEM((MEM_LEN, D), BF16)],
        compiler_params=_cparams("parallel", "arbitrary"),
        name="cross_attention",
    )(h, g_x.reshape(1, D), wq, kv, q_norm.reshape(1, -1), k_norm.reshape(1, -1), wo)


def _pad_rows(w, rows_before, total):
    return jnp.pad(w, ((rows_before, total - rows_before - w.shape[0]), (0, 0)))


def _layer(h, mem2, B, T, p, dil_tabs, ret_tabs):
    bf = lambda w: w.astype(BF16)
    h = _ffn(h, p['norm_ffn1'], bf(p['ffn1_w13']), bf(p['ffn1_w2']))

    w_in = p['w_in']
    rw_w = jnp.pad(w_in[:, :1824], ((0, 0), (0, RW_PAD_COLS - 1824)))
    mu = jnp.pad(p['rw_mu'], (0, RW_PAD_COLS - 1824))
    p_rw = _norm_matmul(h, p['norm_mix'], bf(rw_w), 512, 640)
    p_dil = _norm_matmul(h, p['norm_mix'], bf(w_in[:, 1824:6432]), 512, 1536)
    p_ret = _norm_matmul(h, p['norm_mix'], bf(w_in[:, 6432:7968]), 512, 768)

    w2p = _pad_rows(p['rw_w2'], 0, 128)
    a2p = _pad_rows(p['rw_a2'], 64, 128)
    g2p = _pad_rows(p['rw_g2'], 0, 256)
    r, lw, k, v, a, b, g, bonus = _rwkv_prep(p_rw, B, T, mu, p['rw_w0'], w2p, p['rw_a0'], a2p, g2p,
                                             p['rw_k_k'], p['rw_k_a'], p['rw_r_k'])
    y = _rwkv_recurrence(r, lw, k, v, a, b).reshape(B * T, RW_WIDTH)

    o, lse = [], []
    for grp in range(N_DIL):
        og, lg = _dilated_attention(p_dil, B, T, grp, dil_tabs, p['dil_q_norm'][grp], p['dil_k_norm'][grp])
        o.append(og)
        lse.append(lg)

    y_c = _retention(p_ret, B, T, ret_tabs, p['ret_norm'])

    h = _merge(h, p['norm_mix'], bf(w_in[:, 7968:]), y, bonus, g, p['rw_ln_w'], p['rw_ln_b'],
               o, lse, y_c, bf(p['w_branch_rwkv']), bf(p['w_branch_dil']), bf(p['w_branch_ret']),
               bf(p['w_out']))

    kv = _norm_matmul(mem2, p['norm_mem'], bf(p['xa_wkv']), MEM_LEN, 1024)
    h = _cross_attention(h, B, T, p['norm_xattn'], bf(p['xa_wq']), kv, p['xa_q_norm'], p['xa_k_norm'],
                         bf(p['xa_wo']))
    h = _ffn(h, p['norm_ffn2'], bf(p['ffn2_w13']), bf(p['ffn2_w2']))
    return h


_PARAM_NAMES = ('norm_ffn1', 'ffn1_w13', 'ffn1_w2', 'norm_mix', 'w_in', 'rw_mu', 'rw_w0', 'rw_w2', 'rw_a0',
                'rw_a2', 'rw_g2', 'rw_k_k', 'rw_k_a', 'rw_r_k', 'rw_ln_w', 'rw_ln_b', 'dil_q_norm',
                'dil_k_norm', 'ret_norm', 'w_branch_rwkv', 'w_branch_dil', 'w_branch_ret', 'w_out',
                'norm_xattn', 'norm_mem', 'xa_wq', 'xa_wkv', 'xa_q_norm', 'xa_k_norm', 'xa_wo',
                'norm_ffn2', 'ffn2_w13', 'ffn2_w2')


def kernel(x, mem, norm_ffn1, ffn1_w13, ffn1_w2, norm_mix, w_in, rw_mu, rw_w0, rw_w2, rw_a0, rw_a2, rw_g2, rw_k_k, rw_k_a, rw_r_k, rw_ln_w, rw_ln_b, dil_q_norm, dil_k_norm, ret_norm, w_branch_rwkv, w_branch_dil, w_branch_ret, w_out, norm_xattn, norm_mem, xa_wq, xa_wkv, xa_q_norm, xa_k_norm, xa_wo, norm_ffn2, ffn2_w13, ffn2_w2):
    params = dict(zip(_PARAM_NAMES, (norm_ffn1, ffn1_w13, ffn1_w2, norm_mix, w_in, rw_mu, rw_w0, rw_w2, rw_a0,
                                     rw_a2, rw_g2, rw_k_k, rw_k_a, rw_r_k, rw_ln_w, rw_ln_b, dil_q_norm,
                                     dil_k_norm, ret_norm, w_branch_rwkv, w_branch_dil, w_branch_ret, w_out,
                                     norm_xattn, norm_mem, xa_wq, xa_wkv, xa_q_norm, xa_k_norm, xa_wo,
                                     norm_ffn2, ffn2_w13, ffn2_w2)))
    B, T, D = x.shape
    assert D == D_MODEL and T % 2048 == 0 and mem.shape[1] == MEM_LEN
    depth = norm_ffn1.shape[0]
    dil_tabs = _rope_tables(T, DIL_HEADS, HEAD_DIM, ROPE_DIM, ROPE_THETA)
    ret_tabs = _rope_tables(T, RET_HEADS, RET_QK_DIM, RET_QK_DIM, RET_ROPE_BASE)
    h = x.reshape(B * T, D)
    mem2 = mem.reshape(B * MEM_LEN, D)
    for l in range(depth):
        h = _layer(h, mem2, B, T, {n: params[n][l] for n in _PARAM_NAMES}, dil_tabs, ret_tabs)
    return h.reshape(B, T, D)
```

```python
import functools
import math

import jax
import jax.numpy as jnp
import numpy as np
from jax import lax
from jax.experimental import pallas as pl
from jax.experimental.pallas import tpu as pltpu

F32 = jnp.float32
BF16 = jnp.bfloat16

D_MODEL = 1024
D_FF = 2816
EPS = 1e-6

RW_HEADS = 8
RW_HEAD_DIM = 64
RW_WIDTH = 512
RW_GN_EPS = 64e-5
RW_CHUNK = 64
RW_PAD_COLS = 1920

DIL_PATTERNS = ((128, 1), (512, 4), (2048, 16))
N_DIL = 3
DIL_HEADS = 8
HEAD_DIM = 64
DIL_WIDTH = 512
DIL_BLOCK = 128
DIL_BLOCKS_PER_STEP = 4
ROPE_THETA = 500000.0
ROPE_DIM = 16

RET_HEADS = 4
RET_QK_DIM = 64
RET_V_DIM = 128
RET_CHUNK = 128
RET_ROPE_BASE = 10000.0
RET_WIDTH = 512

XA_HEADS = 4
XA_HEAD_DIM = 256
MEM_LEN = 256

VMEM_LIMIT = 56 * 1024 * 1024
LANES = 128


def _cparams(*sem):
    return pltpu.CompilerParams(dimension_semantics=sem, vmem_limit_bytes=VMEM_LIMIT)


def _dot(a, b):
    return jnp.dot(a.astype(BF16), b.astype(BF16), preferred_element_type=F32)


def _split2(x):
    hi = x.astype(BF16)
    lo = (x - hi.astype(F32)).astype(BF16)
    return hi, lo


def _dot_lhs2(x, w_bf16):
    hi, lo = _split2(x)
    return (jnp.dot(hi, w_bf16, preferred_element_type=F32)
            + jnp.dot(lo, w_bf16, preferred_element_type=F32))


def _dot3(a, b):
    ah, al = _split2(a)
    bh, bl = _split2(b)
    return (jnp.dot(ah, bh, preferred_element_type=F32)
            + jnp.dot(ah, bl, preferred_element_type=F32)
            + jnp.dot(al, bh, preferred_element_type=F32))


def _rms(x, g):
    return x * lax.rsqrt(jnp.mean(x * x, axis=-1, keepdims=True) + EPS) * g


def _store_lane_groups(ref, x):
    for gl in range(ref.shape[0]):
        ref[gl] = x[:, gl * LANES:(gl + 1) * LANES]


def _block_ones(n, width):
    i = lax.broadcasted_iota(jnp.int32, (n, n), 0) // width
    j = lax.broadcasted_iota(jnp.int32, (n, n), 1) // width
    return jnp.where(i == j, 1.0, 0.0).astype(BF16)


def _ffn_kernel(h_ref, g_ref, w1_ref, w3_ref, w2_ref, o_ref, nrm_ref, acc_ref):
    j = pl.program_id(1)

    @pl.when(j == 0)
    def _():
        nrm_ref[...] = _rms(h_ref[...], g_ref[...]).astype(BF16)
        acc_ref[...] = jnp.zeros_like(acc_ref)

    n = nrm_ref[...]
    a = jnp.dot(n, w1_ref[...], preferred_element_type=F32)
    b = jnp.dot(n, w3_ref[...], preferred_element_type=F32)
    mid = (a * jax.nn.sigmoid(a) * b).astype(BF16)
    acc_ref[...] += jnp.dot(mid, w2_ref[...], preferred_element_type=F32)

    @pl.when(j == pl.num_programs(1) - 1)
    def _():
        o_ref[...] = h_ref[...] + 0.5 * acc_ref[...]


def _ffn(h, g, w13, w2, tm=1024, tf=256):
    M, D = h.shape
    nff = D_FF // tf
    return pl.pallas_call(
        _ffn_kernel,
        grid=(M // tm, nff),
        in_specs=[
            pl.BlockSpec((tm, D), lambda i, j: (i, 0)),
            pl.BlockSpec((1, D), lambda i, j: (0, 0)),
            pl.BlockSpec((D, tf), lambda i, j: (0, j)),
            pl.BlockSpec((D, tf), lambda i, j: (0, j + nff)),
            pl.BlockSpec((tf, D), lambda i, j: (j, 0)),
        ],
        out_specs=pl.BlockSpec((tm, D), lambda i, j: (i, 0)),
        out_shape=jax.ShapeDtypeStruct((M, D), F32),
        scratch_shapes=[pltpu.VMEM((tm, D), BF16), pltpu.VMEM((tm, D), F32)],
        compiler_params=_cparams("parallel", "arbitrary"),
        name="ffn",
    )(h, g.reshape(1, D), w13, w13, w2)


def _nmm_kernel(h_ref, g_ref, w_ref, o_ref, nrm_ref):
    @pl.when(pl.program_id(1) == 0)
    def _():
        nrm_ref[...] = _rms(h_ref[...], g_ref[...]).astype(BF16)

    o_ref[...] = jnp.dot(nrm_ref[...], w_ref[...], preferred_element_type=F32)


def _norm_matmul(h, g, w, tm, tn):
    M, D = h.shape
    N = w.shape[1]
    return pl.pallas_call(
        _nmm_kernel,
        grid=(M // tm, N // tn),
        in_specs=[
            pl.BlockSpec((tm, D), lambda i, j: (i, 0)),
            pl.BlockSpec((1, D), lambda i, j: (0, 0)),
            pl.BlockSpec((D, tn), lambda i, j: (0, j)),
        ],
        out_specs=pl.BlockSpec((tm, tn), lambda i, j: (i, j)),
        out_shape=jax.ShapeDtypeStruct((M, N), F32),
        scratch_shapes=[pltpu.VMEM((tm, D), BF16)],
        compiler_params=_cparams("parallel", "arbitrary"),
        name="norm_matmul",
    )(h, g.reshape(1, D), w)


def _rwprep_kernel(p_ref, prev_ref, mu_ref, w0_ref, w2_ref, a0_ref, a2_ref, g2_ref,
                   kk_ref, ka_ref, rk_ref,
                   r_out, lw_out, k_out, v_out, a_out, b_out, g_out, bonus_out, *, tiles_per_seq):
    i = pl.program_id(0)
    p = p_ref[...]
    tm = p.shape[0]
    prev_row = prev_ref[7:8, :]
    prev_row = jnp.where(i % tiles_per_seq == 0, jnp.zeros_like(prev_row), prev_row)
    rows = lax.broadcasted_iota(jnp.int32, p.shape, 0)
    shifted = jnp.where(rows == 0, prev_row, pltpu.roll(p, 1, 0))
    pm = p + (shifted - p) * mu_ref[...]

    r = pm[:, 0:512]
    k = pm[:, 512:1024]
    v = pm[:, 1024:1536]
    wa = pm[:, 1536:1664]
    gd = pm[:, 1664:1920]

    w_in = w0_ref[...] + _dot3(jnp.tanh(wa), w2_ref[...])
    w_log = -jax.nn.softplus(-w_in) - 0.5
    lw = -jnp.exp(w_log)
    a_sig = jax.nn.sigmoid(a0_ref[...] + _dot3(wa, a2_ref[...]))
    g = _dot(jax.nn.sigmoid(gd), g2_ref[...])

    ones = _block_ones(RW_WIDTH, RW_HEAD_DIM)
    kk = k * kk_ref[...]
    ss = _dot_lhs2(kk * kk, ones)
    kk = kk / jnp.maximum(jnp.sqrt(ss), 1e-12)
    k2 = k * (1.0 + (a_sig - 1.0) * ka_ref[...])
    bonus = _dot_lhs2(r * k2 * rk_ref[...], ones) * v

    g_out[...] = g
    bonus_out[...] = bonus
    a_vec = -kk
    b_vec = kk * a_sig
    for h in range(RW_HEADS):
        sl = slice(h * RW_HEAD_DIM, (h + 1) * RW_HEAD_DIM)
        r_out[h] = r[:, sl]
        lw_out[h] = lw[:, sl]
        k_out[h] = k2[:, sl]
        v_out[h] = v[:, sl]
        a_out[h] = a_vec[:, sl]
        b_out[h] = b_vec[:, sl]


def _rwkv_prep(p_rw, B, T, mu, w0, w2p, a0, a2p, g2p, k_k, k_a, r_k, tm=512):
    M = B * T
    tps = T // tm
    row = lambda x: x.reshape(1, -1)
    full = lambda shape: pl.BlockSpec(shape, lambda i: (0,) * len(shape))
    head_spec = pl.BlockSpec((None, RW_HEADS, tm, RW_HEAD_DIM), lambda i: (i // tps, 0, i % tps, 0))
    head_shape = jax.ShapeDtypeStruct((B, RW_HEADS, T, RW_HEAD_DIM), F32)
    tok_spec = pl.BlockSpec((tm, RW_WIDTH), lambda i: (i, 0))
    tok_shape = jax.ShapeDtypeStruct((M, RW_WIDTH), F32)
    return pl.pallas_call(
        functools.partial(_rwprep_kernel, tiles_per_seq=tps),
        grid=(M // tm,),
        in_specs=[
            pl.BlockSpec((tm, RW_PAD_COLS), lambda i: (i, 0)),
            pl.BlockSpec((8, RW_PAD_COLS), lambda i: (jnp.maximum(i * (tm // 8) - 1, 0), 0)),
            full((1, RW_PAD_COLS)), full((1, RW_WIDTH)), full((128, RW_WIDTH)),
            full((1, RW_WIDTH)), full((128, RW_WIDTH)), full((256, RW_WIDTH)),
            full((1, RW_WIDTH)), full((1, RW_WIDTH)), full((1, RW_WIDTH)),
        ],
        out_specs=[head_spec] * 6 + [tok_spec] * 2,
        out_shape=[head_shape] * 6 + [tok_shape] * 2,
        compiler_params=_cparams("parallel"),
        name="rwkv_prep",
    )(p_rw, p_rw, row(mu), row(w0), w2p, row(a0), a2p, g2p, row(k_k), row(k_a), row(r_k))


def _bmm(a, b):
    return jnp.einsum('gij,gjk->gik', a.astype(BF16), b.astype(BF16), preferred_element_type=F32)


def _bmm3(a, b):
    ah, al = _split2(a)
    bh, bl = _split2(b)
    e = lambda x, y: jnp.einsum('gij,gjk->gik', x, y, preferred_element_type=F32)
    return e(ah, bh) + e(ah, bl) + e(al, bh)


def _bmm_nt3(a, b):
    ah, al = _split2(a)
    bh, bl = _split2(b)
    e = lambda x, y: jnp.einsum('gik,gjk->gij', x, y, preferred_element_type=F32)
    return e(ah, bh) + e(ah, bl) + e(al, bh)


def _rwkv_kernel(r_ref, lw_ref, k_ref, v_ref, a_ref, b_ref, y_ref, st_ref):
    C = RW_CHUNK
    H = RW_HEADS
    Tc = r_ref.shape[1]
    nc = Tc // C
    G = H * nc

    @pl.when(pl.program_id(1) == 0)
    def _():
        st_ref[...] = jnp.zeros_like(st_ref)

    shp = lambda ref: ref[...].reshape(G, C, RW_HEAD_DIM)
    r, lw, k, v, a, b = (shp(x) for x in (r_ref, lw_ref, k_ref, v_ref, a_ref, b_ref))

    ti = lax.broadcasted_iota(jnp.int32, (C, C), 0)
    tj = lax.broadcasted_iota(jnp.int32, (C, C), 1)
    lower = tj <= ti
    strict = tj < ti
    tri = jnp.broadcast_to(jnp.where(lower, 1.0, 0.0).astype(BF16)[None], (G, C, C))
    eye = jnp.where(ti == tj, 1.0, 0.0).astype(F32)

    l1 = lw.astype(BF16)
    rem = lw - l1.astype(F32)
    l2 = rem.astype(BF16)
    l3 = (rem - l2.astype(F32)).astype(BF16)
    e = lambda x, y: jnp.einsum('gij,gjk->gik', x, y, preferred_element_type=F32)
    L = e(tri, l1) + e(tri, l2) + e(tri, l3)
    Lprev = L - lw
    Ltot = L[:, C - 1:C, :]

    eL = jnp.exp(L)
    enL = jnp.exp(-L)
    eh = jnp.exp(Ltot - L)
    At = a * jnp.exp(Lprev)
    Rt = r * eL
    Bt = b * enL
    Kt = k * enL
    Bh = b * eh
    Kh = k * eh

    N = jnp.where(strict[None], _bmm_nt3(At, Bt), 0.0)
    AK = jnp.where(strict[None], _bmm_nt3(At, Kt), 0.0)
    RB = jnp.where(lower[None], _bmm_nt3(Rt, Bt), 0.0)
    RK = jnp.where(lower[None], _bmm_nt3(Rt, Kt), 0.0)

    X = eye[None] + N
    Pw = N
    for _ in range(int(math.log2(C)) - 1):
        Pw = _bmm3(Pw, Pw)
        X = X + _bmm3(X, Pw)

    Abar = _bmm3(X, At)
    W0 = _bmm3(X, _bmm3(AK, v))
    Y0 = _bmm3(RB, W0) + _bmm3(RK, v)
    Rbar = Rt + _bmm3(RB, Abar)
    BhT = jnp.swapaxes(Bh, 1, 2)
    KhT = jnp.swapaxes(Kh, 1, 2)
    Mtx = _bmm3(BhT, Abar) + eye[None] * jnp.exp(Ltot)
    G0 = _bmm3(BhT, W0) + _bmm3(KhT, v)

    hsplit = lambda x: x.reshape(H, nc, x.shape[1], x.shape[2])
    Rbar, Y0, Mtx, G0 = hsplit(Rbar), hsplit(Y0), hsplit(Mtx), hsplit(G0)
    St = st_ref[...]
    ys = []
    for c in range(nc):
        ys.append(_bmm3(Rbar[:, c], St) + Y0[:, c])
        St = _bmm3(Mtx[:, c], St) + G0[:, c]
    st_ref[...] = St
    y = jnp.concatenate(ys, axis=1)
    for h in range(H):
        y_ref[:, h * RW_HEAD_DIM:(h + 1) * RW_HEAD_DIM] = y[h]


def _rwkv_recurrence(r, lw, k, v, a, b, tc=256):
    B, H, T, Dh = r.shape
    spec = pl.BlockSpec((None, H, tc, Dh), lambda bi, c: (bi, 0, c, 0))
    return pl.pallas_call(
        _rwkv_kernel,
        grid=(B, T // tc),
        in_specs=[spec] * 6,
        out_specs=pl.BlockSpec((None, tc, H * Dh), lambda bi, c: (bi, c, 0)),
        out_shape=jax.ShapeDtypeStruct((B, T, H * Dh), F32),
        scratch_shapes=[pltpu.VMEM((H, Dh, Dh), F32)],
        compiler_params=_cparams("parallel", "arbitrary"),
        name="rwkv_recurrence",
    )(r, lw, k, v, a, b)


def _rope_tables(T, heads, head_dim, rot_dim, base):
    half = rot_dim // 2
    inv_freq = base ** (-jnp.arange(half, dtype=F32) / half)
    ang = jnp.arange(T).astype(F32)[:, None] * inv_freq[None, :]
    cos, sin = jnp.cos(ang), jnp.sin(ang)
    rest = head_dim - rot_dim
    c = jnp.concatenate([cos, cos, jnp.ones((T, rest), F32)], axis=1)
    s_lo = jnp.concatenate([-sin, jnp.zeros((T, half + rest), F32)], axis=1)
    s_hi = jnp.concatenate([jnp.zeros((T, half), F32), sin, jnp.zeros((T, rest), F32)], axis=1)
    tile = lambda x: jnp.tile(x, (1, heads))
    return tile(c), tile(s_lo), tile(s_hi)


def _apply_rope(x, c, s_lo, s_hi, half):
    n = x.shape[-1]
    return x * c + pltpu.roll(x, n - half, 1) * s_lo + pltpu.roll(x, half, 1) * s_hi


def _dilproj_kernel(h_ref, g_ref, w_ref, c_ref, slo_ref, shi_ref, gq_ref, gk_ref, o_ref, nrm_ref, tmp_ref,
                    *, dilation):
    j = pl.program_id(1)

    @pl.when(j == 0)
    def _():
        nrm_ref[...] = _rms(h_ref[...], g_ref[...]).astype(BF16)

    val = jnp.dot(nrm_ref[...], w_ref[...], preferred_element_type=F32)

    @pl.when(j < 2)
    def _():
        gain = jnp.where(j == 0, gq_ref[...], gk_ref[...])
        ms = _dot_lhs2(val * val, _block_ones(DIL_WIDTH, HEAD_DIM)) * (1.0 / HEAD_DIM)
        xn = val * lax.rsqrt(ms + EPS) * gain
        _store_lane_groups(tmp_ref, _apply_rope(xn, c_ref[...], slo_ref[...], shi_ref[...], ROPE_DIM // 2))

    @pl.when(j == 2)
    def _():
        _store_lane_groups(tmp_ref, val)

    rows = tmp_ref.shape[1] // dilation
    for c in range(dilation):
        for gl in range(tmp_ref.shape[0]):
            piece = tmp_ref[gl, pl.ds(c, rows, stride=dilation), :]
            o_ref[c, :, gl * LANES:(gl + 1) * LANES] = piece.astype(BF16)


def _dilated_projection(h, g_mix, w_qkv, B, T, group, tabs, gq, gk, tm=512):
    M, D = h.shape
    d = DIL_PATTERNS[group][1]
    tps = T // tm
    W = DIL_WIDTH
    tspec = pl.BlockSpec((tm, W), lambda i, j: (i % tps, 0))
    gspec = pl.BlockSpec((1, W), lambda i, j: (0, 0))
    return pl.pallas_call(
        functools.partial(_dilproj_kernel, dilation=d),
        grid=(M // tm, 3),
        in_specs=[
            pl.BlockSpec((tm, D), lambda i, j: (i, 0)),
            pl.BlockSpec((1, D), lambda i, j: (0, 0)),
            pl.BlockSpec((D, W), lambda i, j: (0, j)),
            tspec, tspec, tspec, gspec, gspec,
        ],
        out_specs=pl.BlockSpec((None, None, d, tm // d, W), lambda i, j: (j, i // tps, 0, i % tps, 0)),
        out_shape=jax.ShapeDtypeStruct((3, B, d, T // d, W), BF16),
        scratch_shapes=[pltpu.VMEM((tm, D), BF16), pltpu.VMEM((W // LANES, tm, LANES), F32)],
        compiler_params=_cparams("parallel", "arbitrary"),
        name=f"dilated_projection_{group}",
    )(h, g_mix.reshape(1, D), w_qkv, *tabs,
      jnp.tile(gq, DIL_HEADS).reshape(1, -1), jnp.tile(gk, DIL_HEADS).reshape(1, -1))


def _dilattn_kernel(q_ref, k_ref, v_ref, kprev_ref, vprev_ref, o_ref, lse_ref, *, blocks):
    i = pl.program_id(2)
    Lb = DIL_BLOCK
    PW = 2 * HEAD_DIM
    qi = lax.broadcasted_iota(jnp.int32, (2 * Lb, Lb), 0) % Lb
    ki = lax.broadcasted_iota(jnp.int32, (2 * Lb, Lb), 1)
    in_window = ki >= qi
    causal = ki <= qi
    lane = lax.broadcasted_iota(jnp.int32, (Lb, PW), 1)
    low = lane < HEAD_DIM
    ones = jnp.ones((Lb, PW), BF16)
    nt = (((1,), (1,)), ((), ()))
    scale = HEAD_DIM ** -0.5

    for blk in range(blocks):
        rows = slice(blk * Lb, (blk + 1) * Lb)
        if blk == 0:
            kp_all, vp_all = kprev_ref[...], vprev_ref[...]
            valid_p = in_window & (i > 0)
        else:
            prev = slice((blk - 1) * Lb, blk * Lb)
            kp_all, vp_all = k_ref[prev, :], v_ref[prev, :]
            valid_p = in_window
        for p in range(DIL_WIDTH // PW):
            ln = slice(p * PW, (p + 1) * PW)
            q2 = q_ref[rows, ln]
            zero = jnp.zeros_like(q2)
            qe = jnp.concatenate([jnp.where(low, q2, zero), jnp.where(low, zero, q2)], axis=0)
            kp, kc, vp, vc = kp_all[:, ln], k_ref[rows, ln], vp_all[:, ln], v_ref[rows, ln]
            sp = lax.dot_general(qe, kp, nt, preferred_element_type=F32) * scale
            sc = lax.dot_general(qe, kc, nt, preferred_element_type=F32) * scale
            sp = jnp.where(valid_p, sp, -jnp.inf)
            sc = jnp.where(causal, sc, -jnp.inf)
            m = jnp.maximum(jnp.max(sp, axis=-1, keepdims=True), jnp.max(sc, axis=-1, keepdims=True))
            ep = jnp.exp(sp - m).astype(BF16)
            ec = jnp.exp(sc - m).astype(BF16)
            den = jnp.dot(ep, ones, preferred_element_type=F32) + jnp.dot(ec, ones, preferred_element_type=F32)
            num = jnp.dot(ep, vp, preferred_element_type=F32) + jnp.dot(ec, vc, preferred_element_type=F32)
            o2 = num / den
            l2 = m + jnp.log(den)
            o_ref[rows, ln] = jnp.where(low, o2[:Lb], o2[Lb:])
            lse_ref[rows, ln] = jnp.where(low, l2[:Lb], l2[Lb:])


def _dilated_attention(qkv, blocks):
    _, B, d, Mr, W = qkv.shape
    Lb = DIL_BLOCK
    nb = Mr // Lb
    blocks = min(blocks, nb)
    tile = blocks * Lb

    def cur(which):
        return pl.BlockSpec((None, None, None, tile, W), lambda b, c, i: (which, b, c, i, 0))

    def prev(which):
        return pl.BlockSpec((None, None, None, Lb, W),
                            lambda b, c, i: (which, b, c, jnp.maximum(i * blocks - 1, 0), 0))

    ospec = pl.BlockSpec((None, None, tile, W), lambda b, c, i: (b, c, i, 0))
    oshape = jax.ShapeDtypeStruct((B, d, Mr, W), F32)
    return pl.pallas_call(
        functools.partial(_dilattn_kernel, blocks=blocks),
        grid=(B, d, nb // blocks),
        in_specs=[cur(0), cur(1), cur(2), prev(1), prev(2)],
        out_specs=[ospec, ospec],
        out_shape=[oshape, oshape],
        compiler_params=_cparams("parallel", "parallel", "parallel"),
        name=f"dilated_attention_d{d}",
    )(qkv, qkv, qkv, qkv, qkv)


def _ret_kernel(q_ref, k_ref, v_ref, g_ref, c_ref, slo_ref, shi_ref, gain_ref, o_ref, st_ref):
    C = RET_CHUNK

    @pl.when(pl.program_id(1) == 0)
    def _():
        st_ref[...] = jnp.zeros_like(st_ref)

    c, s_lo, s_hi = c_ref[...], slo_ref[...], shi_ref[...]
    q = _apply_rope(q_ref[...], c, s_lo, s_hi, RET_QK_DIM // 2)
    k = _apply_rope(k_ref[...], c, s_lo, s_hi, RET_QK_DIM // 2) * (RET_QK_DIM ** -0.5)
    v = v_ref[...]
    g = g_ref[...]
    gain = gain_ref[...]

    ji = lax.broadcasted_iota(jnp.int32, (C, C), 0)
    jj = lax.broadcasted_iota(jnp.int32, (C, C), 1)
    diff = (ji - jj).astype(F32)
    jcol = lax.broadcasted_iota(jnp.int32, (C, 1), 0).astype(F32)

    for h in range(RET_HEADS):
        lg = math.log(1.0 - 2.0 ** (-5.0 - h))
        qs = slice(h * RET_QK_DIM, (h + 1) * RET_QK_DIM)
        vs = slice(h * RET_V_DIM, (h + 1) * RET_V_DIM)
        qh, kh, vh = q[:, qs], k[:, qs], v[:, vs]
        decay_in = jnp.where(diff >= 0, jnp.exp(lg * jnp.maximum(diff, 0.0)), 0.0)
        s = lax.dot_general(qh.astype(BF16), kh.astype(BF16), (((1,), (1,)), ((), ())),
                            preferred_element_type=F32) * decay_in
        inner = _dot(s, vh)
        S = st_ref[h]
        cross = _dot(qh * jnp.exp(lg * (jcol + 1.0)), S)
        y = inner + cross
        kd = kh * jnp.exp(lg * (C - 1.0 - jcol))
        kv = lax.dot_general(kd.astype(BF16), vh.astype(BF16), (((0,), (0,)), ((), ())),
                             preferred_element_type=F32)
        st_ref[h] = math.exp(lg * C) * S + kv
        yn = y * lax.rsqrt(jnp.mean(y * y, axis=-1, keepdims=True) + EPS) * gain[:, vs]
        gh = g[:, vs]
        o_ref[:, vs] = gh * jax.nn.sigmoid(gh) * yn


def _retention(p_ret, B, T, tabs, gain):
    C = RET_CHUNK
    nc = T // C
    qk = RET_HEADS * RET_QK_DIM
    row = lambda b, c: b * nc + c
    tspec = pl.BlockSpec((C, qk), lambda b, c: (c, 0))
    return pl.pallas_call(
        _ret_kernel,
        grid=(B, nc),
        in_specs=[
            pl.BlockSpec((C, qk), lambda b, c: (row(b, c), 0)),
            pl.BlockSpec((C, qk), lambda b, c: (row(b, c), 1)),
            pl.BlockSpec((C, RET_WIDTH), lambda b, c: (row(b, c), 1)),
            pl.BlockSpec((C, RET_WIDTH), lambda b, c: (row(b, c), 2)),
            tspec, tspec, tspec,
            pl.BlockSpec((1, RET_WIDTH), lambda b, c: (0, 0)),
        ],
        out_specs=pl.BlockSpec((C, RET_WIDTH), lambda b, c: (row(b, c), 0)),
        out_shape=jax.ShapeDtypeStruct((B * T, RET_WIDTH), F32),
        scratch_shapes=[pltpu.VMEM((RET_HEADS, RET_QK_DIM, RET_V_DIM), F32)],
        compiler_params=_cparams("parallel", "arbitrary"),
        name="retention",
    )(p_ret, p_ret, p_ret, p_ret, *tabs, gain.reshape(1, -1))


def _merge_kernel(h_ref, gm_ref, wg_ref, y_ref, bonus_ref, g_ref, lnw_ref, lnb_ref,
                  o0_ref, o1_ref, o2_ref, l0_ref, l1_ref, l2_ref, yc_ref,
                  wa_ref, wb_ref, wc_ref, wo_ref, out_ref, *tok_scratch):
    h = h_ref[...]
    u = _rms(h, gm_ref[...]).astype(BF16)

    ones = _block_ones(RW_WIDTH, RW_HEAD_DIM)
    y = y_ref[...]
    mean = _dot_lhs2(y, ones) * (1.0 / RW_HEAD_DIM)
    yc = y - mean
    var = _dot_lhs2(yc * yc, ones) * (1.0 / RW_HEAD_DIM)
    ya = (yc * lax.rsqrt(var + RW_GN_EPS) * lnw_ref[...] + lnb_ref[...] + bonus_ref[...]) * g_ref[...]

    def token_order(ref, scr):
        dil, rows = ref.shape[0], ref.shape[1]
        if dil == 1:
            return ref[0]
        for c in range(dil):
            for gl in range(scr.shape[0]):
                scr[gl, pl.ds(c, rows, stride=dil), :] = ref[c, :, gl * LANES:(gl + 1) * LANES]
        return jnp.concatenate([scr[gl] for gl in range(scr.shape[0])], axis=-1)

    o0, l0 = o0_ref[0], l0_ref[0]
    o1, l1 = token_order(o1_ref, tok_scratch[0]), token_order(l1_ref, tok_scratch[1])
    o2, l2 = token_order(o2_ref, tok_scratch[2]), token_order(l2_ref, tok_scratch[3])
    mx = jnp.maximum(jnp.maximum(l0, l1), l2)
    e0, e1, e2 = jnp.exp(l0 - mx), jnp.exp(l1 - mx), jnp.exp(l2 - mx)
    yb = (e0 * o0 + e1 * o1 + e2 * o2) / (e0 + e1 + e2)

    D = D_MODEL
    gate = lambda i: jax.nn.sigmoid(jnp.dot(u, wg_ref[:, i * D:(i + 1) * D], preferred_element_type=F32))
    merged = (gate(0) * _dot(ya, wa_ref[...])
              + gate(1) * _dot(yb, wb_ref[...])
              + gate(2) * _dot(yc_ref[...], wc_ref[...]))
    out_ref[...] = h + _dot(merged, wo_ref[...])


def _merge(h, T, g_mix, w_gate, y, bonus, g, ln_w, ln_b, o, lse, y_c, wa, wb, wc, wo, tm=256):
    M, D = h.shape
    tps = T // tm
    tok = lambda w: pl.BlockSpec((tm, w), lambda i: (i, 0))
    full = lambda shape: pl.BlockSpec(shape, lambda i: (0, 0))
    W = RW_WIDTH

    def res(group):
        d = DIL_PATTERNS[group][1]
        return pl.BlockSpec((None, d, tm // d, W), lambda i: (i // tps, 0, i % tps, 0))

    return pl.pallas_call(
        _merge_kernel,
        grid=(M // tm,),
        in_specs=[tok(D), full((1, D)), full((D, 3 * D)),
                  tok(W), tok(W), tok(W), full((1, W)), full((1, W)),
                  res(0), res(1), res(2), res(0), res(1), res(2), tok(W),
                  full((W, D)), full((W, D)), full((W, D)), full((D, D))],
        out_specs=tok(D),
        out_shape=jax.ShapeDtypeStruct((M, D), F32),
        scratch_shapes=[pltpu.VMEM((W // LANES, tm, LANES), F32)] * 4,
        compiler_params=_cparams("parallel"),
        name="gated_merge",
    )(h, g_mix.reshape(1, D), w_gate, y, bonus, g, ln_w.reshape(1, W), ln_b.reshape(1, W),
      o[0], o[1], o[2], lse[0], lse[1], lse[2], y_c, wa, wb, wc, wo)


def _xattn_kernel(h_ref, gx_ref, wq_ref, kv_ref, qn_ref, kn_ref, wo_ref, out_ref, k_scr, v_scr):
    D = D_MODEL

    @pl.when(pl.program_id(1) == 0)
    def _():
        kv = kv_ref[...]
        for hd in range(XA_HEADS):
            sl = slice(hd * XA_HEAD_DIM, (hd + 1) * XA_HEAD_DIM)
            k_scr[:, sl] = _rms(kv[:, sl], kn_ref[...]).astype(BF16)
        v_scr[...] = kv[:, D:].astype(BF16)

    h = h_ref[...]
    hn = _rms(h, gx_ref[...]).astype(BF16)
    q = jnp.dot(hn, wq_ref[...], preferred_element_type=F32)
    outs = []
    for hd in range(XA_HEADS):
        sl = slice(hd * XA_HEAD_DIM, (hd + 1) * XA_HEAD_DIM)
        qh = _rms(q[:, sl], qn_ref[...]).astype(BF16)
        s = lax.dot_general(qh, k_scr[:, sl], (((1,), (1,)), ((), ())),
                            preferred_element_type=F32) * (XA_HEAD_DIM ** -0.5)
        m = jnp.max(s, axis=-1, keepdims=True)
        e = jnp.exp(s - m)
        pr = e / jnp.sum(e, axis=-1, keepdims=True)
        outs.append(jnp.dot(pr.astype(BF16), v_scr[:, sl], preferred_element_type=F32))
    o = jnp.concatenate(outs, axis=-1).astype(BF16)
    out_ref[...] = h + jnp.dot(o, wo_ref[...], preferred_element_type=F32)


def _cross_attention(h, B, T, g_x, wq, kv, q_norm, k_norm, wo, tm=512):
    M, D = h.shape
    tps = T // tm
    full = lambda shape: pl.BlockSpec(shape, lambda b, t: (0, 0))
    return pl.pallas_call(
        _xattn_kernel,
        grid=(B, tps),
        in_specs=[
            pl.BlockSpec((tm, D), lambda b, t: (b * tps + t, 0)),
            full((1, D)), full((D, D)),
            pl.BlockSpec((MEM_LEN, 2 * D), lambda b, t: (b, 0)),
            full((1, XA_HEAD_DIM)), full((1, XA_HEAD_DIM)), full((D, D)),
        ],
        out_specs=pl.BlockSpec((tm, D), lambda b, t: (b * tps + t, 0)),
        out_shape=jax.ShapeDtypeStruct((M, D), F32),
        scratch_shapes=[pltpu.VMEM((MEM_LEN, D), BF16), pltpu.VMEM((MEM_LEN, D), BF16)],
        compiler_params=_cparams("parallel", "arbitrary"),
        name="cross_attention",
    )(h, g_x.reshape(1, D), wq, kv, q_norm.reshape(1, -1), k_norm.reshape(1, -1), wo)


def _pad_rows(w, rows_before, total):
    return jnp.pad(w, ((rows_before, total - rows_before - w.shape[0]), (0, 0)))


def _layer(h, mem2, B, T, p, dil_tabs, ret_tabs):
    bf = lambda w: w.astype(BF16)
    h = _ffn(h, p['norm_ffn1'], bf(p['ffn1_w13']), bf(p['ffn1_w2']))

    w_in = p['w_in']
    rw_w = jnp.pad(w_in[:, :1824], ((0, 0), (0, RW_PAD_COLS - 1824)))
    mu = jnp.pad(p['rw_mu'], (0, RW_PAD_COLS - 1824))
    p_rw = _norm_matmul(h, p['norm_mix'], bf(rw_w), 512, 640)
    p_ret = _norm_matmul(h, p['norm_mix'], bf(w_in[:, 6432:7968]), 512, 768)

    w2p = _pad_rows(p['rw_w2'], 0, 128)
    a2p = _pad_rows(p['rw_a2'], 64, 128)
    g2p = _pad_rows(p['rw_g2'], 0, 256)
    r, lw, k, v, a, b, g, bonus = _rwkv_prep(p_rw, B, T, mu, p['rw_w0'], w2p, p['rw_a0'], a2p, g2p,
                                             p['rw_k_k'], p['rw_k_a'], p['rw_r_k'])
    y = _rwkv_recurrence(r, lw, k, v, a, b).reshape(B * T, RW_WIDTH)

    o, lse = [], []
    for grp in range(N_DIL):
        w_qkv = bf(w_in[:, 1824 + grp * 3 * DIL_WIDTH:1824 + (grp + 1) * 3 * DIL_WIDTH])
        qkv = _dilated_projection(h, p['norm_mix'], w_qkv, B, T, grp, dil_tabs,
                                  p['dil_q_norm'][grp], p['dil_k_norm'][grp])
        og, lg = _dilated_attention(qkv, DIL_BLOCKS_PER_STEP)
        o.append(og)
        lse.append(lg)

    y_c = _retention(p_ret, B, T, ret_tabs, p['ret_norm'])

    h = _merge(h, T, p['norm_mix'], bf(w_in[:, 7968:]), y, bonus, g, p['rw_ln_w'], p['rw_ln_b'],
               o, lse, y_c, bf(p['w_branch_rwkv']), bf(p['w_branch_dil']), bf(p['w_branch_ret']),
               bf(p['w_out']))

    kv = _norm_matmul(mem2, p['norm_mem'], bf(p['xa_wkv']), MEM_LEN, 1024)
    h = _cross_attention(h, B, T, p['norm_xattn'], bf(p['xa_wq']), kv, p['xa_q_norm'], p['xa_k_norm'],
                         bf(p['xa_wo']))
    h = _ffn(h, p['norm_ffn2'], bf(p['ffn2_w13']), bf(p['ffn2_w2']))
    return h


_PARAM_NAMES = ('norm_ffn1', 'ffn1_w13', 'ffn1_w2', 'norm_mix', 'w_in', 'rw_mu', 'rw_w0', 'rw_w2', 'rw_a0',
                'rw_a2', 'rw_g2', 'rw_k_k', 'rw_k_a', 'rw_r_k', 'rw_ln_w', 'rw_ln_b', 'dil_q_norm',
                'dil_k_norm', 'ret_norm', 'w_branch_rwkv', 'w_branch_dil', 'w_branch_ret', 'w_out',
                'norm_xattn', 'norm_mem', 'xa_wq', 'xa_wkv', 'xa_q_norm', 'xa_k_norm', 'xa_wo',
                'norm_ffn2', 'ffn2_w13', 'ffn2_w2')


def kernel(x, mem, norm_ffn1, ffn1_w13, ffn1_w2, norm_mix, w_in, rw_mu, rw_w0, rw_w2, rw_a0, rw_a2, rw_g2, rw_k_k, rw_k_a, rw_r_k, rw_ln_w, rw_ln_b, dil_q_norm, dil_k_norm, ret_norm, w_branch_rwkv, w_branch_dil, w_branch_ret, w_out, norm_xattn, norm_mem, xa_wq, xa_wkv, xa_q_norm, xa_k_norm, xa_wo, norm_ffn2, ffn2_w13, ffn2_w2):
    params = dict(zip(_PARAM_NAMES, (norm_ffn1, ffn1_w13, ffn1_w2, norm_mix, w_in, rw_mu, rw_w0, rw_w2, rw_a0,
                                     rw_a2, rw_g2, rw_k_k, rw_k_a, rw_r_k, rw_ln_w, rw_ln_b, dil_q_norm,
                                     dil_k_norm, ret_norm, w_branch_rwkv, w_branch_dil, w_branch_ret, w_out,
                                     norm_xattn, norm_mem, xa_wq, xa_wkv, xa_q_norm, xa_k_norm, xa_wo,
                                     norm_ffn2, ffn2_w13, ffn2_w2)))
    B, T, D = x.shape
    assert D == D_MODEL and T % 2048 == 0 and mem.shape[1] == MEM_LEN
    depth = norm_ffn1.shape[0]
    dil_tabs = _rope_tables(T, DIL_HEADS, HEAD_DIM, ROPE_DIM, ROPE_THETA)
    ret_tabs = _rope_tables(T, RET_HEADS, RET_QK_DIM, RET_QK_DIM, RET_ROPE_BASE)
    h = x.reshape(B * T, D)
    mem2 = mem.reshape(B * MEM_LEN, D)
    for l in range(depth):
        h = _layer(h, mem2, B, T, {n: params[n][l] for n in _PARAM_NAMES}, dil_tabs, ret_tabs)
    return h.reshape(B, T, D)
```

```python
import functools
import math

import jax
import jax.numpy as jnp
import numpy as np
from jax import lax
from jax.experimental import pallas as pl
from jax.experimental.pallas import tpu as pltpu

F32 = jnp.float32
BF16 = jnp.bfloat16

D_MODEL = 1024
D_FF = 2816
EPS = 1e-6

RW_HEADS = 8
RW_HEAD_DIM = 64
RW_WIDTH = 512
RW_GN_EPS = 64e-5
RW_CHUNK = 64
RW_PASSES_PROD = 1
RW_PASSES_INV = 1
RW_PASSES_APPLY = 1
RW_PASSES_STATE = 1
RW_PAD_COLS = 1920

DIL_PATTERNS = ((128, 1), (512, 4), (2048, 16))
N_DIL = 3
DIL_HEADS = 8
HEAD_DIM = 64
DIL_WIDTH = 512
DIL_BLOCK = 128
DIL_BLOCKS_PER_STEP = 4
ROPE_THETA = 500000.0
ROPE_DIM = 16

RET_HEADS = 4
RET_QK_DIM = 64
RET_V_DIM = 128
RET_CHUNK = 128
RET_ROPE_BASE = 10000.0
RET_WIDTH = 512

XA_HEADS = 4
XA_HEAD_DIM = 256
MEM_LEN = 256

VMEM_LIMIT = 56 * 1024 * 1024
LANES = 128


def _cparams(*sem):
    return pltpu.CompilerParams(dimension_semantics=sem, vmem_limit_bytes=VMEM_LIMIT)


def _dot(a, b):
    return jnp.dot(a.astype(BF16), b.astype(BF16), preferred_element_type=F32)


def _split2(x):
    hi = x.astype(BF16)
    lo = (x - hi.astype(F32)).astype(BF16)
    return hi, lo


def _dot_lhs2(x, w_bf16):
    hi, lo = _split2(x)
    return (jnp.dot(hi, w_bf16, preferred_element_type=F32)
            + jnp.dot(lo, w_bf16, preferred_element_type=F32))


def _dot3(a, b):
    ah, al = _split2(a)
    bh, bl = _split2(b)
    return (jnp.dot(ah, bh, preferred_element_type=F32)
            + jnp.dot(ah, bl, preferred_element_type=F32)
            + jnp.dot(al, bh, preferred_element_type=F32))


def _rms(x, g):
    return x * lax.rsqrt(jnp.mean(x * x, axis=-1, keepdims=True) + EPS) * g


def _store_lane_groups(ref, x):
    for gl in range(ref.shape[0]):
        ref[gl] = x[:, gl * LANES:(gl + 1) * LANES]


def _block_ones(n, width):
    i = lax.broadcasted_iota(jnp.int32, (n, n), 0) // width
    j = lax.broadcasted_iota(jnp.int32, (n, n), 1) // width
    return jnp.where(i == j, 1.0, 0.0).astype(BF16)


def _ffn_kernel(h_ref, g_ref, w1_ref, w3_ref, w2_ref, o_ref, nrm_ref, acc_ref):
    j = pl.program_id(1)

    @pl.when(j == 0)
    def _():
        nrm_ref[...] = _rms(h_ref[...], g_ref[...]).astype(BF16)
        acc_ref[...] = jnp.zeros_like(acc_ref)

    n = nrm_ref[...]
    a = jnp.dot(n, w1_ref[...], preferred_element_type=F32)
    b = jnp.dot(n, w3_ref[...], preferred_element_type=F32)
    mid = (a * jax.nn.sigmoid(a) * b).astype(BF16)
    acc_ref[...] += jnp.dot(mid, w2_ref[...], preferred_element_type=F32)

    @pl.when(j == pl.num_programs(1) - 1)
    def _():
        o_ref[...] = h_ref[...] + 0.5 * acc_ref[...]


def _ffn(h, g, w13, w2, tm=1024, tf=256):
    M, D = h.shape
    nff = D_FF // tf
    return pl.pallas_call(
        _ffn_kernel,
        grid=(M // tm, nff),
        in_specs=[
            pl.BlockSpec((tm, D), lambda i, j: (i, 0)),
            pl.BlockSpec((1, D), lambda i, j: (0, 0)),
            pl.BlockSpec((D, tf), lambda i, j: (0, j)),
            pl.BlockSpec((D, tf), lambda i, j: (0, j + nff)),
            pl.BlockSpec((tf, D), lambda i, j: (j, 0)),
        ],
        out_specs=pl.BlockSpec((tm, D), lambda i, j: (i, 0)),
        out_shape=jax.ShapeDtypeStruct((M, D), F32),
        scratch_shapes=[pltpu.VMEM((tm, D), BF16), pltpu.VMEM((tm, D), F32)],
        compiler_params=_cparams("parallel", "arbitrary"),
        name="ffn",
    )(h, g.reshape(1, D), w13, w13, w2)


def _nmm_kernel(h_ref, g_ref, w_ref, o_ref, nrm_ref):
    @pl.when(pl.program_id(1) == 0)
    def _():
        nrm_ref[...] = _rms(h_ref[...], g_ref[...]).astype(BF16)

    o_ref[...] = jnp.dot(nrm_ref[...], w_ref[...], preferred_element_type=F32)


def _norm_matmul(h, g, w, tm, tn):
    M, D = h.shape
    N = w.shape[1]
    return pl.pallas_call(
        _nmm_kernel,
        grid=(M // tm, N // tn),
        in_specs=[
            pl.BlockSpec((tm, D), lambda i, j: (i, 0)),
            pl.BlockSpec((1, D), lambda i, j: (0, 0)),
            pl.BlockSpec((D, tn), lambda i, j: (0, j)),
        ],
        out_specs=pl.BlockSpec((tm, tn), lambda i, j: (i, j)),
        out_shape=jax.ShapeDtypeStruct((M, N), F32),
        scratch_shapes=[pltpu.VMEM((tm, D), BF16)],
        compiler_params=_cparams("parallel", "arbitrary"),
        name="norm_matmul",
    )(h, g.reshape(1, D), w)


def _rwprep_kernel(p_ref, prev_ref, mu_ref, w0_ref, w2_ref, a0_ref, a2_ref, g2_ref,
                   kk_ref, ka_ref, rk_ref,
                   r_out, lw_out, k_out, v_out, a_out, b_out, g_out, bonus_out, *, tiles_per_seq):
    i = pl.program_id(0)
    p = p_ref[...]
    tm = p.shape[0]
    prev_row = prev_ref[7:8, :]
    prev_row = jnp.where(i % tiles_per_seq == 0, jnp.zeros_like(prev_row), prev_row)
    rows = lax.broadcasted_iota(jnp.int32, p.shape, 0)
    shifted = jnp.where(rows == 0, prev_row, pltpu.roll(p, 1, 0))
    pm = p + (shifted - p) * mu_ref[...]

    r = pm[:, 0:512]
    k = pm[:, 512:1024]
    v = pm[:, 1024:1536]
    wa = pm[:, 1536:1664]
    gd = pm[:, 1664:1920]

    w_in = w0_ref[...] + _dot3(jnp.tanh(wa), w2_ref[...])
    w_log = -jax.nn.softplus(-w_in) - 0.5
    lw = -jnp.exp(w_log)
    a_sig = jax.nn.sigmoid(a0_ref[...] + _dot3(wa, a2_ref[...]))
    g = _dot(jax.nn.sigmoid(gd), g2_ref[...])

    ones = _block_ones(RW_WIDTH, RW_HEAD_DIM)
    kk = k * kk_ref[...]
    ss = _dot_lhs2(kk * kk, ones)
    kk = kk / jnp.maximum(jnp.sqrt(ss), 1e-12)
    k2 = k * (1.0 + (a_sig - 1.0) * ka_ref[...])
    bonus = _dot_lhs2(r * k2 * rk_ref[...], ones) * v

    g_out[...] = g
    bonus_out[...] = bonus
    a_vec = -kk
    b_vec = kk * a_sig
    for h in range(RW_HEADS):
        sl = slice(h * RW_HEAD_DIM, (h + 1) * RW_HEAD_DIM)
        r_out[h] = r[:, sl]
        lw_out[h] = lw[:, sl]
        k_out[h] = k2[:, sl]
        v_out[h] = v[:, sl]
        a_out[h] = a_vec[:, sl]
        b_out[h] = b_vec[:, sl]


def _rwkv_prep(p_rw, B, T, mu, w0, w2p, a0, a2p, g2p, k_k, k_a, r_k, tm=512):
    M = B * T
    tps = T // tm
    row = lambda x: x.reshape(1, -1)
    full = lambda shape: pl.BlockSpec(shape, lambda i: (0,) * len(shape))
    head_spec = pl.BlockSpec((None, RW_HEADS, tm, RW_HEAD_DIM), lambda i: (i // tps, 0, i % tps, 0))
    head_shape = jax.ShapeDtypeStruct((B, RW_HEADS, T, RW_HEAD_DIM), F32)
    tok_spec = pl.BlockSpec((tm, RW_WIDTH), lambda i: (i, 0))
    tok_shape = jax.ShapeDtypeStruct((M, RW_WIDTH), F32)
    return pl.pallas_call(
        functools.partial(_rwprep_kernel, tiles_per_seq=tps),
        grid=(M // tm,),
        in_specs=[
            pl.BlockSpec((tm, RW_PAD_COLS), lambda i: (i, 0)),
            pl.BlockSpec((8, RW_PAD_COLS), lambda i: (jnp.maximum(i * (tm // 8) - 1, 0), 0)),
            full((1, RW_PAD_COLS)), full((1, RW_WIDTH)), full((128, RW_WIDTH)),
            full((1, RW_WIDTH)), full((128, RW_WIDTH)), full((256, RW_WIDTH)),
            full((1, RW_WIDTH)), full((1, RW_WIDTH)), full((1, RW_WIDTH)),
        ],
        out_specs=[head_spec] * 6 + [tok_spec] * 2,
        out_shape=[head_shape] * 6 + [tok_shape] * 2,
        compiler_params=_cparams("parallel"),
        name="rwkv_prep",
    )(p_rw, p_rw, row(mu), row(w0), w2p, row(a0), a2p, g2p, row(k_k), row(k_a), row(r_k))


def _bmm(a, b, passes, nt=False):
    spec = 'gik,gjk->gij' if nt else 'gij,gjk->gik'
    e = lambda x, y: jnp.einsum(spec, x, y, preferred_element_type=F32)
    if passes == 1:
        return e(a.astype(BF16), b.astype(BF16))
    ah, al = _split2(a)
    bh, bl = _split2(b)
    return e(ah, bh) + e(ah, bl) + e(al, bh)


def _rwkv_kernel(r_ref, lw_ref, k_ref, v_ref, a_ref, b_ref, y_ref, st_ref):
    C = RW_CHUNK
    H = RW_HEADS
    Tc = r_ref.shape[1]
    nc = Tc // C
    G = H * nc

    @pl.when(pl.program_id(1) == 0)
    def _():
        st_ref[...] = jnp.zeros_like(st_ref)

    shp = lambda ref: ref[...].reshape(G, C, RW_HEAD_DIM)
    r, lw, k, v, a, b = (shp(x) for x in (r_ref, lw_ref, k_ref, v_ref, a_ref, b_ref))

    ti = lax.broadcasted_iota(jnp.int32, (C, C), 0)
    tj = lax.broadcasted_iota(jnp.int32, (C, C), 1)
    tri = jnp.broadcast_to(jnp.where(tj <= ti, 1.0, 0.0).astype(BF16)[None], (G, C, C))
    eye = jnp.where(ti == tj, 1.0, 0.0).astype(F32)
    wi = lax.broadcasted_iota(jnp.int32, (C, 2 * C), 0)
    wj = lax.broadcasted_iota(jnp.int32, (C, 2 * C), 1)
    second = wj >= C
    wjc = jnp.where(second, wj - C, wj)

    l1 = lw.astype(BF16)
    rem = lw - l1.astype(F32)
    l2 = rem.astype(BF16)
    l3 = (rem - l2.astype(F32)).astype(BF16)
    e = lambda x, y: jnp.einsum('gij,gjk->gik', x, y, preferred_element_type=F32)
    L = e(tri, l1) + e(tri, l2) + e(tri, l3)
    Lprev = L - lw
    Ltot = L[:, C - 1:C, :]

    eL = jnp.exp(L)
    enL = jnp.exp(-L)
    eh = jnp.exp(Ltot - L)
    At = a * jnp.exp(Lprev)
    Rt = r * eL
    BKt = jnp.concatenate([b * enL, k * enL], axis=1)
    BKh = jnp.concatenate([b * eh, k * eh], axis=1)

    P4 = _bmm(jnp.concatenate([At, Rt], axis=1), BKt, RW_PASSES_PROD, nt=True)
    top, bot = P4[:, :C, :], P4[:, C:, :]
    N = jnp.where((tj < ti)[None], top[:, :, :C], 0.0)
    AKz = jnp.where((second & (wjc < wi))[None], top, 0.0)
    RBK = jnp.where((wjc <= wi)[None], bot, 0.0)
    RB = RBK[:, :, :C]

    def siblings(s):
        return ((ti // (2 * s)) == (tj // (2 * s))) & ((ti // s) != (tj // s))

    X = eye[None] + jnp.where(siblings(1)[None], N, 0.0)
    s_blk = 2
    while s_blk < C:
        XE = _bmm(X, jnp.where(siblings(s_blk)[None], N, 0.0), RW_PASSES_INV)
        X = X + _bmm(XE, X, RW_PASSES_INV)
        s_blk *= 2

    zv = jnp.concatenate([jnp.zeros_like(v), v], axis=1)
    Abar = _bmm(X, At, RW_PASSES_APPLY)
    W0 = _bmm(X, _bmm(AKz, zv, RW_PASSES_APPLY), RW_PASSES_APPLY)
    wv = jnp.concatenate([W0, v], axis=1)
    Y0 = _bmm(RBK, wv, RW_PASSES_APPLY)
    Rbar = Rt + _bmm(RB, Abar, RW_PASSES_APPLY)
    BKhT = jnp.swapaxes(BKh, 1, 2)
    Mtx = _bmm(BKhT[:, :, :C], Abar, RW_PASSES_APPLY) + eye[None] * jnp.exp(Ltot)
    G0 = _bmm(BKhT, wv, RW_PASSES_APPLY)

    hsplit = lambda x: x.reshape(H, nc, x.shape[1], x.shape[2])
    RM = hsplit(jnp.concatenate([Rbar, Mtx], axis=1))
    Y0, G0 = hsplit(Y0), hsplit(G0)
    St = st_ref[...]
    ys = []
    for c in range(nc):
        both = _bmm(RM[:, c], St, RW_PASSES_STATE)
        ys.append(both[:, :C] + Y0[:, c])
        St = both[:, C:] + G0[:, c]
    st_ref[...] = St
    y = jnp.concatenate(ys, axis=1)
    for h in range(H):
        y_ref[:, h * RW_HEAD_DIM:(h + 1) * RW_HEAD_DIM] = y[h]


def _rwkv_recurrence(r, lw, k, v, a, b, tc=256):
    B, H, T, Dh = r.shape
    spec = pl.BlockSpec((None, H, tc, Dh), lambda bi, c: (bi, 0, c, 0))
    return pl.pallas_call(
        _rwkv_kernel,
        grid=(B, T // tc),
        in_specs=[spec] * 6,
        out_specs=pl.BlockSpec((None, tc, H * Dh), lambda bi, c: (bi, c, 0)),
        out_shape=jax.ShapeDtypeStruct((B, T, H * Dh), F32),
        scratch_shapes=[pltpu.VMEM((H, Dh, Dh), F32)],
        compiler_params=_cparams("parallel", "arbitrary"),
        name="rwkv_recurrence",
    )(r, lw, k, v, a, b)


def _rope_tables(T, heads, head_dim, rot_dim, base):
    half = rot_dim // 2
    inv_freq = base ** (-jnp.arange(half, dtype=F32) / half)
    ang = jnp.arange(T).astype(F32)[:, None] * inv_freq[None, :]
    cos, sin = jnp.cos(ang), jnp.sin(ang)
    rest = head_dim - rot_dim
    c = jnp.concatenate([cos, cos, jnp.ones((T, rest), F32)], axis=1)
    s_lo = jnp.concatenate([-sin, jnp.zeros((T, half + rest), F32)], axis=1)
    s_hi = jnp.concatenate([jnp.zeros((T, half), F32), sin, jnp.zeros((T, rest), F32)], axis=1)
    tile = lambda x: jnp.tile(x, (1, heads))
    return tile(c), tile(s_lo), tile(s_hi)


def _apply_rope(x, c, s_lo, s_hi, half):
    n = x.shape[-1]
    return x * c + pltpu.roll(x, n - half, 1) * s_lo + pltpu.roll(x, half, 1) * s_hi


def _dilproj_kernel(h_ref, g_ref, w_ref, c_ref, slo_ref, shi_ref, gq_ref, gk_ref, o_ref, nrm_ref, tmp_ref,
                    *, dilation):
    j = pl.program_id(1)

    @pl.when(j == 0)
    def _():
        nrm_ref[...] = _rms(h_ref[...], g_ref[...]).astype(BF16)

    val = jnp.dot(nrm_ref[...], w_ref[...], preferred_element_type=F32)

    @pl.when(j < 2)
    def _():
        gain = jnp.where(j == 0, gq_ref[...], gk_ref[...])
        ms = _dot_lhs2(val * val, _block_ones(DIL_WIDTH, HEAD_DIM)) * (1.0 / HEAD_DIM)
        xn = val * lax.rsqrt(ms + EPS) * gain
        _store_lane_groups(tmp_ref, _apply_rope(xn, c_ref[...], slo_ref[...], shi_ref[...], ROPE_DIM // 2))

    @pl.when(j == 2)
    def _():
        _store_lane_groups(tmp_ref, val)

    rows = tmp_ref.shape[1] // dilation
    for c in range(dilation):
        for gl in range(tmp_ref.shape[0]):
            piece = tmp_ref[gl, pl.ds(c, rows, stride=dilation), :]
            o_ref[c, :, gl * LANES:(gl + 1) * LANES] = piece.astype(BF16)


def _dilated_projection(h, g_mix, w_qkv, B, T, group, tabs, gq, gk, tm=512):
    M, D = h.shape
    d = DIL_PATTERNS[group][1]
    tps = T // tm
    W = DIL_WIDTH
    tspec = pl.BlockSpec((tm, W), lambda i, j: (i % tps, 0))
    gspec = pl.BlockSpec((1, W), lambda i, j: (0, 0))
    return pl.pallas_call(
        functools.partial(_dilproj_kernel, dilation=d),
        grid=(M // tm, 3),
        in_specs=[
            pl.BlockSpec((tm, D), lambda i, j: (i, 0)),
            pl.BlockSpec((1, D), lambda i, j: (0, 0)),
            pl.BlockSpec((D, W), lambda i, j: (0, j)),
            tspec, tspec, tspec, gspec, gspec,
        ],
        out_specs=pl.BlockSpec((None, None, d, tm // d, W), lambda i, j: (j, i // tps, 0, i % tps, 0)),
        out_shape=jax.ShapeDtypeStruct((3, B, d, T // d, W), BF16),
        scratch_shapes=[pltpu.VMEM((tm, D), BF16), pltpu.VMEM((W // LANES, tm, LANES), F32)],
        compiler_params=_cparams("parallel", "arbitrary"),
        name=f"dilated_projection_{group}",
    )(h, g_mix.reshape(1, D), w_qkv, *tabs,
      jnp.tile(gq, DIL_HEADS).reshape(1, -1), jnp.tile(gk, DIL_HEADS).reshape(1, -1))


def _dilattn_kernel(q_ref, k_ref, v_ref, kprev_ref, vprev_ref, o_ref, lse_ref, *, blocks):
    i = pl.program_id(2)
    Lb = DIL_BLOCK
    PW = 2 * HEAD_DIM
    qi = lax.broadcasted_iota(jnp.int32, (2 * Lb, Lb), 0) % Lb
    ki = lax.broadcasted_iota(jnp.int32, (2 * Lb, Lb), 1)
    in_window = ki >= qi
    causal = ki <= qi
    lane = lax.broadcasted_iota(jnp.int32, (Lb, PW), 1)
    low = lane < HEAD_DIM
    ones = jnp.ones((Lb, PW), BF16)
    nt = (((1,), (1,)), ((), ()))
    scale = HEAD_DIM ** -0.5

    for blk in range(blocks):
        rows = slice(blk * Lb, (blk + 1) * Lb)
        if blk == 0:
            kp_all, vp_all = kprev_ref[...], vprev_ref[...]
            valid_p = in_window & (i > 0)
        else:
            prev = slice((blk - 1) * Lb, blk * Lb)
            kp_all, vp_all = k_ref[prev, :], v_ref[prev, :]
            valid_p = in_window
        for p in range(DIL_WIDTH // PW):
            ln = slice(p * PW, (p + 1) * PW)
            q2 = q_ref[rows, ln]
            zero = jnp.zeros_like(q2)
            qe = jnp.concatenate([jnp.where(low, q2, zero), jnp.where(low, zero, q2)], axis=0)
            kp, kc, vp, vc = kp_all[:, ln], k_ref[rows, ln], vp_all[:, ln], v_ref[rows, ln]
            sp = lax.dot_general(qe, kp, nt, preferred_element_type=F32) * scale
            sc = lax.dot_general(qe, kc, nt, preferred_element_type=F32) * scale
            sp = jnp.where(valid_p, sp, -jnp.inf)
            sc = jnp.where(causal, sc, -jnp.inf)
            m = jnp.maximum(jnp.max(sp, axis=-1, keepdims=True), jnp.max(sc, axis=-1, keepdims=True))
            ep = jnp.exp(sp - m).astype(BF16)
            ec = jnp.exp(sc - m).astype(BF16)
            den = jnp.dot(ep, ones, preferred_element_type=F32) + jnp.dot(ec, ones, preferred_element_type=F32)
            num = jnp.dot(ep, vp, preferred_element_type=F32) + jnp.dot(ec, vc, preferred_element_type=F32)
            o2 = num / den
            l2 = m + jnp.log(den)
            o_ref[rows, ln] = jnp.where(low, o2[:Lb], o2[Lb:])
            lse_ref[rows, ln] = jnp.where(low, l2[:Lb], l2[Lb:])


def _dilated_attention(qkv, blocks):
    _, B, d, Mr, W = qkv.shape
    Lb = DIL_BLOCK
    nb = Mr // Lb
    blocks = min(blocks, nb)
    tile = blocks * Lb

    def cur(which):
        return pl.BlockSpec((None, None, None, tile, W), lambda b, c, i: (which, b, c, i, 0))

    def prev(which):
        return pl.BlockSpec((None, None, None, Lb, W),
                            lambda b, c, i: (which, b, c, jnp.maximum(i * blocks - 1, 0), 0))

    ospec = pl.BlockSpec((None, None, tile, W), lambda b, c, i: (b, c, i, 0))
    oshape = jax.ShapeDtypeStruct((B, d, Mr, W), F32)
    return pl.pallas_call(
        functools.partial(_dilattn_kernel, blocks=blocks),
        grid=(B, d, nb // blocks),
        in_specs=[cur(0), cur(1), cur(2), prev(1), prev(2)],
        out_specs=[ospec, ospec],
        out_shape=[oshape, oshape],
        compiler_params=_cparams("parallel", "parallel", "parallel"),
        name=f"dilated_attention_d{d}",
    )(qkv, qkv, qkv, qkv, qkv)


def _ret_kernel(q_ref, k_ref, v_ref, g_ref, c_ref, slo_ref, shi_ref, gain_ref, o_ref, st_ref):
    C = RET_CHUNK

    @pl.when(pl.program_id(1) == 0)
    def _():
        st_ref[...] = jnp.zeros_like(st_ref)

    c, s_lo, s_hi = c_ref[...], slo_ref[...], shi_ref[...]
    q = _apply_rope(q_ref[...], c, s_lo, s_hi, RET_QK_DIM // 2)
    k = _apply_rope(k_ref[...], c, s_lo, s_hi, RET_QK_DIM // 2) * (RET_QK_DIM ** -0.5)
    v = v_ref[...]
    g = g_ref[...]
    gain = gain_ref[...]

    ji = lax.broadcasted_iota(jnp.int32, (C, C), 0)
    jj = lax.broadcasted_iota(jnp.int32, (C, C), 1)
    diff = (ji - jj).astype(F32)
    jcol = lax.broadcasted_iota(jnp.int32, (C, 1), 0).astype(F32)

    for h in range(RET_HEADS):
        lg = math.log(1.0 - 2.0 ** (-5.0 - h))
        qs = slice(h * RET_QK_DIM, (h + 1) * RET_QK_DIM)
        vs = slice(h * RET_V_DIM, (h + 1) * RET_V_DIM)
        qh, kh, vh = q[:, qs], k[:, qs], v[:, vs]
        decay_in = jnp.where(diff >= 0, jnp.exp(lg * jnp.maximum(diff, 0.0)), 0.0)
        s = lax.dot_general(qh.astype(BF16), kh.astype(BF16), (((1,), (1,)), ((), ())),
                            preferred_element_type=F32) * decay_in
        inner = _dot(s, vh)
        S = st_ref[h]
        cross = _dot(qh * jnp.exp(lg * (jcol + 1.0)), S)
        y = inner + cross
        kd = kh * jnp.exp(lg * (C - 1.0 - jcol))
        kv = lax.dot_general(kd.astype(BF16), vh.astype(BF16), (((0,), (0,)), ((), ())),
                             preferred_element_type=F32)
        st_ref[h] = math.exp(lg * C) * S + kv
        yn = y * lax.rsqrt(jnp.mean(y * y, axis=-1, keepdims=True) + EPS) * gain[:, vs]
        gh = g[:, vs]
        o_ref[:, vs] = gh * jax.nn.sigmoid(gh) * yn


def _retention(p_ret, B, T, tabs, gain):
    C = RET_CHUNK
    nc = T // C
    qk = RET_HEADS * RET_QK_DIM
    row = lambda b, c: b * nc + c
    tspec = pl.BlockSpec((C, qk), lambda b, c: (c, 0))
    return pl.pallas_call(
        _ret_kernel,
        grid=(B, nc),
        in_specs=[
            pl.BlockSpec((C, qk), lambda b, c: (row(b, c), 0)),
            pl.BlockSpec((C, qk), lambda b, c: (row(b, c), 1)),
            pl.BlockSpec((C, RET_WIDTH), lambda b, c: (row(b, c), 1)),
            pl.BlockSpec((C, RET_WIDTH), lambda b, c: (row(b, c), 2)),
            tspec, tspec, tspec,
            pl.BlockSpec((1, RET_WIDTH), lambda b, c: (0, 0)),
        ],
        out_specs=pl.BlockSpec((C, RET_WIDTH), lambda b, c: (row(b, c), 0)),
        out_shape=jax.ShapeDtypeStruct((B * T, RET_WIDTH), F32),
        scratch_shapes=[pltpu.VMEM((RET_HEADS, RET_QK_DIM, RET_V_DIM), F32)],
        compiler_params=_cparams("parallel", "arbitrary"),
        name="retention",
    )(p_ret, p_ret, p_ret, p_ret, *tabs, gain.reshape(1, -1))


def _merge_kernel(h_ref, gm_ref, wg_ref, y_ref, bonus_ref, g_ref, lnw_ref, lnb_ref,
                  o0_ref, o1_ref, o2_ref, l0_ref, l1_ref, l2_ref, yc_ref,
                  wa_ref, wb_ref, wc_ref, wo_ref, out_ref, *tok_scratch):
    h = h_ref[...]
    u = _rms(h, gm_ref[...]).astype(BF16)

    ones = _block_ones(RW_WIDTH, RW_HEAD_DIM)
    y = y_ref[...]
    mean = _dot_lhs2(y, ones) * (1.0 / RW_HEAD_DIM)
    yc = y - mean
    var = _dot_lhs2(yc * yc, ones) * (1.0 / RW_HEAD_DIM)
    ya = (yc * lax.rsqrt(var + RW_GN_EPS) * lnw_ref[...] + lnb_ref[...] + bonus_ref[...]) * g_ref[...]

    def token_order(ref, scr):
        dil, rows = ref.shape[0], ref.shape[1]
        if dil == 1:
            return ref[0]
        for c in range(dil):
            for gl in range(scr.shape[0]):
                scr[gl, pl.ds(c, rows, stride=dil), :] = ref[c, :, gl * LANES:(gl + 1) * LANES]
        return jnp.concatenate([scr[gl] for gl in range(scr.shape[0])], axis=-1)

    o0, l0 = o0_ref[0], l0_ref[0]
    o1, l1 = token_order(o1_ref, tok_scratch[0]), token_order(l1_ref, tok_scratch[1])
    o2, l2 = token_order(o2_ref, tok_scratch[2]), token_order(l2_ref, tok_scratch[3])
    mx = jnp.maximum(jnp.maximum(l0, l1), l2)
    e0, e1, e2 = jnp.exp(l0 - mx), jnp.exp(l1 - mx), jnp.exp(l2 - mx)
    yb = (e0 * o0 + e1 * o1 + e2 * o2) / (e0 + e1 + e2)

    D = D_MODEL
    gate = lambda i: jax.nn.sigmoid(jnp.dot(u, wg_ref[:, i * D:(i + 1) * D], preferred_element_type=F32))
    merged = (gate(0) * _dot(ya, wa_ref[...])
              + gate(1) * _dot(yb, wb_ref[...])
              + gate(2) * _dot(yc_ref[...], wc_ref[...]))
    out_ref[...] = h + _dot(merged, wo_ref[...])


def _merge(h, T, g_mix, w_gate, y, bonus, g, ln_w, ln_b, o, lse, y_c, wa, wb, wc, wo, tm=256):
    M, D = h.shape
    tps = T // tm
    tok = lambda w: pl.BlockSpec((tm, w), lambda i: (i, 0))
    full = lambda shape: pl.BlockSpec(shape, lambda i: (0, 0))
    W = RW_WIDTH

    def res(group):
        d = DIL_PATTERNS[group][1]
        return pl.BlockSpec((None, d, tm // d, W), lambda i: (i // tps, 0, i % tps, 0))

    return pl.pallas_call(
        _merge_kernel,
        grid=(M // tm,),
        in_specs=[tok(D), full((1, D)), full((D, 3 * D)),
                  tok(W), tok(W), tok(W), full((1, W)), full((1, W)),
                  res(0), res(1), res(2), res(0), res(1), res(2), tok(W),
                  full((W, D)), full((W, D)), full((W, D)), full((D, D))],
        out_specs=tok(D),
        out_shape=jax.ShapeDtypeStruct((M, D), F32),
        scratch_shapes=[pltpu.VMEM((W // LANES, tm, LANES), F32)] * 4,
        compiler_params=_cparams("parallel"),
        name="gated_merge",
    )(h, g_mix.reshape(1, D), w_gate, y, bonus, g, ln_w.reshape(1, W), ln_b.reshape(1, W),
      o[0], o[1], o[2], lse[0], lse[1], lse[2], y_c, wa, wb, wc, wo)


def _xattn_kernel(h_ref, gx_ref, wq_ref, kv_ref, qn_ref, kn_ref, wo_ref, out_ref, k_scr, v_scr):
    D = D_MODEL

    @pl.when(pl.program_id(1) == 0)
    def _():
        kv = kv_ref[...]
        for hd in range(XA_HEADS):
            sl = slice(hd * XA_HEAD_DIM, (hd + 1) * XA_HEAD_DIM)
            k_scr[:, sl] = _rms(kv[:, sl], kn_ref[...]).astype(BF16)
        v_scr[...] = kv[:, D:].astype(BF16)

    h = h_ref[...]
    hn = _rms(h, gx_ref[...]).astype(BF16)
    q = jnp.dot(hn, wq_ref[...], preferred_element_type=F32)
    outs = []
    for hd in range(XA_HEADS):
        sl = slice(hd * XA_HEAD_DIM, (hd + 1) * XA_HEAD_DIM)
        qh = _rms(q[:, sl], qn_ref[...]).astype(BF16)
        s = lax.dot_general(qh, k_scr[:, sl], (((1,), (1,)), ((), ())),
                            preferred_element_type=F32) * (XA_HEAD_DIM ** -0.5)
        m = jnp.max(s, axis=-1, keepdims=True)
        e = jnp.exp(s - m)
        pr = e / jnp.sum(e, axis=-1, keepdims=True)
        outs.append(jnp.dot(pr.astype(BF16), v_scr[:, sl], preferred_element_type=F32))
    o = jnp.concatenate(outs, axis=-1).astype(BF16)
    out_ref[...] = h + jnp.dot(o, wo_ref[...], preferred_element_type=F32)


def _cross_attention(h, B, T, g_x, wq, kv, q_norm, k_norm, wo, tm=512):
    M, D = h.shape
    tps = T // tm
    full = lambda shape: pl.BlockSpec(shape, lambda b, t: (0, 0))
    return pl.pallas_call(
        _xattn_kernel,
        grid=(B, tps),
        in_specs=[
            pl.BlockSpec((tm, D), lambda b, t: (b * tps + t, 0)),
            full((1, D)), full((D, D)),
            pl.BlockSpec((MEM_LEN, 2 * D), lambda b, t: (b, 0)),
            full((1, XA_HEAD_DIM)), full((1, XA_HEAD_DIM)), full((D, D)),
        ],
        out_specs=pl.BlockSpec((tm, D), lambda b, t: (b * tps + t, 0)),
        out_shape=jax.ShapeDtypeStruct((M, D), F32),
        scratch_shapes=[pltpu.VMEM((MEM_LEN, D), BF16), pltpu.VMEM((MEM_LEN, D), BF16)],
        compiler_params=_cparams("parallel", "arbitrary"),
        name="cross_attention",
    )(h, g_x.reshape(1, D), wq, kv, q_norm.reshape(1, -1), k_norm.reshape(1, -1), wo)


def _pad_rows(w, rows_before, total):
    return jnp.pad(w, ((rows_before, total - rows_before - w.shape[0]), (0, 0)))


def _layer(h, mem2, B, T, p, dil_tabs, ret_tabs):
    bf = lambda w: w.astype(BF16)
    h = _ffn(h, p['norm_ffn1'], bf(p['ffn1_w13']), bf(p['ffn1_w2']))

    w_in = p['w_in']
    rw_w = jnp.pad(w_in[:, :1824], ((0, 0), (0, RW_PAD_COLS - 1824)))
    mu = jnp.pad(p['rw_mu'], (0, RW_PAD_COLS - 1824))
    p_rw = _norm_matmul(h, p['norm_mix'], bf(rw_w), 512, 640)
    p_ret = _norm_matmul(h, p['norm_mix'], bf(w_in[:, 6432:7968]), 512, 768)

    w2p = _pad_rows(p['rw_w2'], 0, 128)
    a2p = _pad_rows(p['rw_a2'], 64, 128)
    g2p = _pad_rows(p['rw_g2'], 0, 256)
    r, lw, k, v, a, b, g, bonus = _rwkv_prep(p_rw, B, T, mu, p['rw_w0'], w2p, p['rw_a0'], a2p, g2p,
                                             p['rw_k_k'], p['rw_k_a'], p['rw_r_k'])
    y = _rwkv_recurrence(r, lw, k, v, a, b).reshape(B * T, RW_WIDTH)

    o, lse = [], []
    for grp in range(N_DIL):
        w_qkv = bf(w_in[:, 1824 + grp * 3 * DIL_WIDTH:1824 + (grp + 1) * 3 * DIL_WIDTH])
        qkv = _dilated_projection(h, p['norm_mix'], w_qkv, B, T, grp, dil_tabs,
                                  p['dil_q_norm'][grp], p['dil_k_norm'][grp])
        og, lg = _dilated_attention(qkv, DIL_BLOCKS_PER_STEP)
        o.append(og)
        lse.append(lg)

    y_c = _retention(p_ret, B, T, ret_tabs, p['ret_norm'])

    h = _merge(h, T, p['norm_mix'], bf(w_in[:, 7968:]), y, bonus, g, p['rw_ln_w'], p['rw_ln_b'],
               o, lse, y_c, bf(p['w_branch_rwkv']), bf(p['w_branch_dil']), bf(p['w_branch_ret']),
               bf(p['w_out']))

    kv = _norm_matmul(mem2, p['norm_mem'], bf(p['xa_wkv']), MEM_LEN, 1024)
    h = _cross_attention(h, B, T, p['norm_xattn'], bf(p['xa_wq']), kv, p['xa_q_norm'], p['xa_k_norm'],
                         bf(p['xa_wo']))
    h = _ffn(h, p['norm_ffn2'], bf(p['ffn2_w13']), bf(p['ffn2_w2']))
    return h


_PARAM_NAMES = ('norm_ffn1', 'ffn1_w13', 'ffn1_w2', 'norm_mix', 'w_in', 'rw_mu', 'rw_w0', 'rw_w2', 'rw_a0',
                'rw_a2', 'rw_g2', 'rw_k_k', 'rw_k_a', 'rw_r_k', 'rw_ln_w', 'rw_ln_b', 'dil_q_norm',
                'dil_k_norm', 'ret_norm', 'w_branch_rwkv', 'w_branch_dil', 'w_branch_ret', 'w_out',
                'norm_xattn', 'norm_mem', 'xa_wq', 'xa_wkv', 'xa_q_norm', 'xa_k_norm', 'xa_wo',
                'norm_ffn2', 'ffn2_w13', 'ffn2_w2')


def kernel(x, mem, norm_ffn1, ffn1_w13, ffn1_w2, norm_mix, w_in, rw_mu, rw_w0, rw_w2, rw_a0, rw_a2, rw_g2, rw_k_k, rw_k_a, rw_r_k, rw_ln_w, rw_ln_b, dil_q_norm, dil_k_norm, ret_norm, w_branch_rwkv, w_branch_dil, w_branch_ret, w_out, norm_xattn, norm_mem, xa_wq, xa_wkv, xa_q_norm, xa_k_norm, xa_wo, norm_ffn2, ffn2_w13, ffn2_w2):
    params = dict(zip(_PARAM_NAMES, (norm_ffn1, ffn1_w13, ffn1_w2, norm_mix, w_in, rw_mu, rw_w0, rw_w2, rw_a0,
                                     rw_a2, rw_g2, rw_k_k, rw_k_a, rw_r_k, rw_ln_w, rw_ln_b, dil_q_norm,
                                     dil_k_norm, ret_norm, w_branch_rwkv, w_branch_dil, w_branch_ret, w_out,
                                     norm_xattn, norm_mem, xa_wq, xa_wkv, xa_q_norm, xa_k_norm, xa_wo,
                                     norm_ffn2, ffn2_w13, ffn2_w2)))
    B, T, D = x.shape
    assert D == D_MODEL and T % 2048 == 0 and mem.shape[1] == MEM_LEN
    depth = norm_ffn1.shape[0]
    dil_tabs = _rope_tables(T, DIL_HEADS, HEAD_DIM, ROPE_DIM, ROPE_THETA)
    ret_tabs = _rope_tables(T, RET_HEADS, RET_QK_DIM, RET_QK_DIM, RET_ROPE_BASE)
    h = x.reshape(B * T, D)
    mem2 = mem.reshape(B * MEM_LEN, D)
    for l in range(depth):
        h = _layer(h, mem2, B, T, {n: params[n][l] for n in _PARAM_NAMES}, dil_tabs, ret_tabs)
    return h.reshape(B, T, D)
```

```python
import functools
import math

import jax
import jax.numpy as jnp
import numpy as np
from jax import lax
from jax.experimental import pallas as pl
from jax.experimental.pallas import tpu as pltpu

F32 = jnp.float32
BF16 = jnp.bfloat16

D_MODEL = 1024
D_FF = 2816
EPS = 1e-6

RW_HEADS = 8
RW_HEAD_DIM = 64
RW_WIDTH = 512
RW_GN_EPS = 64e-5
RW_CHUNK = 64
RW_PASSES_PROD = 1
RW_PASSES_INV = 1
RW_PASSES_APPLY = 1
RW_PASSES_STATE = 1
RW_PAD_COLS = 1920

DIL_PATTERNS = ((128, 1), (512, 4), (2048, 16))
N_DIL = 3
DIL_HEADS = 8
HEAD_DIM = 64
DIL_WIDTH = 512
DIL_BLOCK = 128
DIL_BLOCKS_PER_STEP = 4
ROPE_THETA = 500000.0
ROPE_DIM = 16

RET_HEADS = 4
RET_QK_DIM = 64
RET_V_DIM = 128
RET_CHUNK = 128
RET_ROPE_BASE = 10000.0
RET_WIDTH = 512

XA_HEADS = 4
XA_HEAD_DIM = 256
MEM_LEN = 256

VMEM_LIMIT = 56 * 1024 * 1024
LANES = 128


def _cparams(*sem):
    return pltpu.CompilerParams(dimension_semantics=sem, vmem_limit_bytes=VMEM_LIMIT)


def _dot(a, b):
    return jnp.dot(a.astype(BF16), b.astype(BF16), preferred_element_type=F32)


def _split2(x):
    hi = x.astype(BF16)
    lo = (x - hi.astype(F32)).astype(BF16)
    return hi, lo


def _dot3(a, b):
    ah, al = _split2(a)
    bh, bl = _split2(b)
    return (jnp.dot(ah, bh, preferred_element_type=F32)
            + jnp.dot(ah, bl, preferred_element_type=F32)
            + jnp.dot(al, bh, preferred_element_type=F32))


def _rms(x, g):
    return x * lax.rsqrt(jnp.mean(x * x, axis=-1, keepdims=True) + EPS) * g


def _store_lane_groups(ref, x):
    for gl in range(ref.shape[0]):
        ref[gl] = x[:, gl * LANES:(gl + 1) * LANES]


def _block_ones(n, width):
    i = lax.broadcasted_iota(jnp.int32, (n, n), 0) // width
    j = lax.broadcasted_iota(jnp.int32, (n, n), 1) // width
    return jnp.where(i == j, 1.0, 0.0).astype(BF16)


def _ffn_kernel(h_ref, g_ref, w1_ref, w3_ref, w2_ref, o_ref, nrm_ref, acc_ref):
    j = pl.program_id(1)

    @pl.when(j == 0)
    def _():
        nrm_ref[...] = _rms(h_ref[...], g_ref[...]).astype(BF16)
        acc_ref[...] = jnp.zeros_like(acc_ref)

    n = nrm_ref[...]
    a = jnp.dot(n, w1_ref[...], preferred_element_type=F32)
    b = jnp.dot(n, w3_ref[...], preferred_element_type=F32)
    mid = (a * jax.nn.sigmoid(a) * b).astype(BF16)
    acc_ref[...] += jnp.dot(mid, w2_ref[...], preferred_element_type=F32)

    @pl.when(j == pl.num_programs(1) - 1)
    def _():
        o_ref[...] = h_ref[...] + 0.5 * acc_ref[...]


def _ffn(h, g, w13, w2, tm=1024, tf=256):
    M, D = h.shape
    nff = D_FF // tf
    return pl.pallas_call(
        _ffn_kernel,
        grid=(M // tm, nff),
        in_specs=[
            pl.BlockSpec((tm, D), lambda i, j: (i, 0)),
            pl.BlockSpec((1, D), lambda i, j: (0, 0)),
            pl.BlockSpec((D, tf), lambda i, j: (0, j)),
            pl.BlockSpec((D, tf), lambda i, j: (0, j + nff)),
            pl.BlockSpec((tf, D), lambda i, j: (j, 0)),
        ],
        out_specs=pl.BlockSpec((tm, D), lambda i, j: (i, 0)),
        out_shape=jax.ShapeDtypeStruct((M, D), F32),
        scratch_shapes=[pltpu.VMEM((tm, D), BF16), pltpu.VMEM((tm, D), F32)],
        compiler_params=_cparams("parallel", "arbitrary"),
        name="ffn",
    )(h, g.reshape(1, D), w13, w13, w2)


def _nmm_kernel(h_ref, g_ref, w_ref, o_ref, nrm_ref):
    @pl.when(pl.program_id(1) == 0)
    def _():
        nrm_ref[...] = _rms(h_ref[...], g_ref[...]).astype(BF16)

    o_ref[...] = jnp.dot(nrm_ref[...], w_ref[...], preferred_element_type=F32)


def _norm_matmul(h, g, w, tm, tn):
    M, D = h.shape
    N = w.shape[1]
    return pl.pallas_call(
        _nmm_kernel,
        grid=(M // tm, N // tn),
        in_specs=[
            pl.BlockSpec((tm, D), lambda i, j: (i, 0)),
            pl.BlockSpec((1, D), lambda i, j: (0, 0)),
            pl.BlockSpec((D, tn), lambda i, j: (0, j)),
        ],
        out_specs=pl.BlockSpec((tm, tn), lambda i, j: (i, j)),
        out_shape=jax.ShapeDtypeStruct((M, N), F32),
        scratch_shapes=[pltpu.VMEM((tm, D), BF16)],
        compiler_params=_cparams("parallel", "arbitrary"),
        name="norm_matmul",
    )(h, g.reshape(1, D), w)


def _rwprep_kernel(p_ref, prev_ref, mu_ref, w0_ref, w2_ref, a0_ref, a2_ref, g2_ref,
                   kk_ref, ka_ref, rk_ref,
                   r_out, lw_out, k_out, v_out, a_out, b_out, g_out, bonus_out, *, tiles_per_seq):
    i = pl.program_id(0)
    p = p_ref[...]
    tm = p.shape[0]
    prev_row = prev_ref[7:8, :]
    prev_row = jnp.where(i % tiles_per_seq == 0, jnp.zeros_like(prev_row), prev_row)
    rows = lax.broadcasted_iota(jnp.int32, p.shape, 0)
    shifted = jnp.where(rows == 0, prev_row, pltpu.roll(p, 1, 0))
    pm = p + (shifted - p) * mu_ref[...]

    r = pm[:, 0:512]
    k = pm[:, 512:1024]
    v = pm[:, 1024:1536]
    wa = pm[:, 1536:1664]
    gd = pm[:, 1664:1920]

    w_in = w0_ref[...] + _dot3(jnp.tanh(wa), w2_ref[...])
    w_log = -jax.nn.softplus(-w_in) - 0.5
    lw = -jnp.exp(w_log)
    a_sig = jax.nn.sigmoid(a0_ref[...] + _dot3(wa, a2_ref[...]))
    g = _dot(jax.nn.sigmoid(gd), g2_ref[...])

    ones = _block_ones(RW_WIDTH, RW_HEAD_DIM)
    kk = k * kk_ref[...]
    ss = _dot(kk * kk, ones)
    kk = kk / jnp.maximum(jnp.sqrt(ss), 1e-12)
    k2 = k * (1.0 + (a_sig - 1.0) * ka_ref[...])
    bonus = _dot(r * k2 * rk_ref[...], ones) * v

    g_out[...] = g
    bonus_out[...] = bonus
    a_vec = -kk
    b_vec = kk * a_sig
    for h in range(RW_HEADS):
        sl = slice(h * RW_HEAD_DIM, (h + 1) * RW_HEAD_DIM)
        r_out[h] = r[:, sl]
        lw_out[h] = lw[:, sl]
        k_out[h] = k2[:, sl]
        v_out[h] = v[:, sl]
        a_out[h] = a_vec[:, sl]
        b_out[h] = b_vec[:, sl]


def _rwkv_prep(p_rw, B, T, mu, w0, w2p, a0, a2p, g2p, k_k, k_a, r_k, tm=512):
    M = B * T
    tps = T // tm
    row = lambda x: x.reshape(1, -1)
    full = lambda shape: pl.BlockSpec(shape, lambda i: (0,) * len(shape))
    head_spec = pl.BlockSpec((None, RW_HEADS, tm, RW_HEAD_DIM), lambda i: (i // tps, 0, i % tps, 0))
    head_shape = jax.ShapeDtypeStruct((B, RW_HEADS, T, RW_HEAD_DIM), F32)
    tok_spec = pl.BlockSpec((tm, RW_WIDTH), lambda i: (i, 0))
    tok_shape = jax.ShapeDtypeStruct((M, RW_WIDTH), F32)
    return pl.pallas_call(
        functools.partial(_rwprep_kernel, tiles_per_seq=tps),
        grid=(M // tm,),
        in_specs=[
            pl.BlockSpec((tm, RW_PAD_COLS), lambda i: (i, 0)),
            pl.BlockSpec((8, RW_PAD_COLS), lambda i: (jnp.maximum(i * (tm // 8) - 1, 0), 0)),
            full((1, RW_PAD_COLS)), full((1, RW_WIDTH)), full((128, RW_WIDTH)),
            full((1, RW_WIDTH)), full((128, RW_WIDTH)), full((256, RW_WIDTH)),
            full((1, RW_WIDTH)), full((1, RW_WIDTH)), full((1, RW_WIDTH)),
        ],
        out_specs=[head_spec] * 6 + [tok_spec] * 2,
        out_shape=[head_shape] * 6 + [tok_shape] * 2,
        compiler_params=_cparams("parallel"),
        name="rwkv_prep",
    )(p_rw, p_rw, row(mu), row(w0), w2p, row(a0), a2p, g2p, row(k_k), row(k_a), row(r_k))


def _bmm(a, b, passes, nt=False):
    spec = 'gik,gjk->gij' if nt else 'gij,gjk->gik'
    e = lambda x, y: jnp.einsum(spec, x, y, preferred_element_type=F32)
    if passes == 1:
        return e(a.astype(BF16), b.astype(BF16))
    ah, al = _split2(a)
    bh, bl = _split2(b)
    return e(ah, bh) + e(ah, bl) + e(al, bh)


def _rwkv_kernel(r_ref, lw_ref, k_ref, v_ref, a_ref, b_ref, y_ref, st_ref):
    C = RW_CHUNK
    H = RW_HEADS
    Tc = r_ref.shape[1]
    nc = Tc // C
    G = H * nc

    @pl.when(pl.program_id(1) == 0)
    def _():
        st_ref[...] = jnp.zeros_like(st_ref)

    shp = lambda ref: ref[...].reshape(G, C, RW_HEAD_DIM)
    r, lw, k, v, a, b = (shp(x) for x in (r_ref, lw_ref, k_ref, v_ref, a_ref, b_ref))

    ti = lax.broadcasted_iota(jnp.int32, (C, C), 0)
    tj = lax.broadcasted_iota(jnp.int32, (C, C), 1)
    tri = jnp.broadcast_to(jnp.where(tj <= ti, 1.0, 0.0).astype(BF16)[None], (G, C, C))
    eye = jnp.where(ti == tj, 1.0, 0.0).astype(F32)
    wi = lax.broadcasted_iota(jnp.int32, (C, 2 * C), 0)
    wj = lax.broadcasted_iota(jnp.int32, (C, 2 * C), 1)
    second = wj >= C
    wjc = jnp.where(second, wj - C, wj)

    l1 = lw.astype(BF16)
    rem = lw - l1.astype(F32)
    l2 = rem.astype(BF16)
    l3 = (rem - l2.astype(F32)).astype(BF16)
    e = lambda x, y: jnp.einsum('gij,gjk->gik', x, y, preferred_element_type=F32)
    L = e(tri, l1) + e(tri, l2) + e(tri, l3)
    Lprev = L - lw
    Ltot = L[:, C - 1:C, :]

    eL = jnp.exp(L)
    enL = jnp.exp(-L)
    eh = jnp.exp(Ltot - L)
    At = a * jnp.exp(Lprev)
    Rt = r * eL
    BKt = jnp.concatenate([b * enL, k * enL], axis=1)
    BKh = jnp.concatenate([b * eh, k * eh], axis=1)

    P4 = _bmm(jnp.concatenate([At, Rt], axis=1), BKt, RW_PASSES_PROD, nt=True)
    top, bot = P4[:, :C, :], P4[:, C:, :]
    N = jnp.where((tj < ti)[None], top[:, :, :C], 0.0)
    AKz = jnp.where((second & (wjc < wi))[None], top, 0.0)
    RBK = jnp.where((wjc <= wi)[None], bot, 0.0)
    RB = RBK[:, :, :C]

    def siblings(s):
        return ((ti // (2 * s)) == (tj // (2 * s))) & ((ti // s) != (tj // s))

    X = eye[None] + jnp.where(siblings(1)[None], N, 0.0)
    s_blk = 2
    while s_blk < C:
        XE = _bmm(X, jnp.where(siblings(s_blk)[None], N, 0.0), RW_PASSES_INV)
        X = X + _bmm(XE, X, RW_PASSES_INV)
        s_blk *= 2

    zv = jnp.concatenate([jnp.zeros_like(v), v], axis=1)
    Abar = _bmm(X, At, RW_PASSES_APPLY)
    W0 = _bmm(X, _bmm(AKz, zv, RW_PASSES_APPLY), RW_PASSES_APPLY)
    wv = jnp.concatenate([W0, v], axis=1)
    Y0 = _bmm(RBK, wv, RW_PASSES_APPLY)
    Rbar = Rt + _bmm(RB, Abar, RW_PASSES_APPLY)
    BKhT = jnp.swapaxes(BKh, 1, 2)
    Mtx = _bmm(BKhT[:, :, :C], Abar, RW_PASSES_APPLY) + eye[None] * jnp.exp(Ltot)
    G0 = _bmm(BKhT, wv, RW_PASSES_APPLY)

    hsplit = lambda x: x.reshape(H, nc, x.shape[1], x.shape[2])
    RM = hsplit(jnp.concatenate([Rbar, Mtx], axis=1))
    Y0, G0 = hsplit(Y0), hsplit(G0)
    St = st_ref[...]
    ys = []
    for c in range(nc):
        both = _bmm(RM[:, c], St, RW_PASSES_STATE)
        ys.append(both[:, :C] + Y0[:, c])
        St = both[:, C:] + G0[:, c]
    st_ref[...] = St
    y = jnp.concatenate(ys, axis=1)
    for h in range(H):
        y_ref[:, h * RW_HEAD_DIM:(h + 1) * RW_HEAD_DIM] = y[h]


def _rwkv_recurrence(r, lw, k, v, a, b, tc=256):
    B, H, T, Dh = r.shape
    spec = pl.BlockSpec((None, H, tc, Dh), lambda bi, c: (bi, 0, c, 0))
    return pl.pallas_call(
        _rwkv_kernel,
        grid=(B, T // tc),
        in_specs=[spec] * 6,
        out_specs=pl.BlockSpec((None, tc, H * Dh), lambda bi, c: (bi, c, 0)),
        out_shape=jax.ShapeDtypeStruct((B, T, H * Dh), F32),
        scratch_shapes=[pltpu.VMEM((H, Dh, Dh), F32)],
        compiler_params=_cparams("parallel", "arbitrary"),
        name="rwkv_recurrence",
    )(r, lw, k, v, a, b)


def _rope_tables(T, heads, head_dim, rot_dim, base):
    half = rot_dim // 2
    inv_freq = base ** (-jnp.arange(half, dtype=F32) / half)
    ang = jnp.arange(T).astype(F32)[:, None] * inv_freq[None, :]
    cos, sin = jnp.cos(ang), jnp.sin(ang)
    rest = head_dim - rot_dim
    c = jnp.concatenate([cos, cos, jnp.ones((T, rest), F32)], axis=1)
    s_lo = jnp.concatenate([-sin, jnp.zeros((T, half + rest), F32)], axis=1)
    s_hi = jnp.concatenate([jnp.zeros((T, half), F32), sin, jnp.zeros((T, rest), F32)], axis=1)
    tile = lambda x: jnp.tile(x, (1, heads))
    return tile(c), tile(s_lo), tile(s_hi)


def _apply_rope(x, c, s_lo, s_hi, half):
    n = x.shape[-1]
    return x * c + pltpu.roll(x, n - half, 1) * s_lo + pltpu.roll(x, half, 1) * s_hi


def _dilproj_kernel(h_ref, g_ref, w_ref, c_ref, slo_ref, shi_ref, gq_ref, gk_ref, o_ref, nrm_ref, tmp_ref,
                    *, dilation):
    j = pl.program_id(1)

    @pl.when(j == 0)
    def _():
        nrm_ref[...] = _rms(h_ref[...], g_ref[...]).astype(BF16)

    val = jnp.dot(nrm_ref[...], w_ref[...], preferred_element_type=F32)

    @pl.when(j < 2)
    def _():
        gain = jnp.where(j == 0, gq_ref[...], gk_ref[...])
        ms = _dot(val * val, _block_ones(DIL_WIDTH, HEAD_DIM)) * (1.0 / HEAD_DIM)
        xn = val * lax.rsqrt(ms + EPS) * gain
        _store_lane_groups(tmp_ref, _apply_rope(xn, c_ref[...], slo_ref[...], shi_ref[...], ROPE_DIM // 2))

    @pl.when(j == 2)
    def _():
        _store_lane_groups(tmp_ref, val)

    rows = tmp_ref.shape[1] // dilation
    for c in range(dilation):
        for gl in range(tmp_ref.shape[0]):
            piece = tmp_ref[gl, pl.ds(c, rows, stride=dilation), :]
            o_ref[c, :, gl * LANES:(gl + 1) * LANES] = piece.astype(BF16)


def _dilated_projection(h, g_mix, w_qkv, B, T, group, tabs, gq, gk, tm=512):
    M, D = h.shape
    d = DIL_PATTERNS[group][1]
    tps = T // tm
    W = DIL_WIDTH
    tspec = pl.BlockSpec((tm, W), lambda i, j: (i % tps, 0))
    gspec = pl.BlockSpec((1, W), lambda i, j: (0, 0))
    return pl.pallas_call(
        functools.partial(_dilproj_kernel, dilation=d),
        grid=(M // tm, 3),
        in_specs=[
            pl.BlockSpec((tm, D), lambda i, j: (i, 0)),
            pl.BlockSpec((1, D), lambda i, j: (0, 0)),
            pl.BlockSpec((D, W), lambda i, j: (0, j)),
            tspec, tspec, tspec, gspec, gspec,
        ],
        out_specs=pl.BlockSpec((None, None, d, tm // d, W), lambda i, j: (j, i // tps, 0, i % tps, 0)),
        out_shape=jax.ShapeDtypeStruct((3, B, d, T // d, W), BF16),
        scratch_shapes=[pltpu.VMEM((tm, D), BF16), pltpu.VMEM((W // LANES, tm, LANES), F32)],
        compiler_params=_cparams("parallel", "arbitrary"),
        name=f"dilated_projection_{group}",
    )(h, g_mix.reshape(1, D), w_qkv, *tabs,
      jnp.tile(gq, DIL_HEADS).reshape(1, -1), jnp.tile(gk, DIL_HEADS).reshape(1, -1))


def _dilattn_kernel(q_ref, k_ref, v_ref, kprev_ref, vprev_ref, o_ref, lse_ref, *, blocks):
    i = pl.program_id(2)
    Lb = DIL_BLOCK
    PW = 2 * HEAD_DIM
    NP = DIL_WIDTH // PW
    lane = lax.broadcasted_iota(jnp.int32, (Lb, PW), 1)
    low = lane < HEAD_DIM

    qs, ks, vs = [], [], []
    for blk in range(blocks):
        rows = slice(blk * Lb, (blk + 1) * Lb)
        for p in range(NP):
            ln = slice(p * PW, (p + 1) * PW)
            q2 = q_ref[rows, ln]
            zero = jnp.zeros_like(q2)
            qs.append(jnp.concatenate([jnp.where(low, q2, zero), jnp.where(low, zero, q2)], axis=0))
            if blk == 0:
                ks.append(jnp.concatenate([kprev_ref[:, ln], k_ref[rows, ln]], axis=0))
                vs.append(jnp.concatenate([vprev_ref[:, ln], v_ref[rows, ln]], axis=0))
            else:
                both = slice((blk - 1) * Lb, (blk + 1) * Lb)
                ks.append(k_ref[both, ln])
                vs.append(v_ref[both, ln])
    Q, K, V = jnp.stack(qs), jnp.stack(ks), jnp.stack(vs)

    s = jnp.einsum('gqd,gkd->gqk', Q, K, preferred_element_type=F32) * (HEAD_DIM ** -0.5)
    qi = lax.broadcasted_iota(jnp.int32, (2 * Lb, 2 * Lb), 0) % Lb
    kj = lax.broadcasted_iota(jnp.int32, (2 * Lb, 2 * Lb), 1)
    window = (kj >= qi) & (kj <= qi + Lb)
    window0 = window & ((kj >= Lb) | (i > 0))
    s = jnp.concatenate([jnp.where(window0[None], s[:NP], -jnp.inf),
                         jnp.where(window[None], s[NP:], -jnp.inf)], axis=0) if blocks > 1 else \
        jnp.where(window0[None], s, -jnp.inf)
    m = jnp.max(s, axis=-1, keepdims=True)
    e = jnp.exp(s - m).astype(BF16)
    vx = jnp.concatenate([V, jnp.ones_like(V)], axis=-1)
    nd = jnp.einsum('gqk,gkd->gqd', e, vx, preferred_element_type=F32)
    num, den = nd[:, :, :PW], nd[:, :, PW:]
    o2 = num / den
    l2 = m + jnp.log(den)
    for blk in range(blocks):
        rows = slice(blk * Lb, (blk + 1) * Lb)
        for p in range(NP):
            g = blk * NP + p
            ln = slice(p * PW, (p + 1) * PW)
            o_ref[rows, ln] = jnp.where(low, o2[g, :Lb], o2[g, Lb:])
            lse_ref[rows, ln] = jnp.where(low, l2[g, :Lb], l2[g, Lb:])


def _dilated_attention(qkv, blocks):
    _, B, d, Mr, W = qkv.shape
    Lb = DIL_BLOCK
    nb = Mr // Lb
    blocks = min(blocks, nb)
    tile = blocks * Lb

    def cur(which):
        return pl.BlockSpec((None, None, None, tile, W), lambda b, c, i: (which, b, c, i, 0))

    def prev(which):
        return pl.BlockSpec((None, None, None, Lb, W),
                            lambda b, c, i: (which, b, c, jnp.maximum(i * blocks - 1, 0), 0))

    ospec = pl.BlockSpec((None, None, tile, W), lambda b, c, i: (b, c, i, 0))
    oshape = jax.ShapeDtypeStruct((B, d, Mr, W), F32)
    return pl.pallas_call(
        functools.partial(_dilattn_kernel, blocks=blocks),
        grid=(B, d, nb // blocks),
        in_specs=[cur(0), cur(1), cur(2), prev(1), prev(2)],
        out_specs=[ospec, ospec],
        out_shape=[oshape, oshape],
        compiler_params=_cparams("parallel", "parallel", "parallel"),
        name=f"dilated_attention_d{d}",
    )(qkv, qkv, qkv, qkv, qkv)


def _ret_kernel(q_ref, k_ref, v_ref, g_ref, c_ref, slo_ref, shi_ref, gain_ref, o_ref, st_ref):
    C = RET_CHUNK

    @pl.when(pl.program_id(1) == 0)
    def _():
        st_ref[...] = jnp.zeros_like(st_ref)

    c, s_lo, s_hi = c_ref[...], slo_ref[...], shi_ref[...]
    q = _apply_rope(q_ref[...], c, s_lo, s_hi, RET_QK_DIM // 2)
    k = _apply_rope(k_ref[...], c, s_lo, s_hi, RET_QK_DIM // 2) * (RET_QK_DIM ** -0.5)
    v = v_ref[...]
    g = g_ref[...]
    gain = gain_ref[...]

    ji = lax.broadcasted_iota(jnp.int32, (C, C), 0)
    jj = lax.broadcasted_iota(jnp.int32, (C, C), 1)
    diff = (ji - jj).astype(F32)
    jcol = lax.broadcasted_iota(jnp.int32, (C, 1), 0).astype(F32)

    for h in range(RET_HEADS):
        lg = math.log(1.0 - 2.0 ** (-5.0 - h))
        qs = slice(h * RET_QK_DIM, (h + 1) * RET_QK_DIM)
        vs = slice(h * RET_V_DIM, (h + 1) * RET_V_DIM)
        qh, kh, vh = q[:, qs], k[:, qs], v[:, vs]
        decay_in = jnp.where(diff >= 0, jnp.exp(lg * jnp.maximum(diff, 0.0)), 0.0)
        s = lax.dot_general(qh.astype(BF16), kh.astype(BF16), (((1,), (1,)), ((), ())),
                            preferred_element_type=F32) * decay_in
        inner = _dot(s, vh)
        S = st_ref[h]
        cross = _dot(qh * jnp.exp(lg * (jcol + 1.0)), S)
        y = inner + cross
        kd = kh * jnp.exp(lg * (C - 1.0 - jcol))
        kv = lax.dot_general(kd.astype(BF16), vh.astype(BF16), (((0,), (0,)), ((), ())),
                             preferred_element_type=F32)
        st_ref[h] = math.exp(lg * C) * S + kv
        yn = y * lax.rsqrt(jnp.mean(y * y, axis=-1, keepdims=True) + EPS) * gain[:, vs]
        gh = g[:, vs]
        o_ref[:, vs] = gh * jax.nn.sigmoid(gh) * yn


def _retention(p_ret, B, T, tabs, gain):
    C = RET_CHUNK
    nc = T // C
    qk = RET_HEADS * RET_QK_DIM
    row = lambda b, c: b * nc + c
    tspec = pl.BlockSpec((C, qk), lambda b, c: (c, 0))
    return pl.pallas_call(
        _ret_kernel,
        grid=(B, nc),
        in_specs=[
            pl.BlockSpec((C, qk), lambda b, c: (row(b, c), 0)),
            pl.BlockSpec((C, qk), lambda b, c: (row(b, c), 1)),
            pl.BlockSpec((C, RET_WIDTH), lambda b, c: (row(b, c), 1)),
            pl.BlockSpec((C, RET_WIDTH), lambda b, c: (row(b, c), 2)),
            tspec, tspec, tspec,
            pl.BlockSpec((1, RET_WIDTH), lambda b, c: (0, 0)),
        ],
        out_specs=pl.BlockSpec((C, RET_WIDTH), lambda b, c: (row(b, c), 0)),
        out_shape=jax.ShapeDtypeStruct((B * T, RET_WIDTH), F32),
        scratch_shapes=[pltpu.VMEM((RET_HEADS, RET_QK_DIM, RET_V_DIM), F32)],
        compiler_params=_cparams("parallel", "arbitrary"),
        name="retention",
    )(p_ret, p_ret, p_ret, p_ret, *tabs, gain.reshape(1, -1))


def _merge_kernel(h_ref, gm_ref, wg_ref, y_ref, bonus_ref, g_ref, lnw_ref, lnb_ref,
                  o0_ref, o1_ref, o2_ref, l0_ref, l1_ref, l2_ref, yc_ref,
                  wa_ref, wb_ref, wc_ref, wo_ref, out_ref, *tok_scratch):
    h = h_ref[...]
    u = _rms(h, gm_ref[...]).astype(BF16)

    ones = _block_ones(RW_WIDTH, RW_HEAD_DIM)
    y = y_ref[...]
    mean = _dot(y, ones) * (1.0 / RW_HEAD_DIM)
    yc = y - mean
    var = _dot(yc * yc, ones) * (1.0 / RW_HEAD_DIM)
    ya = (yc * lax.rsqrt(var + RW_GN_EPS) * lnw_ref[...] + lnb_ref[...] + bonus_ref[...]) * g_ref[...]

    def token_order(ref, scr):
        dil, rows = ref.shape[0], ref.shape[1]
        if dil == 1:
            return ref[0]
        for c in range(dil):
            for gl in range(scr.shape[0]):
                scr[gl, pl.ds(c, rows, stride=dil), :] = ref[c, :, gl * LANES:(gl + 1) * LANES]
        return jnp.concatenate([scr[gl] for gl in range(scr.shape[0])], axis=-1)

    o0, l0 = o0_ref[0], l0_ref[0]
    o1, l1 = token_order(o1_ref, tok_scratch[0]), token_order(l1_ref, tok_scratch[1])
    o2, l2 = token_order(o2_ref, tok_scratch[2]), token_order(l2_ref, tok_scratch[3])
    mx = jnp.maximum(jnp.maximum(l0, l1), l2)
    e0, e1, e2 = jnp.exp(l0 - mx), jnp.exp(l1 - mx), jnp.exp(l2 - mx)
    yb = (e0 * o0 + e1 * o1 + e2 * o2) / (e0 + e1 + e2)

    D = D_MODEL
    gate = lambda i: jax.nn.sigmoid(jnp.dot(u, wg_ref[:, i * D:(i + 1) * D], preferred_element_type=F32))
    merged = (gate(0) * _dot(ya, wa_ref[...])
              + gate(1) * _dot(yb, wb_ref[...])
              + gate(2) * _dot(yc_ref[...], wc_ref[...]))
    out_ref[...] = h + _dot(merged, wo_ref[...])


def _merge(h, T, g_mix, w_gate, y, bonus, g, ln_w, ln_b, o, lse, y_c, wa, wb, wc, wo, tm=256):
    M, D = h.shape
    tps = T // tm
    tok = lambda w: pl.BlockSpec((tm, w), lambda i: (i, 0))
    full = lambda shape: pl.BlockSpec(shape, lambda i: (0, 0))
    W = RW_WIDTH

    def res(group):
        d = DIL_PATTERNS[group][1]
        return pl.BlockSpec((None, d, tm // d, W), lambda i: (i // tps, 0, i % tps, 0))

    return pl.pallas_call(
        _merge_kernel,
        grid=(M // tm,),
        in_specs=[tok(D), full((1, D)), full((D, 3 * D)),
                  tok(W), tok(W), tok(W), full((1, W)), full((1, W)),
                  res(0), res(1), res(2), res(0), res(1), res(2), tok(W),
                  full((W, D)), full((W, D)), full((W, D)), full((D, D))],
        out_specs=tok(D),
        out_shape=jax.ShapeDtypeStruct((M, D), F32),
        scratch_shapes=[pltpu.VMEM((W // LANES, tm, LANES), F32)] * 4,
        compiler_params=_cparams("parallel"),
        name="gated_merge",
    )(h, g_mix.reshape(1, D), w_gate, y, bonus, g, ln_w.reshape(1, W), ln_b.reshape(1, W),
      o[0], o[1], o[2], lse[0], lse[1], lse[2], y_c, wa, wb, wc, wo)


def _xattn_kernel(h_ref, gx_ref, wq_ref, kv_ref, qn_ref, kn_ref, wo_ref, out_ref, k_scr, v_scr):
    D = D_MODEL

    @pl.when(pl.program_id(1) == 0)
    def _():
        kv = kv_ref[...]
        for hd in range(XA_HEADS):
            sl = slice(hd * XA_HEAD_DIM, (hd + 1) * XA_HEAD_DIM)
            k_scr[:, sl] = _rms(kv[:, sl], kn_ref[...]).astype(BF16)
        v_scr[...] = kv[:, D:].astype(BF16)

    h = h_ref[...]
    hn = _rms(h, gx_ref[...]).astype(BF16)
    q = jnp.dot(hn, wq_ref[...], preferred_element_type=F32)
    outs = []
    for hd in range(XA_HEADS):
        sl = slice(hd * XA_HEAD_DIM, (hd + 1) * XA_HEAD_DIM)
        qh = _rms(q[:, sl], qn_ref[...]).astype(BF16)
        s = lax.dot_general(qh, k_scr[:, sl], (((1,), (1,)), ((), ())),
                            preferred_element_type=F32) * (XA_HEAD_DIM ** -0.5)
        m = jnp.max(s, axis=-1, keepdims=True)
        e = jnp.exp(s - m)
        pr = e / jnp.sum(e, axis=-1, keepdims=True)
        outs.append(jnp.dot(pr.astype(BF16), v_scr[:, sl], preferred_element_type=F32))
    o = jnp.concatenate(outs, axis=-1).astype(BF16)
    out_ref[...] = h + jnp.dot(o, wo_ref[...], preferred_element_type=F32)


def _cross_attention(h, B, T, g_x, wq, kv, q_norm, k_norm, wo, tm=512):
    M, D = h.shape
    tps = T // tm
    full = lambda shape: pl.BlockSpec(shape, lambda b, t: (0, 0))
    return pl.pallas_call(
        _xattn_kernel,
        grid=(B, tps),
        in_specs=[
            pl.BlockSpec((tm, D), lambda b, t: (b * tps + t, 0)),
            full((1, D)), full((D, D)),
            pl.BlockSpec((MEM_LEN, 2 * D), lambda b, t: (b, 0)),
            full((1, XA_HEAD_DIM)), full((1, XA_HEAD_DIM)), full((D, D)),
        ],
        out_specs=pl.BlockSpec((tm, D), lambda b, t: (b * tps + t, 0)),
        out_shape=jax.ShapeDtypeStruct((M, D), F32),
        scratch_shapes=[pltpu.VMEM((MEM_LEN, D), BF16), pltpu.VMEM((MEM_LEN, D), BF16)],
        compiler_params=_cparams("parallel", "arbitrary"),
        name="cross_attention",
    )(h, g_x.reshape(1, D), wq, kv, q_norm.reshape(1, -1), k_norm.reshape(1, -1), wo)


def _pad_rows(w, rows_before, total):
    return jnp.pad(w, ((rows_before, total - rows_before - w.shape[0]), (0, 0)))


def _layer(h, mem2, B, T, p, dil_tabs, ret_tabs):
    bf = lambda w: w.astype(BF16)
    h = _ffn(h, p['norm_ffn1'], bf(p['ffn1_w13']), bf(p['ffn1_w2']))

    w_in = p['w_in']
    rw_w = jnp.pad(w_in[:, :1824], ((0, 0), (0, RW_PAD_COLS - 1824)))
    mu = jnp.pad(p['rw_mu'], (0, RW_PAD_COLS - 1824))
    p_rw = _norm_matmul(h, p['norm_mix'], bf(rw_w), 1024, 640)
    p_ret = _norm_matmul(h, p['norm_mix'], bf(w_in[:, 6432:7968]), 1024, 768)

    w2p = _pad_rows(p['rw_w2'], 0, 128)
    a2p = _pad_rows(p['rw_a2'], 64, 128)
    g2p = _pad_rows(p['rw_g2'], 0, 256)
    r, lw, k, v, a, b, g, bonus = _rwkv_prep(p_rw, B, T, mu, p['rw_w0'], w2p, p['rw_a0'], a2p, g2p,
                                             p['rw_k_k'], p['rw_k_a'], p['rw_r_k'])
    y = _rwkv_recurrence(r, lw, k, v, a, b).reshape(B * T, RW_WIDTH)

    o, lse = [], []
    for grp in range(N_DIL):
        w_qkv = bf(w_in[:, 1824 + grp * 3 * DIL_WIDTH:1824 + (grp + 1) * 3 * DIL_WIDTH])
        qkv = _dilated_projection(h, p['norm_mix'], w_qkv, B, T, grp, dil_tabs,
                                  p['dil_q_norm'][grp], p['dil_k_norm'][grp])
        og, lg = _dilated_attention(qkv, DIL_BLOCKS_PER_STEP)
        o.append(og)
        lse.append(lg)

    y_c = _retention(p_ret, B, T, ret_tabs, p['ret_norm'])

    h = _merge(h, T, p['norm_mix'], bf(w_in[:, 7968:]), y, bonus, g, p['rw_ln_w'], p['rw_ln_b'],
               o, lse, y_c, bf(p['w_branch_rwkv']), bf(p['w_branch_dil']), bf(p['w_branch_ret']),
               bf(p['w_out']))

    kv = _norm_matmul(mem2, p['norm_mem'], bf(p['xa_wkv']), MEM_LEN, 1024)
    h = _cross_attention(h, B, T, p['norm_xattn'], bf(p['xa_wq']), kv, p['xa_q_norm'], p['xa_k_norm'],
                         bf(p['xa_wo']))
    h = _ffn(h, p['norm_ffn2'], bf(p['ffn2_w13']), bf(p['ffn2_w2']))
    return h


_PARAM_NAMES = ('norm_ffn1', 'ffn1_w13', 'ffn1_w2', 'norm_mix', 'w_in', 'rw_mu', 'rw_w0', 'rw_w2', 'rw_a0',
                'rw_a2', 'rw_g2', 'rw_k_k', 'rw_k_a', 'rw_r_k', 'rw_ln_w', 'rw_ln_b', 'dil_q_norm',
                'dil_k_norm', 'ret_norm', 'w_branch_rwkv', 'w_branch_dil', 'w_branch_ret', 'w_out',
                'norm_xattn', 'norm_mem', 'xa_wq', 'xa_wkv', 'xa_q_norm', 'xa_k_norm', 'xa_wo',
                'norm_ffn2', 'ffn2_w13', 'ffn2_w2')


def kernel(x, mem, norm_ffn1, ffn1_w13, ffn1_w2, norm_mix, w_in, rw_mu, rw_w0, rw_w2, rw_a0, rw_a2, rw_g2, rw_k_k, rw_k_a, rw_r_k, rw_ln_w, rw_ln_b, dil_q_norm, dil_k_norm, ret_norm, w_branch_rwkv, w_branch_dil, w_branch_ret, w_out, norm_xattn, norm_mem, xa_wq, xa_wkv, xa_q_norm, xa_k_norm, xa_wo, norm_ffn2, ffn2_w13, ffn2_w2):
    params = dict(zip(_PARAM_NAMES, (norm_ffn1, ffn1_w13, ffn1_w2, norm_mix, w_in, rw_mu, rw_w0, rw_w2, rw_a0,
                                     rw_a2, rw_g2, rw_k_k, rw_k_a, rw_r_k, rw_ln_w, rw_ln_b, dil_q_norm,
                                     dil_k_norm, ret_norm, w_branch_rwkv, w_branch_dil, w_branch_ret, w_out,
                                     norm_xattn, norm_mem, xa_wq, xa_wkv, xa_q_norm, xa_k_norm, xa_wo,
                                     norm_ffn2, ffn2_w13, ffn2_w2)))
    B, T, D = x.shape
    assert D == D_MODEL and T % 2048 == 0 and mem.shape[1] == MEM_LEN
    depth = norm_ffn1.shape[0]
    dil_tabs = _rope_tables(T, DIL_HEADS, HEAD_DIM, ROPE_DIM, ROPE_THETA)
    ret_tabs = _rope_tables(T, RET_HEADS, RET_QK_DIM, RET_QK_DIM, RET_ROPE_BASE)
    h = x.reshape(B * T, D)
    mem2 = mem.reshape(B * MEM_LEN, D)
    for l in range(depth):
        h = _layer(h, mem2, B, T, {n: params[n][l] for n in _PARAM_NAMES}, dil_tabs, ret_tabs)
    return h.reshape(B, T, D)
```

```python
import functools
import math

import jax
import jax.numpy as jnp
import numpy as np
from jax import lax
from jax.experimental import pallas as pl
from jax.experimental.pallas import tpu as pltpu

F32 = jnp.float32
BF16 = jnp.bfloat16

D_MODEL = 1024
D_FF = 2816
EPS = 1e-6

RW_HEADS = 8
RW_HEAD_DIM = 64
RW_WIDTH = 512
RW_GN_EPS = 64e-5
RW_CHUNK = 64
RW_PASSES_PROD = 1
RW_PASSES_INV = 1
RW_PASSES_APPLY = 1
RW_PASSES_STATE = 1
RW_PAD_COLS = 1920

DIL_PATTERNS = ((128, 1), (512, 4), (2048, 16))
N_DIL = 3
DIL_HEADS = 8
HEAD_DIM = 64
DIL_WIDTH = 512
DIL_BLOCK = 128
DIL_BLOCKS_PER_STEP = 4
ROPE_THETA = 500000.0
ROPE_DIM = 16

RET_HEADS = 4
RET_QK_DIM = 64
RET_V_DIM = 128
RET_CHUNK = 128
RET_ROPE_BASE = 10000.0
RET_WIDTH = 512

XA_HEADS = 4
XA_HEAD_DIM = 256
MEM_LEN = 256

VMEM_LIMIT = 56 * 1024 * 1024
LANES = 128


def _cparams(*sem):
    return pltpu.CompilerParams(dimension_semantics=sem, vmem_limit_bytes=VMEM_LIMIT)


def _dot(a, b):
    return jnp.dot(a.astype(BF16), b.astype(BF16), preferred_element_type=F32)


def _split2(x):
    hi = x.astype(BF16)
    lo = (x - hi.astype(F32)).astype(BF16)
    return hi, lo


def _dot3(a, b):
    ah, al = _split2(a)
    bh, bl = _split2(b)
    return (jnp.dot(ah, bh, preferred_element_type=F32)
            + jnp.dot(ah, bl, preferred_element_type=F32)
            + jnp.dot(al, bh, preferred_element_type=F32))


def _rms(x, g):
    return x * lax.rsqrt(jnp.mean(x * x, axis=-1, keepdims=True) + EPS) * g


def _store_lane_groups(ref, x):
    for gl in range(ref.shape[0]):
        ref[gl] = x[:, gl * LANES:(gl + 1) * LANES]


def _block_ones(n, width):
    i = lax.broadcasted_iota(jnp.int32, (n, n), 0) // width
    j = lax.broadcasted_iota(jnp.int32, (n, n), 1) // width
    return jnp.where(i == j, 1.0, 0.0).astype(BF16)


def _ffn_kernel(h_ref, g_ref, w1_ref, w3_ref, w2_ref, o_ref, nrm_ref, acc_ref):
    j = pl.program_id(1)

    @pl.when(j == 0)
    def _():
        nrm_ref[...] = _rms(h_ref[...], g_ref[...]).astype(BF16)
        acc_ref[...] = jnp.zeros_like(acc_ref)

    n = nrm_ref[...]
    a = jnp.dot(n, w1_ref[...], preferred_element_type=F32)
    b = jnp.dot(n, w3_ref[...], preferred_element_type=F32)
    mid = (a * jax.nn.sigmoid(a) * b).astype(BF16)
    acc_ref[...] += jnp.dot(mid, w2_ref[...], preferred_element_type=F32)

    @pl.when(j == pl.num_programs(1) - 1)
    def _():
        o_ref[...] = h_ref[...] + 0.5 * acc_ref[...]


def _ffn(h, g, w13, w2, tm=1024, tf=256):
    M, D = h.shape
    nff = D_FF // tf
    return pl.pallas_call(
        _ffn_kernel,
        grid=(M // tm, nff),
        in_specs=[
            pl.BlockSpec((tm, D), lambda i, j: (i, 0)),
            pl.BlockSpec((1, D), lambda i, j: (0, 0)),
            pl.BlockSpec((D, tf), lambda i, j: (0, j)),
            pl.BlockSpec((D, tf), lambda i, j: (0, j + nff)),
            pl.BlockSpec((tf, D), lambda i, j: (j, 0)),
        ],
        out_specs=pl.BlockSpec((tm, D), lambda i, j: (i, 0)),
        out_shape=jax.ShapeDtypeStruct((M, D), F32),
        scratch_shapes=[pltpu.VMEM((tm, D), BF16), pltpu.VMEM((tm, D), F32)],
        compiler_params=_cparams("parallel", "arbitrary"),
        name="ffn",
    )(h, g.reshape(1, D), w13, w13, w2)


def _nmm_kernel(h_ref, g_ref, w_ref, o_ref, nrm_ref):
    @pl.when(pl.program_id(1) == 0)
    def _():
        nrm_ref[...] = _rms(h_ref[...], g_ref[...]).astype(BF16)

    o_ref[...] = jnp.dot(nrm_ref[...], w_ref[...], preferred_element_type=F32)


def _norm_matmul(h, g, w, tm, tn):
    M, D = h.shape
    N = w.shape[1]
    return pl.pallas_call(
        _nmm_kernel,
        grid=(M // tm, N // tn),
        in_specs=[
            pl.BlockSpec((tm, D), lambda i, j: (i, 0)),
            pl.BlockSpec((1, D), lambda i, j: (0, 0)),
            pl.BlockSpec((D, tn), lambda i, j: (0, j)),
        ],
        out_specs=pl.BlockSpec((tm, tn), lambda i, j: (i, j)),
        out_shape=jax.ShapeDtypeStruct((M, N), F32),
        scratch_shapes=[pltpu.VMEM((tm, D), BF16)],
        compiler_params=_cparams("parallel", "arbitrary"),
        name="norm_matmul",
    )(h, g.reshape(1, D), w)


def _rwprep_kernel(h_ref, gm_ref, w_ref, mu_ref, w0_ref, w2_ref, a0_ref, a2_ref, g2_ref,
                   kk_ref, ka_ref, rk_ref, ones_ref,
                   r_out, lw_out, k_out, v_out, a_out, b_out, g_out, bonus_out, last_ref, *, tiles_per_seq):
    i = pl.program_id(0)
    p = jnp.dot(_rms(h_ref[...], gm_ref[...]).astype(BF16), w_ref[...], preferred_element_type=F32)
    tm = p.shape[0]

    @pl.when(i % tiles_per_seq == 0)
    def _():
        last_ref[...] = jnp.zeros_like(last_ref)

    prev_row = last_ref[0:1, :]
    last_ref[0:1, :] = p[tm - 1:tm, :]
    rows = lax.broadcasted_iota(jnp.int32, p.shape, 0)
    shifted = jnp.where(rows == 0, prev_row, pltpu.roll(p, 1, 0))
    pm = p + (shifted - p) * mu_ref[...]

    r = pm[:, 0:512]
    k = pm[:, 512:1024]
    v = pm[:, 1024:1536]
    wa = pm[:, 1536:1664]
    gd = pm[:, 1664:1920]

    w_in = w0_ref[...] + _dot3(jnp.tanh(wa), w2_ref[...])
    w_log = -jax.nn.softplus(-w_in) - 0.5
    lw = -jnp.exp(w_log)
    a_sig = jax.nn.sigmoid(a0_ref[...] + _dot3(wa, a2_ref[...]))
    g = _dot(jax.nn.sigmoid(gd), g2_ref[...])

    ones = ones_ref[...]
    kk = k * kk_ref[...]
    ss = _dot(kk * kk, ones)
    kk = kk / jnp.maximum(jnp.sqrt(ss), 1e-12)
    k2 = k * (1.0 + (a_sig - 1.0) * ka_ref[...])
    bonus = _dot(r * k2 * rk_ref[...], ones) * v

    g_out[...] = g
    bonus_out[...] = bonus
    a_vec = -kk
    b_vec = kk * a_sig
    for h in range(RW_HEADS):
        sl = slice(h * RW_HEAD_DIM, (h + 1) * RW_HEAD_DIM)
        r_out[h] = r[:, sl]
        lw_out[h] = lw[:, sl]
        k_out[h] = k2[:, sl]
        v_out[h] = v[:, sl]
        a_out[h] = a_vec[:, sl]
        b_out[h] = b_vec[:, sl]


def _rwkv_prep(h, g_mix, w_rw, B, T, mu, w0, w2p, a0, a2p, g2p, k_k, k_a, r_k, ones, tm=512):
    M, D = h.shape
    tps = T // tm
    row = lambda x: x.reshape(1, -1)
    full = lambda shape: pl.BlockSpec(shape, lambda i: (0,) * len(shape))
    head_spec = pl.BlockSpec((None, RW_HEADS, tm, RW_HEAD_DIM), lambda i: (i // tps, 0, i % tps, 0))
    head_shape = jax.ShapeDtypeStruct((B, RW_HEADS, T, RW_HEAD_DIM), F32)
    tok_spec = pl.BlockSpec((tm, RW_WIDTH), lambda i: (i, 0))
    tok_shape = jax.ShapeDtypeStruct((M, RW_WIDTH), F32)
    return pl.pallas_call(
        functools.partial(_rwprep_kernel, tiles_per_seq=tps),
        grid=(M // tm,),
        in_specs=[
            pl.BlockSpec((tm, D), lambda i: (i, 0)), full((1, D)), full((D, RW_PAD_COLS)),
            full((1, RW_PAD_COLS)), full((1, RW_WIDTH)), full((128, RW_WIDTH)),
            full((1, RW_WIDTH)), full((128, RW_WIDTH)), full((256, RW_WIDTH)),
            full((1, RW_WIDTH)), full((1, RW_WIDTH)), full((1, RW_WIDTH)), full((RW_WIDTH, RW_WIDTH)),
        ],
        out_specs=[head_spec] * 6 + [tok_spec] * 2,
        out_shape=[head_shape] * 6 + [tok_shape] * 2,
        scratch_shapes=[pltpu.VMEM((8, RW_PAD_COLS), F32)],
        compiler_params=_cparams("arbitrary"),
        name="rwkv_prep",
    )(h, row(g_mix), w_rw, row(mu), row(w0), w2p, row(a0), a2p, g2p, row(k_k), row(k_a), row(r_k), ones)


def _bmm(a, b, passes, nt=False):
    spec = 'gik,gjk->gij' if nt else 'gij,gjk->gik'
    e = lambda x, y: jnp.einsum(spec, x, y, preferred_element_type=F32)
    if passes == 1:
        return e(a.astype(BF16), b.astype(BF16))
    ah, al = _split2(a)
    bh, bl = _split2(b)
    return e(ah, bh) + e(ah, bl) + e(al, bh)


def _rwkv_kernel(r_ref, lw_ref, k_ref, v_ref, a_ref, b_ref, y_ref, st_ref):
    C = RW_CHUNK
    H = RW_HEADS
    Tc = r_ref.shape[1]
    nc = Tc // C
    G = H * nc

    @pl.when(pl.program_id(1) == 0)
    def _():
        st_ref[...] = jnp.zeros_like(st_ref)

    shp = lambda ref: ref[...].reshape(G, C, RW_HEAD_DIM)
    r, lw, k, v, a, b = (shp(x) for x in (r_ref, lw_ref, k_ref, v_ref, a_ref, b_ref))

    ti = lax.broadcasted_iota(jnp.int32, (C, C), 0)
    tj = lax.broadcasted_iota(jnp.int32, (C, C), 1)
    tri = jnp.broadcast_to(jnp.where(tj <= ti, 1.0, 0.0).astype(BF16)[None], (G, C, C))
    eye = jnp.where(ti == tj, 1.0, 0.0).astype(F32)
    wi = lax.broadcasted_iota(jnp.int32, (C, 2 * C), 0)
    wj = lax.broadcasted_iota(jnp.int32, (C, 2 * C), 1)
    second = wj >= C
    wjc = jnp.where(second, wj - C, wj)

    l1 = lw.astype(BF16)
    rem = lw - l1.astype(F32)
    l2 = rem.astype(BF16)
    l3 = (rem - l2.astype(F32)).astype(BF16)
    e = lambda x, y: jnp.einsum('gij,gjk->gik', x, y, preferred_element_type=F32)
    L = e(tri, l1) + e(tri, l2) + e(tri, l3)
    Lprev = L - lw
    Ltot = L[:, C - 1:C, :]

    eL = jnp.exp(L)
    enL = jnp.exp(-L)
    eh = jnp.exp(Ltot - L)
    At = a * jnp.exp(Lprev)
    Rt = r * eL
    BKt = jnp.concatenate([b * enL, k * enL], axis=1)
    BKh = jnp.concatenate([b * eh, k * eh], axis=1)

    P4 = _bmm(jnp.concatenate([At, Rt], axis=1), BKt, RW_PASSES_PROD, nt=True)
    top, bot = P4[:, :C, :], P4[:, C:, :]
    N = jnp.where((tj < ti)[None], top[:, :, :C], 0.0)
    AKz = jnp.where((second & (wjc < wi))[None], top, 0.0)
    RBK = jnp.where((wjc <= wi)[None], bot, 0.0)
    RB = RBK[:, :, :C]

    def siblings(s):
        return ((ti // (2 * s)) == (tj // (2 * s))) & ((ti // s) != (tj // s))

    X = eye[None] + jnp.where(siblings(1)[None], N, 0.0)
    s_blk = 2
    while s_blk < C:
        XE = _bmm(X, jnp.where(siblings(s_blk)[None], N, 0.0), RW_PASSES_INV)
        X = X + _bmm(XE, X, RW_PASSES_INV)
        s_blk *= 2

    zv = jnp.concatenate([jnp.zeros_like(v), v], axis=1)
    Abar = _bmm(X, At, RW_PASSES_APPLY)
    W0 = _bmm(X, _bmm(AKz, zv, RW_PASSES_APPLY), RW_PASSES_APPLY)
    wv = jnp.concatenate([W0, v], axis=1)
    Y0 = _bmm(RBK, wv, RW_PASSES_APPLY)
    Rbar = Rt + _bmm(RB, Abar, RW_PASSES_APPLY)
    BKhT = jnp.swapaxes(BKh, 1, 2)
    Mtx = _bmm(BKhT[:, :, :C], Abar, RW_PASSES_APPLY) + eye[None] * jnp.exp(Ltot)
    G0 = _bmm(BKhT, wv, RW_PASSES_APPLY)

    hsplit = lambda x: x.reshape(H, nc, x.shape[1], x.shape[2])
    RM = hsplit(jnp.concatenate([Rbar, Mtx], axis=1))
    Y0, G0 = hsplit(Y0), hsplit(G0)
    St = st_ref[...]
    ys = []
    for c in range(nc):
        both = _bmm(RM[:, c], St, RW_PASSES_STATE)
        ys.append(both[:, :C] + Y0[:, c])
        St = both[:, C:] + G0[:, c]
    st_ref[...] = St
    y = jnp.concatenate(ys, axis=1)
    for h in range(H):
        y_ref[:, h * RW_HEAD_DIM:(h + 1) * RW_HEAD_DIM] = y[h]


def _rwkv_recurrence(r, lw, k, v, a, b, tc=256):
    B, H, T, Dh = r.shape
    spec = pl.BlockSpec((None, H, tc, Dh), lambda bi, c: (bi, 0, c, 0))
    return pl.pallas_call(
        _rwkv_kernel,
        grid=(B, T // tc),
        in_specs=[spec] * 6,
        out_specs=pl.BlockSpec((None, tc, H * Dh), lambda bi, c: (bi, c, 0)),
        out_shape=jax.ShapeDtypeStruct((B, T, H * Dh), F32),
        scratch_shapes=[pltpu.VMEM((H, Dh, Dh), F32)],
        compiler_params=_cparams("parallel", "arbitrary"),
        name="rwkv_recurrence",
    )(r, lw, k, v, a, b)


def _rope_tables(T, heads, head_dim, rot_dim, base):
    half = rot_dim // 2
    inv_freq = base ** (-jnp.arange(half, dtype=F32) / half)
    ang = jnp.arange(T).astype(F32)[:, None] * inv_freq[None, :]
    cos, sin = jnp.cos(ang), jnp.sin(ang)
    rest = head_dim - rot_dim
    c = jnp.concatenate([cos, cos, jnp.ones((T, rest), F32)], axis=1)
    s_lo = jnp.concatenate([-sin, jnp.zeros((T, half + rest), F32)], axis=1)
    s_hi = jnp.concatenate([jnp.zeros((T, half), F32), sin, jnp.zeros((T, rest), F32)], axis=1)
    tile = lambda x: jnp.tile(x, (1, heads))
    return tile(c), tile(s_lo), tile(s_hi)


def _apply_rope(x, c, s_lo, s_hi, half):
    n = x.shape[-1]
    return x * c + pltpu.roll(x, n - half, 1) * s_lo + pltpu.roll(x, half, 1) * s_hi


def _dilproj_kernel(h_ref, g_ref, w_ref, c_ref, slo_ref, shi_ref, gq_ref, gk_ref, ones_ref,
                    o0_ref, o1_ref, o2_ref, tmp_ref):
    u = _rms(h_ref[...], g_ref[...]).astype(BF16)
    c, s_lo, s_hi = c_ref[...], slo_ref[...], shi_ref[...]
    W = DIL_WIDTH
    for grp, o_ref in enumerate((o0_ref, o1_ref, o2_ref)):
        dilation = DIL_PATTERNS[grp][1]
        rows = u.shape[0] // dilation
        for which in range(3):
            col = (3 * grp + which) * W
            val = jnp.dot(u, w_ref[:, col:col + W], preferred_element_type=F32)
            if which < 2:
                gain = (gq_ref if which == 0 else gk_ref)[grp:grp + 1, :]
                ms = jnp.dot((val * val).astype(BF16), ones_ref[...], preferred_element_type=F32) * (1.0 / HEAD_DIM)
                val = _apply_rope(val * lax.rsqrt(ms + EPS) * gain, c, s_lo, s_hi, ROPE_DIM // 2)
            if dilation == 1:
                o_ref[which, 0] = val.astype(BF16)
            else:
                _store_lane_groups(tmp_ref, val)
                for res in range(dilation):
                    for gl in range(tmp_ref.shape[0]):
                        piece = tmp_ref[gl, pl.ds(res, rows, stride=dilation), :]
                        o_ref[which, res, :, gl * LANES:(gl + 1) * LANES] = piece.astype(BF16)


def _dilated_projection(h, g_mix, w_dil, B, T, tabs, gq, gk, ones, tm=512):
    M, D = h.shape
    tps = T // tm
    W = DIL_WIDTH
    const = lambda shape: pl.BlockSpec(shape, lambda i: (0,) * len(shape))
    tspec = pl.BlockSpec((tm, W), lambda i: (i % tps, 0))
    out_specs, out_shape = [], []
    for _, d in DIL_PATTERNS:
        out_specs.append(pl.BlockSpec((3, None, d, tm // d, W), lambda i: (0, i // tps, 0, i % tps, 0)))
        out_shape.append(jax.ShapeDtypeStruct((3, B, d, T // d, W), BF16))
    return pl.pallas_call(
        _dilproj_kernel,
        grid=(M // tm,),
        in_specs=[pl.BlockSpec((tm, D), lambda i: (i, 0)), const((1, D)), const((D, 3 * N_DIL * W)),
                  tspec, tspec, tspec, const((N_DIL, W)), const((N_DIL, W)), const((W, W))],
        out_specs=out_specs,
        out_shape=out_shape,
        scratch_shapes=[pltpu.VMEM((W // LANES, tm, LANES), F32)],
        compiler_params=_cparams("parallel"),
        name="dilated_projection",
    )(h, g_mix.reshape(1, D), w_dil, *tabs, jnp.tile(gq, (1, DIL_HEADS)), jnp.tile(gk, (1, DIL_HEADS)), ones)


def _dilattn_kernel(q_ref, k_ref, v_ref, kprev_ref, vprev_ref, o_ref, lse_ref, *, blocks):
    i = pl.program_id(2)
    Lb = DIL_BLOCK
    PW = 2 * HEAD_DIM
    NP = DIL_WIDTH // PW
    lane = lax.broadcasted_iota(jnp.int32, (Lb, PW), 1)
    low = lane < HEAD_DIM

    qs, ks, vs = [], [], []
    for blk in range(blocks):
        rows = slice(blk * Lb, (blk + 1) * Lb)
        for p in range(NP):
            ln = slice(p * PW, (p + 1) * PW)
            q2 = q_ref[rows, ln]
            zero = jnp.zeros_like(q2)
            qs.append(jnp.concatenate([jnp.where(low, q2, zero), jnp.where(low, zero, q2)], axis=0))
            if blk == 0:
                ks.append(jnp.concatenate([kprev_ref[:, ln], k_ref[rows, ln]], axis=0))
                vs.append(jnp.concatenate([vprev_ref[:, ln], v_ref[rows, ln]], axis=0))
            else:
                both = slice((blk - 1) * Lb, (blk + 1) * Lb)
                ks.append(k_ref[both, ln])
                vs.append(v_ref[both, ln])
    Q, K, V = jnp.stack(qs), jnp.stack(ks), jnp.stack(vs)

    s = jnp.einsum('gqd,gkd->gqk', Q, K, preferred_element_type=F32) * (HEAD_DIM ** -0.5)
    qi = lax.broadcasted_iota(jnp.int32, (2 * Lb, 2 * Lb), 0) % Lb
    kj = lax.broadcasted_iota(jnp.int32, (2 * Lb, 2 * Lb), 1)
    window = (kj >= qi) & (kj <= qi + Lb)
    window0 = window & ((kj >= Lb) | (i > 0))
    s = jnp.concatenate([jnp.where(window0[None], s[:NP], -jnp.inf),
                         jnp.where(window[None], s[NP:], -jnp.inf)], axis=0) if blocks > 1 else \
        jnp.where(window0[None], s, -jnp.inf)
    m = jnp.max(s, axis=-1, keepdims=True)
    e = jnp.exp(s - m).astype(BF16)
    vx = jnp.concatenate([V, jnp.ones_like(V)], axis=-1)
    nd = jnp.einsum('gqk,gkd->gqd', e, vx, preferred_element_type=F32)
    num, den = nd[:, :, :PW], nd[:, :, PW:]
    o2 = num / den
    l2 = m + jnp.log(den)
    for blk in range(blocks):
        rows = slice(blk * Lb, (blk + 1) * Lb)
        for p in range(NP):
            g = blk * NP + p
            ln = slice(p * PW, (p + 1) * PW)
            o_ref[rows, ln] = jnp.where(low, o2[g, :Lb], o2[g, Lb:])
            lse_ref[rows, ln] = jnp.where(low, l2[g, :Lb], l2[g, Lb:])


def _dilated_attention(qkv, blocks):
    _, B, d, Mr, W = qkv.shape
    Lb = DIL_BLOCK
    nb = Mr // Lb
    blocks = min(blocks, nb)
    tile = blocks * Lb

    def cur(which):
        return pl.BlockSpec((None, None, None, tile, W), lambda b, c, i: (which, b, c, i, 0))

    def prev(which):
        return pl.BlockSpec((None, None, None, Lb, W),
                            lambda b, c, i: (which, b, c, jnp.maximum(i * blocks - 1, 0), 0))

    ospec = pl.BlockSpec((None, None, tile, W), lambda b, c, i: (b, c, i, 0))
    oshape = jax.ShapeDtypeStruct((B, d, Mr, W), F32)
    return pl.pallas_call(
        functools.partial(_dilattn_kernel, blocks=blocks),
        grid=(B, d, nb // blocks),
        in_specs=[cur(0), cur(1), cur(2), prev(1), prev(2)],
        out_specs=[ospec, ospec],
        out_shape=[oshape, oshape],
        compiler_params=_cparams("parallel", "parallel", "parallel"),
        name=f"dilated_attention_d{d}",
    )(qkv, qkv, qkv, qkv, qkv)


def _ret_kernel(q_ref, k_ref, v_ref, g_ref, c_ref, slo_ref, shi_ref, gain_ref, o_ref, st_ref):
    C = RET_CHUNK

    @pl.when(pl.program_id(1) == 0)
    def _():
        st_ref[...] = jnp.zeros_like(st_ref)

    c, s_lo, s_hi = c_ref[...], slo_ref[...], shi_ref[...]
    q = _apply_rope(q_ref[...], c, s_lo, s_hi, RET_QK_DIM // 2)
    k = _apply_rope(k_ref[...], c, s_lo, s_hi, RET_QK_DIM // 2) * (RET_QK_DIM ** -0.5)
    v = v_ref[...]
    g = g_ref[...]
    gain = gain_ref[...]

    ji = lax.broadcasted_iota(jnp.int32, (C, C), 0)
    jj = lax.broadcasted_iota(jnp.int32, (C, C), 1)
    diff = (ji - jj).astype(F32)
    jcol = lax.broadcasted_iota(jnp.int32, (C, 1), 0).astype(F32)

    for h in range(RET_HEADS):
        lg = math.log(1.0 - 2.0 ** (-5.0 - h))
        qs = slice(h * RET_QK_DIM, (h + 1) * RET_QK_DIM)
        vs = slice(h * RET_V_DIM, (h + 1) * RET_V_DIM)
        qh, kh, vh = q[:, qs], k[:, qs], v[:, vs]
        decay_in = jnp.where(diff >= 0, jnp.exp(lg * jnp.maximum(diff, 0.0)), 0.0)
        s = lax.dot_general(qh.astype(BF16), kh.astype(BF16), (((1,), (1,)), ((), ())),
                            preferred_element_type=F32) * decay_in
        inner = _dot(s, vh)
        S = st_ref[h]
        cross = _dot(qh * jnp.exp(lg * (jcol + 1.0)), S)
        y = inner + cross
        kd = kh * jnp.exp(lg * (C - 1.0 - jcol))
        kv = lax.dot_general(kd.astype(BF16), vh.astype(BF16), (((0,), (0,)), ((), ())),
                             preferred_element_type=F32)
        st_ref[h] = math.exp(lg * C) * S + kv
        yn = y * lax.rsqrt(jnp.mean(y * y, axis=-1, keepdims=True) + EPS) * gain[:, vs]
        gh = g[:, vs]
        o_ref[:, vs] = gh * jax.nn.sigmoid(gh) * yn


def _retention(p_ret, B, T, tabs, gain):
    C = RET_CHUNK
    nc = T // C
    qk = RET_HEADS * RET_QK_DIM
    row = lambda b, c: b * nc + c
    tspec = pl.BlockSpec((C, qk), lambda b, c: (c, 0))
    return pl.pallas_call(
        _ret_kernel,
        grid=(B, nc),
        in_specs=[
            pl.BlockSpec((C, qk), lambda b, c: (row(b, c), 0)),
            pl.BlockSpec((C, qk), lambda b, c: (row(b, c), 1)),
            pl.BlockSpec((C, RET_WIDTH), lambda b, c: (row(b, c), 1)),
            pl.BlockSpec((C, RET_WIDTH), lambda b, c: (row(b, c), 2)),
            tspec, tspec, tspec,
            pl.BlockSpec((1, RET_WIDTH), lambda b, c: (0, 0)),
        ],
        out_specs=pl.BlockSpec((C, RET_WIDTH), lambda b, c: (row(b, c), 0)),
        out_shape=jax.ShapeDtypeStruct((B * T, RET_WIDTH), F32),
        scratch_shapes=[pltpu.VMEM((RET_HEADS, RET_QK_DIM, RET_V_DIM), F32)],
        compiler_params=_cparams("parallel", "arbitrary"),
        name="retention",
    )(p_ret, p_ret, p_ret, p_ret, *tabs, gain.reshape(1, -1))


def _merge_kernel(h_ref, gm_ref, wg_ref, y_ref, bonus_ref, g_ref, lnw_ref, lnb_ref,
                  o0_ref, o1_ref, o2_ref, l0_ref, l1_ref, l2_ref, yc_ref,
                  wa_ref, wb_ref, wc_ref, wo_ref, ones_ref, out_ref, *tok_scratch):
    h = h_ref[...]
    u = _rms(h, gm_ref[...]).astype(BF16)

    ones = ones_ref[...]
    y = y_ref[...]
    mean = _dot(y, ones) * (1.0 / RW_HEAD_DIM)
    yc = y - mean
    var = _dot(yc * yc, ones) * (1.0 / RW_HEAD_DIM)
    ya = (yc * lax.rsqrt(var + RW_GN_EPS) * lnw_ref[...] + lnb_ref[...] + bonus_ref[...]) * g_ref[...]

    def token_order(ref, scr):
        dil, rows = ref.shape[0], ref.shape[1]
        if dil == 1:
            return ref[0]
        for c in range(dil):
            for gl in range(scr.shape[0]):
                scr[gl, pl.ds(c, rows, stride=dil), :] = ref[c, :, gl * LANES:(gl + 1) * LANES]
        return jnp.concatenate([scr[gl] for gl in range(scr.shape[0])], axis=-1)

    o0, l0 = o0_ref[0], l0_ref[0]
    o1, l1 = token_order(o1_ref, tok_scratch[0]), token_order(l1_ref, tok_scratch[1])
    o2, l2 = token_order(o2_ref, tok_scratch[2]), token_order(l2_ref, tok_scratch[3])
    mx = jnp.maximum(jnp.maximum(l0, l1), l2)
    e0, e1, e2 = jnp.exp(l0 - mx), jnp.exp(l1 - mx), jnp.exp(l2 - mx)
    yb = (e0 * o0 + e1 * o1 + e2 * o2) / (e0 + e1 + e2)

    D = D_MODEL
    gate = lambda i: jax.nn.sigmoid(jnp.dot(u, wg_ref[:, i * D:(i + 1) * D], preferred_element_type=F32))
    merged = (gate(0) * _dot(ya, wa_ref[...])
              + gate(1) * _dot(yb, wb_ref[...])
              + gate(2) * _dot(yc_ref[...], wc_ref[...]))
    out_ref[...] = h + _dot(merged, wo_ref[...])


def _merge(h, T, g_mix, w_gate, y, bonus, g, ln_w, ln_b, o, lse, y_c, wa, wb, wc, wo, ones, tm=256):
    M, D = h.shape
    tps = T // tm
    tok = lambda w: pl.BlockSpec((tm, w), lambda i: (i, 0))
    full = lambda shape: pl.BlockSpec(shape, lambda i: (0, 0))
    W = RW_WIDTH

    def res(group):
        d = DIL_PATTERNS[group][1]
        return pl.BlockSpec((None, d, tm // d, W), lambda i: (i // tps, 0, i % tps, 0))

    return pl.pallas_call(
        _merge_kernel,
        grid=(M // tm,),
        in_specs=[tok(D), full((1, D)), full((D, 3 * D)),
                  tok(W), tok(W), tok(W), full((1, W)), full((1, W)),
                  res(0), res(1), res(2), res(0), res(1), res(2), tok(W),
                  full((W, D)), full((W, D)), full((W, D)), full((D, D)), full((W, W))],
        out_specs=tok(D),
        out_shape=jax.ShapeDtypeStruct((M, D), F32),
        scratch_shapes=[pltpu.VMEM((W // LANES, tm, LANES), F32)] * 4,
        compiler_params=_cparams("parallel"),
        name="gated_merge",
    )(h, g_mix.reshape(1, D), w_gate, y, bonus, g, ln_w.reshape(1, W), ln_b.reshape(1, W),
      o[0], o[1], o[2], lse[0], lse[1], lse[2], y_c, wa, wb, wc, wo, ones)


def _xattn_kernel(h_ref, gx_ref, wq_ref, kv_ref, qn_ref, kn_ref, wo_ref, out_ref, k_scr, v_scr):
    D = D_MODEL

    @pl.when(pl.program_id(1) == 0)
    def _():
        kv = kv_ref[...]
        for hd in range(XA_HEADS):
            sl = slice(hd * XA_HEAD_DIM, (hd + 1) * XA_HEAD_DIM)
            k_scr[:, sl] = _rms(kv[:, sl], kn_ref[...]).astype(BF16)
        v_scr[...] = kv[:, D:].astype(BF16)

    h = h_ref[...]
    hn = _rms(h, gx_ref[...]).astype(BF16)
    q = jnp.dot(hn, wq_ref[...], preferred_element_type=F32)
    outs = []
    for hd in range(XA_HEADS):
        sl = slice(hd * XA_HEAD_DIM, (hd + 1) * XA_HEAD_DIM)
        qh = _rms(q[:, sl], qn_ref[...]).astype(BF16)
        s = lax.dot_general(qh, k_scr[:, sl], (((1,), (1,)), ((), ())),
                            preferred_element_type=F32) * (XA_HEAD_DIM ** -0.5)
        m = jnp.max(s, axis=-1, keepdims=True)
        e = jnp.exp(s - m)
        pr = e / jnp.sum(e, axis=-1, keepdims=True)
        outs.append(jnp.dot(pr.astype(BF16), v_scr[:, sl], preferred_element_type=F32))
    o = jnp.concatenate(outs, axis=-1).astype(BF16)
    out_ref[...] = h + jnp.dot(o, wo_ref[...], preferred_element_type=F32)


def _cross_attention(h, B, T, g_x, wq, kv, q_norm, k_norm, wo, tm=512):
    M, D = h.shape
    tps = T // tm
    full = lambda shape: pl.BlockSpec(shape, lambda b, t: (0, 0))
    return pl.pallas_call(
        _xattn_kernel,
        grid=(B, tps),
        in_specs=[
            pl.BlockSpec((tm, D), lambda b, t: (b * tps + t, 0)),
            full((1, D)), full((D, D)),
            pl.BlockSpec((MEM_LEN, 2 * D), lambda b, t: (b, 0)),
            full((1, XA_HEAD_DIM)), full((1, XA_HEAD_DIM)), full((D, D)),
        ],
        out_specs=pl.BlockSpec((tm, D), lambda b, t: (b * tps + t, 0)),
        out_shape=jax.ShapeDtypeStruct((M, D), F32),
        scratch_shapes=[pltpu.VMEM((MEM_LEN, D), BF16), pltpu.VMEM((MEM_LEN, D), BF16)],
        compiler_params=_cparams("parallel", "arbitrary"),
        name="cross_attention",
    )(h, g_x.reshape(1, D), wq, kv, q_norm.reshape(1, -1), k_norm.reshape(1, -1), wo)


def _pad_rows(w, rows_before, total):
    return jnp.pad(w, ((rows_before, total - rows_before - w.shape[0]), (0, 0)))


def _layer(h, mem2, B, T, p, dil_tabs, ret_tabs):
    bf = lambda w: w.astype(BF16)
    ones = _block_ones(RW_WIDTH, RW_HEAD_DIM)
    h = _ffn(h, p['norm_ffn1'], bf(p['ffn1_w13']), bf(p['ffn1_w2']))

    w_in = p['w_in']
    rw_w = jnp.pad(w_in[:, :1824], ((0, 0), (0, RW_PAD_COLS - 1824)))
    mu = jnp.pad(p['rw_mu'], (0, RW_PAD_COLS - 1824))
    p_ret = _norm_matmul(h, p['norm_mix'], bf(w_in[:, 6432:7968]), 1024, 768)

    w2p = _pad_rows(p['rw_w2'], 0, 128)
    a2p = _pad_rows(p['rw_a2'], 64, 128)
    g2p = _pad_rows(p['rw_g2'], 0, 256)
    r, lw, k, v, a, b, g, bonus = _rwkv_prep(h, p['norm_mix'], bf(rw_w), B, T, mu, p['rw_w0'], w2p, p['rw_a0'], a2p, g2p,
                                             p['rw_k_k'], p['rw_k_a'], p['rw_r_k'], ones)
    y = _rwkv_recurrence(r, lw, k, v, a, b).reshape(B * T, RW_WIDTH)

    qkvs = _dilated_projection(h, p['norm_mix'], bf(w_in[:, 1824:6432]), B, T, dil_tabs,
                               p['dil_q_norm'], p['dil_k_norm'], ones)
    o, lse = zip(*[_dilated_attention(qkv, DIL_BLOCKS_PER_STEP) for qkv in qkvs])

    y_c = _retention(p_ret, B, T, ret_tabs, p['ret_norm'])

    h = _merge(h, T, p['norm_mix'], bf(w_in[:, 7968:]), y, bonus, g, p['rw_ln_w'], p['rw_ln_b'],
               o, lse, y_c, bf(p['w_branch_rwkv']), bf(p['w_branch_dil']), bf(p['w_branch_ret']),
               bf(p['w_out']), ones)

    kv = _norm_matmul(mem2, p['norm_mem'], bf(p['xa_wkv']), MEM_LEN, 1024)
    h = _cross_attention(h, B, T, p['norm_xattn'], bf(p['xa_wq']), kv, p['xa_q_norm'], p['xa_k_norm'],
                         bf(p['xa_wo']))
    h = _ffn(h, p['norm_ffn2'], bf(p['ffn2_w13']), bf(p['ffn2_w2']))
    return h


_PARAM_NAMES = ('norm_ffn1', 'ffn1_w13', 'ffn1_w2', 'norm_mix', 'w_in', 'rw_mu', 'rw_w0', 'rw_w2', 'rw_a0',
                'rw_a2', 'rw_g2', 'rw_k_k', 'rw_k_a', 'rw_r_k', 'rw_ln_w', 'rw_ln_b', 'dil_q_norm',
                'dil_k_norm', 'ret_norm', 'w_branch_rwkv', 'w_branch_dil', 'w_branch_ret', 'w_out',
                'norm_xattn', 'norm_mem', 'xa_wq', 'xa_wkv', 'xa_q_norm', 'xa_k_norm', 'xa_wo',
                'norm_ffn2', 'ffn2_w13', 'ffn2_w2')


def kernel(x, mem, norm_ffn1, ffn1_w13, ffn1_w2, norm_mix, w_in, rw_mu, rw_w0, rw_w2, rw_a0, rw_a2, rw_g2, rw_k_k, rw_k_a, rw_r_k, rw_ln_w, rw_ln_b, dil_q_norm, dil_k_norm, ret_norm, w_branch_rwkv, w_branch_dil, w_branch_ret, w_out, norm_xattn, norm_mem, xa_wq, xa_wkv, xa_q_norm, xa_k_norm, xa_wo, norm_ffn2, ffn2_w13, ffn2_w2):
    params = dict(zip(_PARAM_NAMES, (norm_ffn1, ffn1_w13, ffn1_w2, norm_mix, w_in, rw_mu, rw_w0, rw_w2, rw_a0,
                                     rw_a2, rw_g2, rw_k_k, rw_k_a, rw_r_k, rw_ln_w, rw_ln_b, dil_q_norm,
                                     dil_k_norm, ret_norm, w_branch_rwkv, w_branch_dil, w_branch_ret, w_out,
                                     norm_xattn, norm_mem, xa_wq, xa_wkv, xa_q_norm, xa_k_norm, xa_wo,
                                     norm_ffn2, ffn2_w13, ffn2_w2)))
    B, T, D = x.shape
    assert D == D_MODEL and T % 2048 == 0 and mem.shape[1] == MEM_LEN
    depth = norm_ffn1.shape[0]
    dil_tabs = _rope_tables(T, DIL_HEADS, HEAD_DIM, ROPE_DIM, ROPE_THETA)
    ret_tabs = _rope_tables(T, RET_HEADS, RET_QK_DIM, RET_QK_DIM, RET_ROPE_BASE)
    h = x.reshape(B * T, D)
    mem2 = mem.reshape(B * MEM_LEN, D)
    for l in range(depth):
        h = _layer(h, mem2, B, T, {n: params[n][l] for n in _PARAM_NAMES}, dil_tabs, ret_tabs)
    return h.reshape(B, T, D)
```

```python
import functools
import math

import jax
import jax.numpy as jnp
import numpy as np
from jax import lax
from jax.experimental import pallas as pl
from jax.experimental.pallas import tpu as pltpu

F32 = jnp.float32
BF16 = jnp.bfloat16

D_MODEL = 1024
D_FF = 2816
EPS = 1e-6

RW_HEADS = 8
RW_HEAD_DIM = 64
RW_WIDTH = 512
RW_GN_EPS = 64e-5
RW_CHUNK = 64
RW_PASSES_PROD = 1
RW_PASSES_INV = 1
RW_PASSES_APPLY = 1
RW_PASSES_STATE = 1
RW_PAD_COLS = 1920

DIL_PATTERNS = ((128, 1), (512, 4), (2048, 16))
N_DIL = 3
DIL_HEADS = 8
HEAD_DIM = 64
DIL_WIDTH = 512
DIL_BLOCK = 128
DIL_BLOCKS_PER_STEP = 4
LSE_LANES = 16
ROPE_THETA = 500000.0
ROPE_DIM = 16

RET_HEADS = 4
RET_QK_DIM = 64
RET_V_DIM = 128
RET_CHUNK = 128
RET_ROPE_BASE = 10000.0
RET_WIDTH = 512

XA_HEADS = 4
XA_HEAD_DIM = 256
MEM_LEN = 256

VMEM_LIMIT = 56 * 1024 * 1024
LANES = 128


def _cparams(*sem):
    return pltpu.CompilerParams(dimension_semantics=sem, vmem_limit_bytes=VMEM_LIMIT)


def _dot(a, b):
    return jnp.dot(a.astype(BF16), b.astype(BF16), preferred_element_type=F32)


def _split2(x):
    hi = x.astype(BF16)
    lo = (x - hi.astype(F32)).astype(BF16)
    return hi, lo


def _dot3(a, b):
    ah, al = _split2(a)
    bh, bl = _split2(b)
    return (jnp.dot(ah, bh, preferred_element_type=F32)
            + jnp.dot(ah, bl, preferred_element_type=F32)
            + jnp.dot(al, bh, preferred_element_type=F32))


def _rms(x, g):
    return x * lax.rsqrt(jnp.mean(x * x, axis=-1, keepdims=True) + EPS) * g


def _store_lane_groups(ref, x):
    for gl in range(ref.shape[0]):
        ref[gl] = x[:, gl * LANES:(gl + 1) * LANES]


def _block_ones(n, width):
    i = lax.broadcasted_iota(jnp.int32, (n, n), 0) // width
    j = lax.broadcasted_iota(jnp.int32, (n, n), 1) // width
    return jnp.where(i == j, 1.0, 0.0).astype(BF16)


def _ffn_kernel(h_ref, g_ref, w1_ref, w3_ref, w2_ref, o_ref, nrm_ref, acc_ref):
    j = pl.program_id(1)

    @pl.when(j == 0)
    def _():
        nrm_ref[...] = _rms(h_ref[...], g_ref[...]).astype(BF16)
        acc_ref[...] = jnp.zeros_like(acc_ref)

    n = nrm_ref[...]
    a = jnp.dot(n, w1_ref[...], preferred_element_type=F32)
    b = jnp.dot(n, w3_ref[...], preferred_element_type=F32)
    mid = (a * jax.nn.sigmoid(a) * b).astype(BF16)
    acc_ref[...] += jnp.dot(mid, w2_ref[...], preferred_element_type=F32)

    @pl.when(j == pl.num_programs(1) - 1)
    def _():
        o_ref[...] = h_ref[...] + 0.5 * acc_ref[...]


def _ffn(h, g, w13, w2, tm=1024, tf=256):
    M, D = h.shape
    nff = D_FF // tf
    return pl.pallas_call(
        _ffn_kernel,
        grid=(M // tm, nff),
        in_specs=[
            pl.BlockSpec((tm, D), lambda i, j: (i, 0)),
            pl.BlockSpec((1, D), lambda i, j: (0, 0)),
            pl.BlockSpec((D, tf), lambda i, j: (0, j)),
            pl.BlockSpec((D, tf), lambda i, j: (0, j + nff)),
            pl.BlockSpec((tf, D), lambda i, j: (j, 0)),
        ],
        out_specs=pl.BlockSpec((tm, D), lambda i, j: (i, 0)),
        out_shape=jax.ShapeDtypeStruct((M, D), F32),
        scratch_shapes=[pltpu.VMEM((tm, D), BF16), pltpu.VMEM((tm, D), F32)],
        compiler_params=_cparams("parallel", "arbitrary"),
        name="ffn",
    )(h, g.reshape(1, D), w13, w13, w2)


def _nmm_kernel(h_ref, g_ref, w_ref, o_ref, nrm_ref):
    @pl.when(pl.program_id(1) == 0)
    def _():
        nrm_ref[...] = _rms(h_ref[...], g_ref[...]).astype(BF16)

    o_ref[...] = jnp.dot(nrm_ref[...], w_ref[...], preferred_element_type=F32)


def _norm_matmul(h, g, w, tm, tn):
    M, D = h.shape
    N = w.shape[1]
    return pl.pallas_call(
        _nmm_kernel,
        grid=(M // tm, N // tn),
        in_specs=[
            pl.BlockSpec((tm, D), lambda i, j: (i, 0)),
            pl.BlockSpec((1, D), lambda i, j: (0, 0)),
            pl.BlockSpec((D, tn), lambda i, j: (0, j)),
        ],
        out_specs=pl.BlockSpec((tm, tn), lambda i, j: (i, j)),
        out_shape=jax.ShapeDtypeStruct((M, N), F32),
        scratch_shapes=[pltpu.VMEM((tm, D), BF16)],
        compiler_params=_cparams("parallel", "arbitrary"),
        name="norm_matmul",
    )(h, g.reshape(1, D), w)


def _rwprep_kernel(h_ref, gm_ref, w_ref, mu_ref, w0_ref, w2_ref, a0_ref, a2_ref, g2_ref,
                   kk_ref, ka_ref, rk_ref, ones_ref,
                   r_out, lw_out, k_out, v_out, a_out, b_out, g_out, bonus_out, last_ref, *, tiles_per_seq):
    i = pl.program_id(0)
    p = jnp.dot(_rms(h_ref[...], gm_ref[...]).astype(BF16), w_ref[...], preferred_element_type=F32)
    tm = p.shape[0]

    @pl.when(i % tiles_per_seq == 0)
    def _():
        last_ref[...] = jnp.zeros_like(last_ref)

    prev_row = last_ref[0:1, :]
    last_ref[0:1, :] = p[tm - 1:tm, :]
    rows = lax.broadcasted_iota(jnp.int32, p.shape, 0)
    shifted = jnp.where(rows == 0, prev_row, pltpu.roll(p, 1, 0))
    pm = p + (shifted - p) * mu_ref[...]

    r = pm[:, 0:512]
    k = pm[:, 512:1024]
    v = pm[:, 1024:1536]
    wa = pm[:, 1536:1664]
    gd = pm[:, 1664:1920]

    w_in = w0_ref[...] + _dot3(jnp.tanh(wa), w2_ref[...])
    w_log = -jax.nn.softplus(-w_in) - 0.5
    lw = -jnp.exp(w_log)
    a_sig = jax.nn.sigmoid(a0_ref[...] + _dot3(wa, a2_ref[...]))
    g = _dot(jax.nn.sigmoid(gd), g2_ref[...])

    ones = ones_ref[...]
    kk = k * kk_ref[...]
    ss = _dot(kk * kk, ones)
    kk = kk / jnp.maximum(jnp.sqrt(ss), 1e-12)
    k2 = k * (1.0 + (a_sig - 1.0) * ka_ref[...])
    bonus = _dot(r * k2 * rk_ref[...], ones) * v

    g_out[...] = g
    bonus_out[...] = bonus
    a_vec = -kk
    b_vec = kk * a_sig
    for h in range(RW_HEADS):
        sl = slice(h * RW_HEAD_DIM, (h + 1) * RW_HEAD_DIM)
        r_out[h] = r[:, sl]
        lw_out[h] = lw[:, sl]
        k_out[h] = k2[:, sl]
        v_out[h] = v[:, sl]
        a_out[h] = a_vec[:, sl]
        b_out[h] = b_vec[:, sl]


def _rwkv_prep(h, g_mix, w_rw, B, T, mu, w0, w2p, a0, a2p, g2p, k_k, k_a, r_k, ones, tm=512):
    M, D = h.shape
    tps = T // tm
    row = lambda x: x.reshape(1, -1)
    full = lambda shape: pl.BlockSpec(shape, lambda i: (0,) * len(shape))
    head_spec = pl.BlockSpec((None, RW_HEADS, tm, RW_HEAD_DIM), lambda i: (i // tps, 0, i % tps, 0))
    head_shape = jax.ShapeDtypeStruct((B, RW_HEADS, T, RW_HEAD_DIM), F32)
    tok_spec = pl.BlockSpec((tm, RW_WIDTH), lambda i: (i, 0))
    tok_shape = jax.ShapeDtypeStruct((M, RW_WIDTH), F32)
    return pl.pallas_call(
        functools.partial(_rwprep_kernel, tiles_per_seq=tps),
        grid=(M // tm,),
        in_specs=[
            pl.BlockSpec((tm, D), lambda i: (i, 0)), full((1, D)), full((D, RW_PAD_COLS)),
            full((1, RW_PAD_COLS)), full((1, RW_WIDTH)), full((128, RW_WIDTH)),
            full((1, RW_WIDTH)), full((128, RW_WIDTH)), full((256, RW_WIDTH)),
            full((1, RW_WIDTH)), full((1, RW_WIDTH)), full((1, RW_WIDTH)), full((RW_WIDTH, RW_WIDTH)),
        ],
        out_specs=[head_spec] * 6 + [tok_spec] * 2,
        out_shape=[head_shape] * 6 + [tok_shape] * 2,
        scratch_shapes=[pltpu.VMEM((8, RW_PAD_COLS), F32)],
        compiler_params=_cparams("arbitrary"),
        name="rwkv_prep",
    )(h, row(g_mix), w_rw, row(mu), row(w0), w2p, row(a0), a2p, g2p, row(k_k), row(k_a), row(r_k), ones)


def _bmm(a, b, passes, nt=False):
    spec = 'gik,gjk->gij' if nt else 'gij,gjk->gik'
    e = lambda x, y: jnp.einsum(spec, x, y, preferred_element_type=F32)
    if passes == 1:
        return e(a.astype(BF16), b.astype(BF16))
    ah, al = _split2(a)
    bh, bl = _split2(b)
    return e(ah, bh) + e(ah, bl) + e(al, bh)


def _rwkv_kernel(r_ref, lw_ref, k_ref, v_ref, a_ref, b_ref, bonus_ref, g_ref, lnw_ref, lnb_ref, ones_ref,
                 y_ref, st_ref, ytok_ref):
    C = RW_CHUNK
    H = RW_HEADS
    Tc = r_ref.shape[1]
    nc = Tc // C
    G = H * nc

    @pl.when(pl.program_id(1) == 0)
    def _():
        st_ref[...] = jnp.zeros_like(st_ref)

    shp = lambda ref: ref[...].reshape(G, C, RW_HEAD_DIM)
    r, lw, k, v, a, b = (shp(x) for x in (r_ref, lw_ref, k_ref, v_ref, a_ref, b_ref))

    ti = lax.broadcasted_iota(jnp.int32, (C, C), 0)
    tj = lax.broadcasted_iota(jnp.int32, (C, C), 1)
    tri = jnp.broadcast_to(jnp.where(tj <= ti, 1.0, 0.0).astype(BF16)[None], (G, C, C))
    eye = jnp.where(ti == tj, 1.0, 0.0).astype(F32)
    wi = lax.broadcasted_iota(jnp.int32, (C, 2 * C), 0)
    wj = lax.broadcasted_iota(jnp.int32, (C, 2 * C), 1)
    second = wj >= C
    wjc = jnp.where(second, wj - C, wj)

    l1 = lw.astype(BF16)
    rem = lw - l1.astype(F32)
    l2 = rem.astype(BF16)
    l3 = (rem - l2.astype(F32)).astype(BF16)
    e = lambda x, y: jnp.einsum('gij,gjk->gik', x, y, preferred_element_type=F32)
    L = e(tri, l1) + e(tri, l2) + e(tri, l3)
    Lprev = L - lw
    Ltot = L[:, C - 1:C, :]

    eL = jnp.exp(L)
    enL = jnp.exp(-L)
    eh = jnp.exp(Ltot - L)
    At = a * jnp.exp(Lprev)
    Rt = r * eL
    BKt = jnp.concatenate([b * enL, k * enL], axis=1)
    BKh = jnp.concatenate([b * eh, k * eh], axis=1)

    P4 = _bmm(jnp.concatenate([At, Rt], axis=1), BKt, RW_PASSES_PROD, nt=True)
    top, bot = P4[:, :C, :], P4[:, C:, :]
    N = jnp.where((tj < ti)[None], top[:, :, :C], 0.0)
    AKz = jnp.where((second & (wjc < wi))[None], top, 0.0)
    RBK = jnp.where((wjc <= wi)[None], bot, 0.0)
    RB = RBK[:, :, :C]

    def siblings(s):
        return ((ti // (2 * s)) == (tj // (2 * s))) & ((ti // s) != (tj // s))

    X = eye[None] + jnp.where(siblings(1)[None], N, 0.0)
    s_blk = 2
    while s_blk < C:
        XE = _bmm(X, jnp.where(siblings(s_blk)[None], N, 0.0), RW_PASSES_INV)
        X = X + _bmm(XE, X, RW_PASSES_INV)
        s_blk *= 2

    zv = jnp.concatenate([jnp.zeros_like(v), v], axis=1)
    Abar = _bmm(X, At, RW_PASSES_APPLY)
    W0 = _bmm(X, _bmm(AKz, zv, RW_PASSES_APPLY), RW_PASSES_APPLY)
    wv = jnp.concatenate([W0, v], axis=1)
    Y0 = _bmm(RBK, wv, RW_PASSES_APPLY)
    Rbar = Rt + _bmm(RB, Abar, RW_PASSES_APPLY)
    BKhT = jnp.swapaxes(BKh, 1, 2)
    Mtx = _bmm(BKhT[:, :, :C], Abar, RW_PASSES_APPLY) + eye[None] * jnp.exp(Ltot)
    G0 = _bmm(BKhT, wv, RW_PASSES_APPLY)

    hsplit = lambda x: x.reshape(H, nc, x.shape[1], x.shape[2])
    RM = hsplit(jnp.concatenate([Rbar, Mtx], axis=1))
    Y0, G0 = hsplit(Y0), hsplit(G0)
    St = st_ref[...]
    ys = []
    for c in range(nc):
        both = _bmm(RM[:, c], St, RW_PASSES_STATE)
        ys.append(both[:, :C] + Y0[:, c])
        St = both[:, C:] + G0[:, c]
    st_ref[...] = St
    y = jnp.concatenate(ys, axis=1)
    for h in range(H):
        ytok_ref[:, h * RW_HEAD_DIM:(h + 1) * RW_HEAD_DIM] = y[h]

    y = ytok_ref[...]
    ones = ones_ref[...]
    mean = _dot(y, ones) * (1.0 / RW_HEAD_DIM)
    yc = y - mean
    var = _dot(yc * yc, ones) * (1.0 / RW_HEAD_DIM)
    ya = (yc * lax.rsqrt(var + RW_GN_EPS) * lnw_ref[...] + lnb_ref[...] + bonus_ref[...]) * g_ref[...]
    y_ref[...] = ya.astype(BF16)


def _rwkv_recurrence(r, lw, k, v, a, b, bonus, g, ln_w, ln_b, ones, tc=256):
    B, H, T, Dh = r.shape
    W = H * Dh
    nt = T // tc
    spec = pl.BlockSpec((None, H, tc, Dh), lambda bi, c: (bi, 0, c, 0))
    tok = pl.BlockSpec((tc, W), lambda bi, c: (bi * nt + c, 0))
    const = lambda shape: pl.BlockSpec(shape, lambda bi, c: (0, 0))
    return pl.pallas_call(
        _rwkv_kernel,
        grid=(B, nt),
        in_specs=[spec] * 6 + [tok, tok, const((1, W)), const((1, W)), const((W, W))],
        out_specs=tok,
        out_shape=jax.ShapeDtypeStruct((B * T, W), BF16),
        scratch_shapes=[pltpu.VMEM((H, Dh, Dh), F32), pltpu.VMEM((tc, W), F32)],
        compiler_params=_cparams("parallel", "arbitrary"),
        name="rwkv_recurrence",
    )(r, lw, k, v, a, b, bonus, g, ln_w.reshape(1, W), ln_b.reshape(1, W), ones)


def _rope_tables(T, heads, head_dim, rot_dim, base):
    half = rot_dim // 2
    inv_freq = base ** (-jnp.arange(half, dtype=F32) / half)
    ang = jnp.arange(T).astype(F32)[:, None] * inv_freq[None, :]
    cos, sin = jnp.cos(ang), jnp.sin(ang)
    rest = head_dim - rot_dim
    c = jnp.concatenate([cos, cos, jnp.ones((T, rest), F32)], axis=1)
    s_lo = jnp.concatenate([-sin, jnp.zeros((T, half + rest), F32)], axis=1)
    s_hi = jnp.concatenate([jnp.zeros((T, half), F32), sin, jnp.zeros((T, rest), F32)], axis=1)
    tile = lambda x: jnp.tile(x, (1, heads))
    return tile(c), tile(s_lo), tile(s_hi)


def _apply_rope(x, c, s_lo, s_hi, half):
    n = x.shape[-1]
    return x * c + pltpu.roll(x, n - half, 1) * s_lo + pltpu.roll(x, half, 1) * s_hi


def _dilproj_kernel(h_ref, g_ref, w_ref, c_ref, slo_ref, shi_ref, gq_ref, gk_ref, ones_ref,
                    o0_ref, o1_ref, o2_ref, tmp_ref):
    u = _rms(h_ref[...], g_ref[...]).astype(BF16)
    c, s_lo, s_hi = c_ref[...], slo_ref[...], shi_ref[...]
    W = DIL_WIDTH
    for grp, o_ref in enumerate((o0_ref, o1_ref, o2_ref)):
        dilation = DIL_PATTERNS[grp][1]
        rows = u.shape[0] // dilation
        for which in range(3):
            col = (3 * grp + which) * W
            val = jnp.dot(u, w_ref[:, col:col + W], preferred_element_type=F32)
            if which < 2:
                gain = (gq_ref if which == 0 else gk_ref)[grp:grp + 1, :]
                ms = jnp.dot((val * val).astype(BF16), ones_ref[...], preferred_element_type=F32) * (1.0 / HEAD_DIM)
                val = _apply_rope(val * lax.rsqrt(ms + EPS) * gain, c, s_lo, s_hi, ROPE_DIM // 2)
            if dilation == 1:
                o_ref[which, 0] = val.astype(BF16)
            else:
                _store_lane_groups(tmp_ref, val)
                for res in range(dilation):
                    for gl in range(tmp_ref.shape[0]):
                        piece = tmp_ref[gl, pl.ds(res, rows, stride=dilation), :]
                        o_ref[which, res, :, gl * LANES:(gl + 1) * LANES] = piece.astype(BF16)


def _dilated_projection(h, g_mix, w_dil, B, T, tabs, gq, gk, ones, tm=512):
    M, D = h.shape
    tps = T // tm
    W = DIL_WIDTH
    const = lambda shape: pl.BlockSpec(shape, lambda i: (0,) * len(shape))
    tspec = pl.BlockSpec((tm, W), lambda i: (i % tps, 0))
    out_specs, out_shape = [], []
    for _, d in DIL_PATTERNS:
        out_specs.append(pl.BlockSpec((3, None, d, tm // d, W), lambda i: (0, i // tps, 0, i % tps, 0)))
        out_shape.append(jax.ShapeDtypeStruct((3, B, d, T // d, W), BF16))
    return pl.pallas_call(
        _dilproj_kernel,
        grid=(M // tm,),
        in_specs=[pl.BlockSpec((tm, D), lambda i: (i, 0)), const((1, D)), const((D, 3 * N_DIL * W)),
                  tspec, tspec, tspec, const((N_DIL, W)), const((N_DIL, W)), const((W, W))],
        out_specs=out_specs,
        out_shape=out_shape,
        scratch_shapes=[pltpu.VMEM((W // LANES, tm, LANES), F32)],
        compiler_params=_cparams("parallel"),
        name="dilated_projection",
    )(h, g_mix.reshape(1, D), w_dil, *tabs, jnp.tile(gq, (1, DIL_HEADS)), jnp.tile(gk, (1, DIL_HEADS)), ones)


def _dilattn_kernel(q_ref, k_ref, v_ref, kprev_ref, vprev_ref, o_ref, lse_ref, *, blocks):
    i = pl.program_id(2)
    Lb = DIL_BLOCK
    PW = 2 * HEAD_DIM
    NP = DIL_WIDTH // PW
    lane = lax.broadcasted_iota(jnp.int32, (Lb, PW), 1)
    low = lane < HEAD_DIM

    qs, ks, vs = [], [], []
    for blk in range(blocks):
        rows = slice(blk * Lb, (blk + 1) * Lb)
        for p in range(NP):
            ln = slice(p * PW, (p + 1) * PW)
            q2 = q_ref[rows, ln]
            zero = jnp.zeros_like(q2)
            qs.append(jnp.concatenate([jnp.where(low, q2, zero), jnp.where(low, zero, q2)], axis=0))
            if blk == 0:
                ks.append(jnp.concatenate([kprev_ref[:, ln], k_ref[rows, ln]], axis=0))
                vs.append(jnp.concatenate([vprev_ref[:, ln], v_ref[rows, ln]], axis=0))
            else:
                both = slice((blk - 1) * Lb, (blk + 1) * Lb)
                ks.append(k_ref[both, ln])
                vs.append(v_ref[both, ln])
    Q, K, V = jnp.stack(qs), jnp.stack(ks), jnp.stack(vs)

    s = jnp.einsum('gqd,gkd->gqk', Q, K, preferred_element_type=F32) * (HEAD_DIM ** -0.5)
    qi = lax.broadcasted_iota(jnp.int32, (2 * Lb, 2 * Lb), 0) % Lb
    kj = lax.broadcasted_iota(jnp.int32, (2 * Lb, 2 * Lb), 1)
    window = (kj >= qi) & (kj <= qi + Lb)
    window0 = window & ((kj >= Lb) | (i > 0))
    s = jnp.concatenate([jnp.where(window0[None], s[:NP], -jnp.inf),
                         jnp.where(window[None], s[NP:], -jnp.inf)], axis=0) if blocks > 1 else \
        jnp.where(window0[None], s, -jnp.inf)
    m = jnp.max(s, axis=-1, keepdims=True)
    e = jnp.exp(s - m).astype(BF16)
    vx = jnp.concatenate([V, jnp.ones_like(V)], axis=-1)
    nd = jnp.einsum('gqk,gkd->gqd', e, vx, preferred_element_type=F32)
    num, den = nd[:, :, :PW], nd[:, :, PW:]
    o2 = num / den
    l2 = m + jnp.log(den)
    head_of_lane = lane // LSE_LANES
    for blk in range(blocks):
        rows = slice(blk * Lb, (blk + 1) * Lb)
        lse_c = jnp.zeros((Lb, PW), F32)
        for p in range(NP):
            g = blk * NP + p
            ln = slice(p * PW, (p + 1) * PW)
            o_ref[rows, ln] = jnp.where(low, o2[g, :Lb], o2[g, Lb:]).astype(BF16)
            lse_c = jnp.where(head_of_lane == 2 * p, l2[g, :Lb], jnp.where(head_of_lane == 2 * p + 1, l2[g, Lb:], lse_c))
        lse_ref[rows, :] = lse_c


def _dilated_attention(qkv, blocks):
    _, B, d, Mr, W = qkv.shape
    Lb = DIL_BLOCK
    nb = Mr // Lb
    blocks = min(blocks, nb)
    tile = blocks * Lb

    def cur(which):
        return pl.BlockSpec((None, None, None, tile, W), lambda b, c, i: (which, b, c, i, 0))

    def prev(which):
        return pl.BlockSpec((None, None, None, Lb, W),
                            lambda b, c, i: (which, b, c, jnp.maximum(i * blocks - 1, 0), 0))

    ospec = pl.BlockSpec((None, None, tile, W), lambda b, c, i: (b, c, i, 0))
    lspec = pl.BlockSpec((None, None, tile, LANES), lambda b, c, i: (b, c, i, 0))
    return pl.pallas_call(
        functools.partial(_dilattn_kernel, blocks=blocks),
        grid=(B, d, nb // blocks),
        in_specs=[cur(0), cur(1), cur(2), prev(1), prev(2)],
        out_specs=[ospec, lspec],
        out_shape=[jax.ShapeDtypeStruct((B, d, Mr, W), BF16), jax.ShapeDtypeStruct((B, d, Mr, LANES), F32)],
        compiler_params=_cparams("parallel", "parallel", "parallel"),
        name=f"dilated_attention_d{d}",
    )(qkv, qkv, qkv, qkv, qkv)


def _ret_kernel(h_ref, gm_ref, w_ref, c_ref, slo_ref, shi_ref, gain_ref, o_ref, st_ref):
    C = RET_CHUNK
    QK = RET_HEADS * RET_QK_DIM

    @pl.when(pl.program_id(1) == 0)
    def _():
        st_ref[...] = jnp.zeros_like(st_ref)

    u = _rms(h_ref[...], gm_ref[...]).astype(BF16)
    p = jnp.dot(u, w_ref[...], preferred_element_type=F32)
    c, s_lo, s_hi = c_ref[...], slo_ref[...], shi_ref[...]
    q = _apply_rope(p[:, :QK], c, s_lo, s_hi, RET_QK_DIM // 2)
    k = _apply_rope(p[:, QK:2 * QK], c, s_lo, s_hi, RET_QK_DIM // 2) * (RET_QK_DIM ** -0.5)
    v = p[:, 2 * QK:2 * QK + RET_WIDTH]
    g = p[:, 2 * QK + RET_WIDTH:]
    gain = gain_ref[...]

    ji = lax.broadcasted_iota(jnp.int32, (C, C), 0)
    jj = lax.broadcasted_iota(jnp.int32, (C, C), 1)
    diff = (ji - jj).astype(F32)
    jcol = lax.broadcasted_iota(jnp.int32, (C, 1), 0).astype(F32)

    for h in range(RET_HEADS):
        lg = math.log(1.0 - 2.0 ** (-5.0 - h))
        qs = slice(h * RET_QK_DIM, (h + 1) * RET_QK_DIM)
        vs = slice(h * RET_V_DIM, (h + 1) * RET_V_DIM)
        decay_in = jnp.where(diff >= 0, jnp.exp(lg * jnp.maximum(diff, 0.0)), 0.0)
        q_dec = jnp.exp(lg * (jcol + 1.0))
        k_dec = jnp.exp(lg * (C - 1.0 - jcol))
        S = st_ref[h]
        for ch in range(p.shape[0] // C):
            rows = slice(ch * C, (ch + 1) * C)
            qh, kh, vh = q[rows, qs], k[rows, qs], v[rows, vs]
            s = lax.dot_general(qh.astype(BF16), kh.astype(BF16), (((1,), (1,)), ((), ())),
                                preferred_element_type=F32) * decay_in
            y = _dot(s, vh) + _dot(qh * q_dec, S)
            kv = lax.dot_general((kh * k_dec).astype(BF16), vh.astype(BF16), (((0,), (0,)), ((), ())),
                                 preferred_element_type=F32)
            S = math.exp(lg * C) * S + kv
            yn = y * lax.rsqrt(jnp.mean(y * y, axis=-1, keepdims=True) + EPS) * gain[:, vs]
            gh = g[rows, vs]
            o_ref[rows, vs] = (gh * jax.nn.sigmoid(gh) * yn).astype(BF16)
        st_ref[h] = S


def _retention(h, g_mix, w_ret, B, T, tabs, gain, tr=512):
    M, D = h.shape
    nt = T // tr
    qk = RET_HEADS * RET_QK_DIM
    const = lambda shape: pl.BlockSpec(shape, lambda b, t: (0,) * len(shape))
    tspec = pl.BlockSpec((tr, qk), lambda b, t: (t, 0))
    return pl.pallas_call(
        _ret_kernel,
        grid=(B, nt),
        in_specs=[pl.BlockSpec((tr, D), lambda b, t: (b * nt + t, 0)), const((1, D)), const(w_ret.shape),
                  tspec, tspec, tspec, const((1, RET_WIDTH))],
        out_specs=pl.BlockSpec((tr, RET_WIDTH), lambda b, t: (b * nt + t, 0)),
        out_shape=jax.ShapeDtypeStruct((M, RET_WIDTH), BF16),
        scratch_shapes=[pltpu.VMEM((RET_HEADS, RET_QK_DIM, RET_V_DIM), F32)],
        compiler_params=_cparams("parallel", "arbitrary"),
        name="retention",
    )(h, g_mix.reshape(1, D), w_ret, *tabs, gain.reshape(1, -1))


def _merge_kernel(h_ref, gm_ref, wg_ref, ya_ref, o0_ref, o1_ref, o2_ref, l0_ref, l1_ref, l2_ref, yc_ref,
                  wa_ref, wb_ref, wc_ref, wo_ref, expand_ref, out_ref, o1_scr, o2_scr, l1_scr, l2_scr):
    h = h_ref[...]
    u = _rms(h, gm_ref[...]).astype(BF16)

    def token_order(ref, scr):
        dil, rows = ref.shape[0], ref.shape[1]
        for c in range(dil):
            blk = ref[c].astype(F32)
            for gl in range(scr.shape[0]):
                scr[gl, pl.ds(c, rows, stride=dil), :] = blk[:, gl * LANES:(gl + 1) * LANES]
        return jnp.concatenate([scr[gl] for gl in range(scr.shape[0])], axis=-1)

    o0, l0 = o0_ref[0].astype(F32), l0_ref[0]
    o1, l1 = token_order(o1_ref, o1_scr), token_order(l1_ref, l1_scr)
    o2, l2 = token_order(o2_ref, o2_scr), token_order(l2_ref, l2_scr)

    mx = jnp.maximum(jnp.maximum(l0, l1), l2)
    e0, e1, e2 = jnp.exp(l0 - mx), jnp.exp(l1 - mx), jnp.exp(l2 - mx)
    inv = 1.0 / (e0 + e1 + e2)

    def spread(w):
        hi, lo = _split2(w)
        return (jnp.dot(hi, expand_ref[...], preferred_element_type=F32)
                + jnp.dot(lo, expand_ref[...], preferred_element_type=F32))

    yb = spread(e0 * inv) * o0 + spread(e1 * inv) * o1 + spread(e2 * inv) * o2

    D = D_MODEL
    gate = lambda i: jax.nn.sigmoid(jnp.dot(u, wg_ref[:, i * D:(i + 1) * D], preferred_element_type=F32))
    merged = (gate(0) * jnp.dot(ya_ref[...], wa_ref[...], preferred_element_type=F32)
              + gate(1) * _dot(yb, wb_ref[...])
              + gate(2) * jnp.dot(yc_ref[...], wc_ref[...], preferred_element_type=F32))
    out_ref[...] = h + _dot(merged, wo_ref[...])


def _merge(h, T, g_mix, w_gate, y_a, o, lse, y_c, wa, wb, wc, wo, tm=512):
    M, D = h.shape
    tps = T // tm
    tok = lambda w: pl.BlockSpec((tm, w), lambda i: (i, 0))
    full = lambda shape: pl.BlockSpec(shape, lambda i: (0, 0))
    W = RW_WIDTH

    def res(group, width):
        d = DIL_PATTERNS[group][1]
        return pl.BlockSpec((None, d, tm // d, width), lambda i: (i // tps, 0, i % tps, 0))

    li = lax.broadcasted_iota(jnp.int32, (LANES, W), 0)
    lj = lax.broadcasted_iota(jnp.int32, (LANES, W), 1)
    expand = jnp.where(li == (lj // HEAD_DIM) * LSE_LANES, 1.0, 0.0).astype(BF16)

    return pl.pallas_call(
        _merge_kernel,
        grid=(M // tm,),
        in_specs=[tok(D), full((1, D)), full((D, 3 * D)), tok(W),
                  res(0, W), res(1, W), res(2, W), res(0, LANES), res(1, LANES), res(2, LANES), tok(W),
                  full((W, D)), full((W, D)), full((W, D)), full((D, D)), full((LANES, W))],
        out_specs=tok(D),
        out_shape=jax.ShapeDtypeStruct((M, D), F32),
        scratch_shapes=[pltpu.VMEM((W // LANES, tm, LANES), F32)] * 2 + [pltpu.VMEM((1, tm, LANES), F32)] * 2,
        compiler_params=_cparams("parallel"),
        name="gated_merge",
    )(h, g_mix.reshape(1, D), w_gate, y_a, o[0], o[1], o[2], lse[0], lse[1], lse[2], y_c, wa, wb, wc, wo, expand)


def _xattn_kernel(h_ref, gx_ref, wq_ref, kv_ref, qn_ref, kn_ref, wo_ref, out_ref, k_scr, v_scr):
    D = D_MODEL

    @pl.when(pl.program_id(1) == 0)
    def _():
        kv = kv_ref[...]
        for hd in range(XA_HEADS):
            sl = slice(hd * XA_HEAD_DIM, (hd + 1) * XA_HEAD_DIM)
            k_scr[:, sl] = _rms(kv[:, sl], kn_ref[...]).astype(BF16)
        v_scr[...] = kv[:, D:].astype(BF16)

    h = h_ref[...]
    hn = _rms(h, gx_ref[...]).astype(BF16)
    q = jnp.dot(hn, wq_ref[...], preferred_element_type=F32)
    outs = []
    for hd in range(XA_HEADS):
        sl = slice(hd * XA_HEAD_DIM, (hd + 1) * XA_HEAD_DIM)
        qh = _rms(q[:, sl], qn_ref[...]).astype(BF16)
        s = lax.dot_general(qh, k_scr[:, sl], (((1,), (1,)), ((), ())),
                            preferred_element_type=F32) * (XA_HEAD_DIM ** -0.5)
        m = jnp.max(s, axis=-1, keepdims=True)
        e = jnp.exp(s - m)
        pr = e / jnp.sum(e, axis=-1, keepdims=True)
        outs.append(jnp.dot(pr.astype(BF16), v_scr[:, sl], preferred_element_type=F32))
    o = jnp.concatenate(outs, axis=-1).astype(BF16)
    out_ref[...] = h + jnp.dot(o, wo_ref[...], preferred_element_type=F32)


def _cross_attention(h, B, T, g_x, wq, kv, q_norm, k_norm, wo, tm=512):
    M, D = h.shape
    tps = T // tm
    full = lambda shape: pl.BlockSpec(shape, lambda b, t: (0, 0))
    return pl.pallas_call(
        _xattn_kernel,
        grid=(B, tps),
        in_specs=[
            pl.BlockSpec((tm, D), lambda b, t: (b * tps + t, 0)),
            full((1, D)), full((D, D)),
            pl.BlockSpec((MEM_LEN, 2 * D), lambda b, t: (b, 0)),
            full((1, XA_HEAD_DIM)), full((1, XA_HEAD_DIM)), full((D, D)),
        ],
        out_specs=pl.BlockSpec((tm, D), lambda b, t: (b * tps + t, 0)),
        out_shape=jax.ShapeDtypeStruct((M, D), F32),
        scratch_shapes=[pltpu.VMEM((MEM_LEN, D), BF16), pltpu.VMEM((MEM_LEN, D), BF16)],
        compiler_params=_cparams("parallel", "arbitrary"),
        name="cross_attention",
    )(h, g_x.reshape(1, D), wq, kv, q_norm.reshape(1, -1), k_norm.reshape(1, -1), wo)


def _pad_rows(w, rows_before, total):
    return jnp.pad(w, ((rows_before, total - rows_before - w.shape[0]), (0, 0)))


def _layer(h, mem2, B, T, p, dil_tabs, ret_tabs):
    bf = lambda w: w.astype(BF16)
    ones = _block_ones(RW_WIDTH, RW_HEAD_DIM)
    h = _ffn(h, p['norm_ffn1'], bf(p['ffn1_w13']), bf(p['ffn1_w2']))

    w_in = p['w_in']
    rw_w = jnp.pad(w_in[:, :1824], ((0, 0), (0, RW_PAD_COLS - 1824)))
    mu = jnp.pad(p['rw_mu'], (0, RW_PAD_COLS - 1824))

    w2p = _pad_rows(p['rw_w2'], 0, 128)
    a2p = _pad_rows(p['rw_a2'], 64, 128)
    g2p = _pad_rows(p['rw_g2'], 0, 256)
    r, lw, k, v, a, b, g, bonus = _rwkv_prep(h, p['norm_mix'], bf(rw_w), B, T, mu, p['rw_w0'], w2p, p['rw_a0'], a2p, g2p,
                                             p['rw_k_k'], p['rw_k_a'], p['rw_r_k'], ones)
    y_a = _rwkv_recurrence(r, lw, k, v, a, b, bonus, g, p['rw_ln_w'], p['rw_ln_b'], ones)

    qkvs = _dilated_projection(h, p['norm_mix'], bf(w_in[:, 1824:6432]), B, T, dil_tabs,
                               p['dil_q_norm'], p['dil_k_norm'], ones)
    o, lse = zip(*[_dilated_attention(qkv, DIL_BLOCKS_PER_STEP) for qkv in qkvs])

    y_c = _retention(h, p['norm_mix'], bf(w_in[:, 6432:7968]), B, T, ret_tabs, p['ret_norm'])

    h = _merge(h, T, p['norm_mix'], bf(w_in[:, 7968:]), y_a, o, lse, y_c,
               bf(p['w_branch_rwkv']), bf(p['w_branch_dil']), bf(p['w_branch_ret']), bf(p['w_out']))

    kv = _norm_matmul(mem2, p['norm_mem'], bf(p['xa_wkv']), MEM_LEN, 1024)
    h = _cross_attention(h, B, T, p['norm_xattn'], bf(p['xa_wq']), kv, p['xa_q_norm'], p['xa_k_norm'],
                         bf(p['xa_wo']))
    h = _ffn(h, p['norm_ffn2'], bf(p['ffn2_w13']), bf(p['ffn2_w2']))
    return h


_PARAM_NAMES = ('norm_ffn1', 'ffn1_w13', 'ffn1_w2', 'norm_mix', 'w_in', 'rw_mu', 'rw_w0', 'rw_w2', 'rw_a0',
                'rw_a2', 'rw_g2', 'rw_k_k', 'rw_k_a', 'rw_r_k', 'rw_ln_w', 'rw_ln_b', 'dil_q_norm',
                'dil_k_norm', 'ret_norm', 'w_branch_rwkv', 'w_branch_dil', 'w_branch_ret', 'w_out',
                'norm_xattn', 'norm_mem', 'xa_wq', 'xa_wkv', 'xa_q_norm', 'xa_k_norm', 'xa_wo',
                'norm_ffn2', 'ffn2_w13', 'ffn2_w2')


def kernel(x, mem, norm_ffn1, ffn1_w13, ffn1_w2, norm_mix, w_in, rw_mu, rw_w0, rw_w2, rw_a0, rw_a2, rw_g2, rw_k_k, rw_k_a, rw_r_k, rw_ln_w, rw_ln_b, dil_q_norm, dil_k_norm, ret_norm, w_branch_rwkv, w_branch_dil, w_branch_ret, w_out, norm_xattn, norm_mem, xa_wq, xa_wkv, xa_q_norm, xa_k_norm, xa_wo, norm_ffn2, ffn2_w13, ffn2_w2):
    params = dict(zip(_PARAM_NAMES, (norm_ffn1, ffn1_w13, ffn1_w2, norm_mix, w_in, rw_mu, rw_w0, rw_w2, rw_a0,
                                     rw_a2, rw_g2, rw_k_k, rw_k_a, rw_r_k, rw_ln_w, rw_ln_b, dil_q_norm,
                                     dil_k_norm, ret_norm, w_branch_rwkv, w_branch_dil, w_branch_ret, w_out,
                                     norm_xattn, norm_mem, xa_wq, xa_wkv, xa_q_norm, xa_k_norm, xa_wo,
                                     norm_ffn2, ffn2_w13, ffn2_w2)))
    B, T, D = x.shape
    assert D == D_MODEL and T % 2048 == 0 and mem.shape[1] == MEM_LEN
    depth = norm_ffn1.shape[0]
    dil_tabs = _rope_tables(T, DIL_HEADS, HEAD_DIM, ROPE_DIM, ROPE_THETA)
    ret_tabs = _rope_tables(T, RET_HEADS, RET_QK_DIM, RET_QK_DIM, RET_ROPE_BASE)
    h = x.reshape(B * T, D)
    mem2 = mem.reshape(B * MEM_LEN, D)
    for l in range(depth):
        h = _layer(h, mem2, B, T, {n: params[n][l] for n in _PARAM_NAMES}, dil_tabs, ret_tabs)
    return h.reshape(B, T, D)
```

```python
import functools
import math

import jax
import jax.numpy as jnp
import numpy as np
from jax import lax
from jax.experimental import pallas as pl
from jax.experimental.pallas import tpu as pltpu

F32 = jnp.float32
BF16 = jnp.bfloat16

D_MODEL = 1024
D_FF = 2816
EPS = 1e-6

RW_HEADS = 8
RW_HEAD_DIM = 64
RW_WIDTH = 512
RW_GN_EPS = 64e-5
RW_CHUNK = 64
RW_PASSES_PROD = 1
RW_PASSES_INV = 1
RW_PASSES_APPLY = 1
RW_PASSES_STATE = 1
RW_PAD_COLS = 1920

DIL_PATTERNS = ((128, 1), (512, 4), (2048, 16))
N_DIL = 3
DIL_HEADS = 8
HEAD_DIM = 64
DIL_WIDTH = 512
DIL_BLOCK = 128
DIL_BLOCKS_PER_STEP = 8
LSE_LANES = 16
ROPE_THETA = 500000.0
Q_PRESCALE = HEAD_DIM ** -0.5 * math.log2(math.e)
ROPE_DIM = 16

RET_HEADS = 4
RET_QK_DIM = 64
RET_V_DIM = 128
RET_CHUNK = 128
RET_ROPE_BASE = 10000.0
RET_WIDTH = 512

XA_HEADS = 4
XA_HEAD_DIM = 256
MEM_LEN = 256

VMEM_LIMIT = 56 * 1024 * 1024
LANES = 128


def _cparams(*sem):
    return pltpu.CompilerParams(dimension_semantics=sem, vmem_limit_bytes=VMEM_LIMIT)


def _dot(a, b):
    return jnp.dot(a.astype(BF16), b.astype(BF16), preferred_element_type=F32)


def _split2(x):
    hi = x.astype(BF16)
    lo = (x - hi.astype(F32)).astype(BF16)
    return hi, lo


def _dot3(a, b):
    ah, al = _split2(a)
    bh, bl = _split2(b)
    return (jnp.dot(ah, bh, preferred_element_type=F32)
            + jnp.dot(ah, bl, preferred_element_type=F32)
            + jnp.dot(al, bh, preferred_element_type=F32))


def _rms(x, g):
    return x * lax.rsqrt(jnp.mean(x * x, axis=-1, keepdims=True) + EPS) * g


def _store_lane_groups(ref, x):
    for gl in range(ref.shape[0]):
        ref[gl] = x[:, gl * LANES:(gl + 1) * LANES]


def _block_ones(n, width):
    i = lax.broadcasted_iota(jnp.int32, (n, n), 0) // width
    j = lax.broadcasted_iota(jnp.int32, (n, n), 1) // width
    return jnp.where(i == j, 1.0, 0.0).astype(BF16)


def _ffn_kernel(h_ref, g_ref, w1_ref, w3_ref, w2_ref, o_ref, nrm_ref, acc_ref):
    j = pl.program_id(1)

    @pl.when(j == 0)
    def _():
        nrm_ref[...] = _rms(h_ref[...], g_ref[...]).astype(BF16)
        acc_ref[...] = jnp.zeros_like(acc_ref)

    n = nrm_ref[...]
    a = jnp.dot(n, w1_ref[...], preferred_element_type=F32)
    b = jnp.dot(n, w3_ref[...], preferred_element_type=F32)
    mid = (a * jax.nn.sigmoid(a) * b).astype(BF16)
    acc_ref[...] += jnp.dot(mid, w2_ref[...], preferred_element_type=F32)

    @pl.when(j == pl.num_programs(1) - 1)
    def _():
        o_ref[...] = h_ref[...] + 0.5 * acc_ref[...]


def _ffn(h, g, w13, w2, tm=1024, tf=256):
    M, D = h.shape
    nff = D_FF // tf
    return pl.pallas_call(
        _ffn_kernel,
        grid=(M // tm, nff),
        in_specs=[
            pl.BlockSpec((tm, D), lambda i, j: (i, 0)),
            pl.BlockSpec((1, D), lambda i, j: (0, 0)),
            pl.BlockSpec((D, tf), lambda i, j: (0, j)),
            pl.BlockSpec((D, tf), lambda i, j: (0, j + nff)),
            pl.BlockSpec((tf, D), lambda i, j: (j, 0)),
        ],
        out_specs=pl.BlockSpec((tm, D), lambda i, j: (i, 0)),
        out_shape=jax.ShapeDtypeStruct((M, D), F32),
        scratch_shapes=[pltpu.VMEM((tm, D), BF16), pltpu.VMEM((tm, D), F32)],
        compiler_params=_cparams("parallel", "arbitrary"),
        name="ffn",
    )(h, g.reshape(1, D), w13, w13, w2)


def _nmm_kernel(h_ref, g_ref, w_ref, o_ref, nrm_ref):
    @pl.when(pl.program_id(1) == 0)
    def _():
        nrm_ref[...] = _rms(h_ref[...], g_ref[...]).astype(BF16)

    o_ref[...] = jnp.dot(nrm_ref[...], w_ref[...], preferred_element_type=F32)


def _norm_matmul(h, g, w, tm, tn):
    M, D = h.shape
    N = w.shape[1]
    return pl.pallas_call(
        _nmm_kernel,
        grid=(M // tm, N // tn),
        in_specs=[
            pl.BlockSpec((tm, D), lambda i, j: (i, 0)),
            pl.BlockSpec((1, D), lambda i, j: (0, 0)),
            pl.BlockSpec((D, tn), lambda i, j: (0, j)),
        ],
        out_specs=pl.BlockSpec((tm, tn), lambda i, j: (i, j)),
        out_shape=jax.ShapeDtypeStruct((M, N), F32),
        scratch_shapes=[pltpu.VMEM((tm, D), BF16)],
        compiler_params=_cparams("parallel", "arbitrary"),
        name="norm_matmul",
    )(h, g.reshape(1, D), w)


def _rwprep_kernel(h_ref, gm_ref, w_ref, mu_ref, w0_ref, w2_ref, a0_ref, a2_ref, g2_ref,
                   kk_ref, ka_ref, rk_ref, ones_ref,
                   r_out, lw_out, k_out, v_out, a_out, b_out, g_out, bonus_out, last_ref, *, tiles_per_seq):
    i = pl.program_id(0)
    p = jnp.dot(_rms(h_ref[...], gm_ref[...]).astype(BF16), w_ref[...], preferred_element_type=F32)
    tm = p.shape[0]

    @pl.when(i % tiles_per_seq == 0)
    def _():
        last_ref[...] = jnp.zeros_like(last_ref)

    prev_row = last_ref[0:1, :]
    last_ref[0:1, :] = p[tm - 1:tm, :]
    rows = lax.broadcasted_iota(jnp.int32, p.shape, 0)
    shifted = jnp.where(rows == 0, prev_row, pltpu.roll(p, 1, 0))
    pm = p + (shifted - p) * mu_ref[...]

    r = pm[:, 0:512]
    k = pm[:, 512:1024]
    v = pm[:, 1024:1536]
    wa = pm[:, 1536:1664]
    gd = pm[:, 1664:1920]

    w_in = w0_ref[...] + _dot3(jnp.tanh(wa), w2_ref[...])
    w_log = -jax.nn.softplus(-w_in) - 0.5
    lw = -jnp.exp(w_log)
    a_sig = jax.nn.sigmoid(a0_ref[...] + _dot3(wa, a2_ref[...]))
    g = _dot(jax.nn.sigmoid(gd), g2_ref[...])

    ones = ones_ref[...]
    kk = k * kk_ref[...]
    ss = _dot(kk * kk, ones)
    kk = kk / jnp.maximum(jnp.sqrt(ss), 1e-12)
    k2 = k * (1.0 + (a_sig - 1.0) * ka_ref[...])
    bonus = _dot(r * k2 * rk_ref[...], ones) * v

    g_out[...] = g
    bonus_out[...] = bonus
    a_vec = -kk
    b_vec = kk * a_sig
    for h in range(RW_HEADS):
        sl = slice(h * RW_HEAD_DIM, (h + 1) * RW_HEAD_DIM)
        r_out[h] = r[:, sl]
        lw_out[h] = lw[:, sl]
        k_out[h] = k2[:, sl]
        v_out[h] = v[:, sl]
        a_out[h] = a_vec[:, sl]
        b_out[h] = b_vec[:, sl]


def _rwkv_prep(h, g_mix, w_rw, B, T, mu, w0, w2p, a0, a2p, g2p, k_k, k_a, r_k, ones, tm=512):
    M, D = h.shape
    tps = T // tm
    row = lambda x: x.reshape(1, -1)
    full = lambda shape: pl.BlockSpec(shape, lambda i: (0,) * len(shape))
    head_spec = pl.BlockSpec((None, RW_HEADS, tm, RW_HEAD_DIM), lambda i: (i // tps, 0, i % tps, 0))
    head_shape = jax.ShapeDtypeStruct((B, RW_HEADS, T, RW_HEAD_DIM), F32)
    tok_spec = pl.BlockSpec((tm, RW_WIDTH), lambda i: (i, 0))
    tok_shape = jax.ShapeDtypeStruct((M, RW_WIDTH), F32)
    return pl.pallas_call(
        functools.partial(_rwprep_kernel, tiles_per_seq=tps),
        grid=(M // tm,),
        in_specs=[
            pl.BlockSpec((tm, D), lambda i: (i, 0)), full((1, D)), full((D, RW_PAD_COLS)),
            full((1, RW_PAD_COLS)), full((1, RW_WIDTH)), full((128, RW_WIDTH)),
            full((1, RW_WIDTH)), full((128, RW_WIDTH)), full((256, RW_WIDTH)),
            full((1, RW_WIDTH)), full((1, RW_WIDTH)), full((1, RW_WIDTH)), full((RW_WIDTH, RW_WIDTH)),
        ],
        out_specs=[head_spec] * 6 + [tok_spec] * 2,
        out_shape=[head_shape] * 6 + [tok_shape] * 2,
        scratch_shapes=[pltpu.VMEM((8, RW_PAD_COLS), F32)],
        compiler_params=_cparams("arbitrary"),
        name="rwkv_prep",
    )(h, row(g_mix), w_rw, row(mu), row(w0), w2p, row(a0), a2p, g2p, row(k_k), row(k_a), row(r_k), ones)


def _bmm(a, b, passes, nt=False):
    spec = 'gik,gjk->gij' if nt else 'gij,gjk->gik'
    e = lambda x, y: jnp.einsum(spec, x, y, preferred_element_type=F32)
    if passes == 1:
        return e(a.astype(BF16), b.astype(BF16))
    ah, al = _split2(a)
    bh, bl = _split2(b)
    return e(ah, bh) + e(ah, bl) + e(al, bh)


def _rwkv_kernel(r_ref, lw_ref, k_ref, v_ref, a_ref, b_ref, bonus_ref, g_ref, lnw_ref, lnb_ref, ones_ref,
                 y_ref, st_ref, ytok_ref):
    C = RW_CHUNK
    H = RW_HEADS
    Tc = r_ref.shape[1]
    nc = Tc // C
    G = H * nc

    @pl.when(pl.program_id(1) == 0)
    def _():
        st_ref[...] = jnp.zeros_like(st_ref)

    shp = lambda ref: ref[...].reshape(G, C, RW_HEAD_DIM)
    r, lw, k, v, a, b = (shp(x) for x in (r_ref, lw_ref, k_ref, v_ref, a_ref, b_ref))

    ti = lax.broadcasted_iota(jnp.int32, (C, C), 0)
    tj = lax.broadcasted_iota(jnp.int32, (C, C), 1)
    tri = jnp.broadcast_to(jnp.where(tj <= ti, 1.0, 0.0).astype(BF16)[None], (G, C, C))
    eye = jnp.where(ti == tj, 1.0, 0.0).astype(F32)
    wi = lax.broadcasted_iota(jnp.int32, (C, 2 * C), 0)
    wj = lax.broadcasted_iota(jnp.int32, (C, 2 * C), 1)
    second = wj >= C
    wjc = jnp.where(second, wj - C, wj)

    l1 = lw.astype(BF16)
    rem = lw - l1.astype(F32)
    l2 = rem.astype(BF16)
    l3 = (rem - l2.astype(F32)).astype(BF16)
    e = lambda x, y: jnp.einsum('gij,gjk->gik', x, y, preferred_element_type=F32)
    L = e(tri, l1) + e(tri, l2) + e(tri, l3)
    Lprev = L - lw
    Ltot = L[:, C - 1:C, :]

    eL = jnp.exp(L)
    enL = jnp.exp(-L)
    eh = jnp.exp(Ltot - L)
    At = a * jnp.exp(Lprev)
    Rt = r * eL
    BKt = jnp.concatenate([b * enL, k * enL], axis=1)
    BKh = jnp.concatenate([b * eh, k * eh], axis=1)

    P4 = _bmm(jnp.concatenate([At, Rt], axis=1), BKt, RW_PASSES_PROD, nt=True)
    top, bot = P4[:, :C, :], P4[:, C:, :]
    N = jnp.where((tj < ti)[None], top[:, :, :C], 0.0)
    AKz = jnp.where((second & (wjc < wi))[None], top, 0.0)
    RBK = jnp.where((wjc <= wi)[None], bot, 0.0)
    RB = RBK[:, :, :C]

    def siblings(s):
        return ((ti // (2 * s)) == (tj // (2 * s))) & ((ti // s) != (tj // s))

    X = eye[None] + jnp.where(siblings(1)[None], N, 0.0)
    s_blk = 2
    while s_blk < C:
        XE = _bmm(X, jnp.where(siblings(s_blk)[None], N, 0.0), RW_PASSES_INV)
        X = X + _bmm(XE, X, RW_PASSES_INV)
        s_blk *= 2

    zv = jnp.concatenate([jnp.zeros_like(v), v], axis=1)
    Abar = _bmm(X, At, RW_PASSES_APPLY)
    W0 = _bmm(X, _bmm(AKz, zv, RW_PASSES_APPLY), RW_PASSES_APPLY)
    wv = jnp.concatenate([W0, v], axis=1)
    Y0 = _bmm(RBK, wv, RW_PASSES_APPLY)
    Rbar = Rt + _bmm(RB, Abar, RW_PASSES_APPLY)
    BKhT = jnp.swapaxes(BKh, 1, 2)
    Mtx = _bmm(BKhT[:, :, :C], Abar, RW_PASSES_APPLY) + eye[None] * jnp.exp(Ltot)
    G0 = _bmm(BKhT, wv, RW_PASSES_APPLY)

    hsplit = lambda x: x.reshape(H, nc, x.shape[1], x.shape[2])
    RM = hsplit(jnp.concatenate([Rbar, Mtx], axis=1))
    Y0, G0 = hsplit(Y0), hsplit(G0)
    St = st_ref[...]
    ys = []
    for c in range(nc):
        both = _bmm(RM[:, c], St, RW_PASSES_STATE)
        ys.append(both[:, :C] + Y0[:, c])
        St = both[:, C:] + G0[:, c]
    st_ref[...] = St
    y = jnp.concatenate(ys, axis=1)
    for h in range(H):
        ytok_ref[:, h * RW_HEAD_DIM:(h + 1) * RW_HEAD_DIM] = y[h]

    y = ytok_ref[...]
    ones = ones_ref[...]
    mean = _dot(y, ones) * (1.0 / RW_HEAD_DIM)
    yc = y - mean
    var = _dot(yc * yc, ones) * (1.0 / RW_HEAD_DIM)
    ya = (yc * lax.rsqrt(var + RW_GN_EPS) * lnw_ref[...] + lnb_ref[...] + bonus_ref[...]) * g_ref[...]
    y_ref[...] = ya.astype(BF16)


def _rwkv_recurrence(r, lw, k, v, a, b, bonus, g, ln_w, ln_b, ones, tc=256):
    B, H, T, Dh = r.shape
    W = H * Dh
    nt = T // tc
    spec = pl.BlockSpec((None, H, tc, Dh), lambda bi, c: (bi, 0, c, 0))
    tok = pl.BlockSpec((tc, W), lambda bi, c: (bi * nt + c, 0))
    const = lambda shape: pl.BlockSpec(shape, lambda bi, c: (0, 0))
    return pl.pallas_call(
        _rwkv_kernel,
        grid=(B, nt),
        in_specs=[spec] * 6 + [tok, tok, const((1, W)), const((1, W)), const((W, W))],
        out_specs=tok,
        out_shape=jax.ShapeDtypeStruct((B * T, W), BF16),
        scratch_shapes=[pltpu.VMEM((H, Dh, Dh), F32), pltpu.VMEM((tc, W), F32)],
        compiler_params=_cparams("parallel", "arbitrary"),
        name="rwkv_recurrence",
    )(r, lw, k, v, a, b, bonus, g, ln_w.reshape(1, W), ln_b.reshape(1, W), ones)


def _rope_tables(T, heads, head_dim, rot_dim, base):
    half = rot_dim // 2
    inv_freq = base ** (-jnp.arange(half, dtype=F32) / half)
    ang = jnp.arange(T).astype(F32)[:, None] * inv_freq[None, :]
    cos, sin = jnp.cos(ang), jnp.sin(ang)
    rest = head_dim - rot_dim
    c = jnp.concatenate([cos, cos, jnp.ones((T, rest), F32)], axis=1)
    s_lo = jnp.concatenate([-sin, jnp.zeros((T, half + rest), F32)], axis=1)
    s_hi = jnp.concatenate([jnp.zeros((T, half), F32), sin, jnp.zeros((T, rest), F32)], axis=1)
    tile = lambda x: jnp.tile(x, (1, heads))
    return tile(c), tile(s_lo), tile(s_hi)


def _apply_rope(x, c, s_lo, s_hi, half):
    n = x.shape[-1]
    return x * c + pltpu.roll(x, n - half, 1) * s_lo + pltpu.roll(x, half, 1) * s_hi


def _dilproj_kernel(h_ref, g_ref, w_ref, c_ref, slo_ref, shi_ref, gq_ref, gk_ref, ones_ref,
                    o0_ref, o1_ref, o2_ref, tmp_ref):
    u = _rms(h_ref[...], g_ref[...]).astype(BF16)
    c, s_lo, s_hi = c_ref[...], slo_ref[...], shi_ref[...]
    W = DIL_WIDTH
    for grp, o_ref in enumerate((o0_ref, o1_ref, o2_ref)):
        dilation = DIL_PATTERNS[grp][1]
        rows = u.shape[0] // dilation
        for which in range(3):
            col = (3 * grp + which) * W
            val = jnp.dot(u, w_ref[:, col:col + W], preferred_element_type=F32)
            if which < 2:
                gain = (gq_ref if which == 0 else gk_ref)[grp:grp + 1, :]
                ms = jnp.dot((val * val).astype(BF16), ones_ref[...], preferred_element_type=F32) * (1.0 / HEAD_DIM)
                val = _apply_rope(val * lax.rsqrt(ms + EPS) * gain, c, s_lo, s_hi, ROPE_DIM // 2)
            if dilation == 1:
                o_ref[which, 0] = val.astype(BF16)
            else:
                _store_lane_groups(tmp_ref, val)
                for res in range(dilation):
                    for gl in range(tmp_ref.shape[0]):
                        piece = tmp_ref[gl, pl.ds(res, rows, stride=dilation), :]
                        o_ref[which, res, :, gl * LANES:(gl + 1) * LANES] = piece.astype(BF16)


def _dilated_projection(h, g_mix, w_dil, B, T, tabs, gq, gk, ones, tm=512):
    M, D = h.shape
    tps = T // tm
    W = DIL_WIDTH
    const = lambda shape: pl.BlockSpec(shape, lambda i: (0,) * len(shape))
    tspec = pl.BlockSpec((tm, W), lambda i: (i % tps, 0))
    out_specs, out_shape = [], []
    for _, d in DIL_PATTERNS:
        out_specs.append(pl.BlockSpec((3, None, d, tm // d, W), lambda i: (0, i // tps, 0, i % tps, 0)))
        out_shape.append(jax.ShapeDtypeStruct((3, B, d, T // d, W), BF16))
    return pl.pallas_call(
        _dilproj_kernel,
        grid=(M // tm,),
        in_specs=[pl.BlockSpec((tm, D), lambda i: (i, 0)), const((1, D)), const((D, 3 * N_DIL * W)),
                  tspec, tspec, tspec, const((N_DIL, W)), const((N_DIL, W)), const((W, W))],
        out_specs=out_specs,
        out_shape=out_shape,
        scratch_shapes=[pltpu.VMEM((W // LANES, tm, LANES), F32)],
        compiler_params=_cparams("parallel"),
        name="dilated_projection",
    )(h, g_mix.reshape(1, D), w_dil, *tabs, jnp.tile(gq * Q_PRESCALE, (1, DIL_HEADS)), jnp.tile(gk, (1, DIL_HEADS)), ones)


def _dilattn_kernel(q_ref, k_ref, v_ref, kprev_ref, vprev_ref, o_ref, lse_ref, *, blocks):
    i = pl.program_id(2)
    Lb = DIL_BLOCK
    PW = 2 * HEAD_DIM
    NP = DIL_WIDTH // PW
    nres = q_ref.shape[0]
    lane = lax.broadcasted_iota(jnp.int32, (Lb, PW), 1)
    low = lane < HEAD_DIM

    qs, ks, vs = [], [], []
    for blk in range(blocks):
        rows = slice(blk * Lb, (blk + 1) * Lb)
        for rr in range(nres):
            for p in range(NP):
                ln = slice(p * PW, (p + 1) * PW)
                q2 = q_ref[rr, rows, ln]
                zero = jnp.zeros_like(q2)
                qs.append(jnp.concatenate([jnp.where(low, q2, zero), jnp.where(low, zero, q2)], axis=0))
                if blk == 0:
                    ks.append(jnp.concatenate([kprev_ref[rr, :, ln], k_ref[rr, rows, ln]], axis=0))
                    vs.append(jnp.concatenate([vprev_ref[rr, :, ln], v_ref[rr, rows, ln]], axis=0))
                else:
                    both = slice((blk - 1) * Lb, (blk + 1) * Lb)
                    ks.append(k_ref[rr, both, ln])
                    vs.append(v_ref[rr, both, ln])
    Q, K, V = jnp.stack(qs), jnp.stack(ks), jnp.stack(vs)
    first = nres * NP

    s = jnp.einsum('gqd,gkd->gqk', Q, K, preferred_element_type=F32)
    qi = lax.broadcasted_iota(jnp.int32, (2 * Lb, 2 * Lb), 0) % Lb
    kj = lax.broadcasted_iota(jnp.int32, (2 * Lb, 2 * Lb), 1)
    window = (kj >= qi) & (kj <= qi + Lb)
    window0 = window & ((kj >= Lb) | (i > 0))
    s = jnp.concatenate([jnp.where(window0[None], s[:first], -jnp.inf),
                         jnp.where(window[None], s[first:], -jnp.inf)], axis=0) if blocks > 1 else \
        jnp.where(window0[None], s, -jnp.inf)
    m = jnp.max(s, axis=-1, keepdims=True)
    e = jnp.exp2(s - m).astype(BF16)
    vx = jnp.concatenate([V, jnp.ones_like(V)], axis=-1)
    nd = jnp.einsum('gqk,gkd->gqd', e, vx, preferred_element_type=F32)
    num, den = nd[:, :, :PW], nd[:, :, PW:]
    o2 = num / den
    l2 = m * math.log(2.0) + jnp.log(den)
    head_of_lane = lane // LSE_LANES
    for blk in range(blocks):
        rows = slice(blk * Lb, (blk + 1) * Lb)
        for rr in range(nres):
            lse_c = jnp.zeros((Lb, PW), F32)
            for p in range(NP):
                g = (blk * nres + rr) * NP + p
                ln = slice(p * PW, (p + 1) * PW)
                o_ref[rr, rows, ln] = jnp.where(low, o2[g, :Lb], o2[g, Lb:]).astype(BF16)
                lse_c = jnp.where(head_of_lane == 2 * p, l2[g, :Lb],
                                  jnp.where(head_of_lane == 2 * p + 1, l2[g, Lb:], lse_c))
            lse_ref[rr, rows, :] = lse_c


def _dilated_attention(qkv):
    _, B, d, Mr, W = qkv.shape
    Lb = DIL_BLOCK
    nb = Mr // Lb
    blocks = min(DIL_BLOCKS_PER_STEP, nb)
    nres = min(DIL_BLOCKS_PER_STEP // blocks, d)
    tile = blocks * Lb

    def cur(which):
        return pl.BlockSpec((None, None, nres, tile, W), lambda b, c, i: (which, b, c, i, 0))

    def prev(which):
        return pl.BlockSpec((None, None, nres, Lb, W),
                            lambda b, c, i: (which, b, c, jnp.maximum(i * blocks - 1, 0), 0))

    ospec = pl.BlockSpec((None, nres, tile, W), lambda b, c, i: (b, c, i, 0))
    lspec = pl.BlockSpec((None, nres, tile, LANES), lambda b, c, i: (b, c, i, 0))
    return pl.pallas_call(
        functools.partial(_dilattn_kernel, blocks=blocks),
        grid=(B, d // nres, nb // blocks),
        in_specs=[cur(0), cur(1), cur(2), prev(1), prev(2)],
        out_specs=[ospec, lspec],
        out_shape=[jax.ShapeDtypeStruct((B, d, Mr, W), BF16), jax.ShapeDtypeStruct((B, d, Mr, LANES), F32)],
        compiler_params=_cparams("parallel", "parallel", "parallel"),
        name=f"dilated_attention_d{d}",
    )(qkv, qkv, qkv, qkv, qkv)


def _ret_kernel(h_ref, gm_ref, w_ref, c_ref, slo_ref, shi_ref, gain_ref, o_ref, st_ref):
    C = RET_CHUNK
    QK = RET_HEADS * RET_QK_DIM

    @pl.when(pl.program_id(1) == 0)
    def _():
        st_ref[...] = jnp.zeros_like(st_ref)

    u = _rms(h_ref[...], gm_ref[...]).astype(BF16)
    p = jnp.dot(u, w_ref[...], preferred_element_type=F32)
    c, s_lo, s_hi = c_ref[...], slo_ref[...], shi_ref[...]
    q = _apply_rope(p[:, :QK], c, s_lo, s_hi, RET_QK_DIM // 2)
    k = _apply_rope(p[:, QK:2 * QK], c, s_lo, s_hi, RET_QK_DIM // 2) * (RET_QK_DIM ** -0.5)
    v = p[:, 2 * QK:2 * QK + RET_WIDTH]
    g = p[:, 2 * QK + RET_WIDTH:]
    gain = gain_ref[...]

    ji = lax.broadcasted_iota(jnp.int32, (C, C), 0)
    jj = lax.broadcasted_iota(jnp.int32, (C, C), 1)
    diff = (ji - jj).astype(F32)
    jcol = lax.broadcasted_iota(jnp.int32, (C, 1), 0).astype(F32)

    for h in range(RET_HEADS):
        lg = math.log(1.0 - 2.0 ** (-5.0 - h))
        qs = slice(h * RET_QK_DIM, (h + 1) * RET_QK_DIM)
        vs = slice(h * RET_V_DIM, (h + 1) * RET_V_DIM)
        decay_in = jnp.where(diff >= 0, jnp.exp(lg * jnp.maximum(diff, 0.0)), 0.0)
        q_dec = jnp.exp(lg * (jcol + 1.0))
        k_dec = jnp.exp(lg * (C - 1.0 - jcol))
        S = st_ref[h]
        for ch in range(p.shape[0] // C):
            rows = slice(ch * C, (ch + 1) * C)
            qh, kh, vh = q[rows, qs], k[rows, qs], v[rows, vs]
            s = lax.dot_general(qh.astype(BF16), kh.astype(BF16), (((1,), (1,)), ((), ())),
                                preferred_element_type=F32) * decay_in
            y = _dot(s, vh) + _dot(qh * q_dec, S)
            kv = lax.dot_general((kh * k_dec).astype(BF16), vh.astype(BF16), (((0,), (0,)), ((), ())),
                                 preferred_element_type=F32)
            S = math.exp(lg * C) * S + kv
            yn = y * lax.rsqrt(jnp.mean(y * y, axis=-1, keepdims=True) + EPS) * gain[:, vs]
            gh = g[rows, vs]
            o_ref[rows, vs] = (gh * jax.nn.sigmoid(gh) * yn).astype(BF16)
        st_ref[h] = S


def _retention(h, g_mix, w_ret, B, T, tabs, gain, tr=512):
    M, D = h.shape
    nt = T // tr
    qk = RET_HEADS * RET_QK_DIM
    const = lambda shape: pl.BlockSpec(shape, lambda b, t: (0,) * len(shape))
    tspec = pl.BlockSpec((tr, qk), lambda b, t: (t, 0))
    return pl.pallas_call(
        _ret_kernel,
        grid=(B, nt),
        in_specs=[pl.BlockSpec((tr, D), lambda b, t: (b * nt + t, 0)), const((1, D)), const(w_ret.shape),
                  tspec, tspec, tspec, const((1, RET_WIDTH))],
        out_specs=pl.BlockSpec((tr, RET_WIDTH), lambda b, t: (b * nt + t, 0)),
        out_shape=jax.ShapeDtypeStruct((M, RET_WIDTH), BF16),
        scratch_shapes=[pltpu.VMEM((RET_HEADS, RET_QK_DIM, RET_V_DIM), F32)],
        compiler_params=_cparams("parallel", "arbitrary"),
        name="retention",
    )(h, g_mix.reshape(1, D), w_ret, *tabs, gain.reshape(1, -1))


def _merge_kernel(h_ref, gm_ref, wg_ref, ya_ref, o0_ref, o1_ref, o2_ref, l0_ref, l1_ref, l2_ref, yc_ref,
                  wa_ref, wb_ref, wc_ref, wo_ref, expand_ref, out_ref, o1_scr, o2_scr, l1_scr, l2_scr):
    h = h_ref[...]
    u = _rms(h, gm_ref[...]).astype(BF16)

    def token_order(ref, scr):
        dil, rows = ref.shape[0], ref.shape[1]
        for c in range(dil):
            blk = ref[c].astype(F32)
            for gl in range(scr.shape[0]):
                scr[gl, pl.ds(c, rows, stride=dil), :] = blk[:, gl * LANES:(gl + 1) * LANES]
        return jnp.concatenate([scr[gl] for gl in range(scr.shape[0])], axis=-1)

    o0, l0 = o0_ref[0].astype(F32), l0_ref[0]
    o1, l1 = token_order(o1_ref, o1_scr), token_order(l1_ref, l1_scr)
    o2, l2 = token_order(o2_ref, o2_scr), token_order(l2_ref, l2_scr)

    mx = jnp.maximum(jnp.maximum(l0, l1), l2)
    e0, e1, e2 = jnp.exp(l0 - mx), jnp.exp(l1 - mx), jnp.exp(l2 - mx)
    inv = 1.0 / (e0 + e1 + e2)

    def spread(w):
        hi, lo = _split2(w)
        return (jnp.dot(hi, expand_ref[...], preferred_element_type=F32)
                + jnp.dot(lo, expand_ref[...], preferred_element_type=F32))

    yb = spread(e0 * inv) * o0 + spread(e1 * inv) * o1 + spread(e2 * inv) * o2

    D = D_MODEL
    gate = lambda i: jax.nn.sigmoid(jnp.dot(u, wg_ref[:, i * D:(i + 1) * D], preferred_element_type=F32))
    merged = (gate(0) * jnp.dot(ya_ref[...], wa_ref[...], preferred_element_type=F32)
              + gate(1) * _dot(yb, wb_ref[...])
              + gate(2) * jnp.dot(yc_ref[...], wc_ref[...], preferred_element_type=F32))
    out_ref[...] = h + _dot(merged, wo_ref[...])


def _merge(h, T, g_mix, w_gate, y_a, o, lse, y_c, wa, wb, wc, wo, tm=512):
    M, D = h.shape
    tps = T // tm
    tok = lambda w: pl.BlockSpec((tm, w), lambda i: (i, 0))
    full = lambda shape: pl.BlockSpec(shape, lambda i: (0, 0))
    W = RW_WIDTH

    def res(group, width):
        d = DIL_PATTERNS[group][1]
        return pl.BlockSpec((None, d, tm // d, width), lambda i: (i // tps, 0, i % tps, 0))

    li = lax.broadcasted_iota(jnp.int32, (LANES, W), 0)
    lj = lax.broadcasted_iota(jnp.int32, (LANES, W), 1)
    expand = jnp.where(li == (lj // HEAD_DIM) * LSE_LANES, 1.0, 0.0).astype(BF16)

    return pl.pallas_call(
        _merge_kernel,
        grid=(M // tm,),
        in_specs=[tok(D), full((1, D)), full((D, 3 * D)), tok(W),
                  res(0, W), res(1, W), res(2, W), res(0, LANES), res(1, LANES), res(2, LANES), tok(W),
                  full((W, D)), full((W, D)), full((W, D)), full((D, D)), full((LANES, W))],
        out_specs=tok(D),
        out_shape=jax.ShapeDtypeStruct((M, D), F32),
        scratch_shapes=[pltpu.VMEM((W // LANES, tm, LANES), F32)] * 2 + [pltpu.VMEM((1, tm, LANES), F32)] * 2,
        compiler_params=_cparams("parallel"),
        name="gated_merge",
    )(h, g_mix.reshape(1, D), w_gate, y_a, o[0], o[1], o[2], lse[0], lse[1], lse[2], y_c, wa, wb, wc, wo, expand)


def _xattn_kernel(h_ref, gx_ref, wq_ref, kv_ref, qn_ref, kn_ref, wo_ref, out_ref, k_scr, v_scr):
    D = D_MODEL

    @pl.when(pl.program_id(1) == 0)
    def _():
        kv = kv_ref[...]
        for hd in range(XA_HEADS):
            sl = slice(hd * XA_HEAD_DIM, (hd + 1) * XA_HEAD_DIM)
            k_scr[:, sl] = _rms(kv[:, sl], kn_ref[...]).astype(BF16)
        v_scr[...] = kv[:, D:].astype(BF16)

    h = h_ref[...]
    hn = _rms(h, gx_ref[...]).astype(BF16)
    q = jnp.dot(hn, wq_ref[...], preferred_element_type=F32)
    outs = []
    for hd in range(XA_HEADS):
        sl = slice(hd * XA_HEAD_DIM, (hd + 1) * XA_HEAD_DIM)
        qh = _rms(q[:, sl], qn_ref[...]).astype(BF16)
        s = lax.dot_general(qh, k_scr[:, sl], (((1,), (1,)), ((), ())), preferred_element_type=F32)
        m = jnp.max(s, axis=-1, keepdims=True)
        e = jnp.exp2(s - m)
        pr = e / jnp.sum(e, axis=-1, keepdims=True)
        outs.append(jnp.dot(pr.astype(BF16), v_scr[:, sl], preferred_element_type=F32))
    o = jnp.concatenate(outs, axis=-1).astype(BF16)
    out_ref[...] = h + jnp.dot(o, wo_ref[...], preferred_element_type=F32)


def _cross_attention(h, B, T, g_x, wq, kv, q_norm, k_norm, wo, tm=512):
    M, D = h.shape
    tps = T // tm
    full = lambda shape: pl.BlockSpec(shape, lambda b, t: (0, 0))
    return pl.pallas_call(
        _xattn_kernel,
        grid=(B, tps),
        in_specs=[
            pl.BlockSpec((tm, D), lambda b, t: (b * tps + t, 0)),
            full((1, D)), full((D, D)),
            pl.BlockSpec((MEM_LEN, 2 * D), lambda b, t: (b, 0)),
            full((1, XA_HEAD_DIM)), full((1, XA_HEAD_DIM)), full((D, D)),
        ],
        out_specs=pl.BlockSpec((tm, D), lambda b, t: (b * tps + t, 0)),
        out_shape=jax.ShapeDtypeStruct((M, D), F32),
        scratch_shapes=[pltpu.VMEM((MEM_LEN, D), BF16), pltpu.VMEM((MEM_LEN, D), BF16)],
        compiler_params=_cparams("parallel", "arbitrary"),
        name="cross_attention",
    )(h, g_x.reshape(1, D), wq, kv, (q_norm * (XA_HEAD_DIM ** -0.5 * math.log2(math.e))).reshape(1, -1),
      k_norm.reshape(1, -1), wo)


def _pad_rows(w, rows_before, total):
    return jnp.pad(w, ((rows_before, total - rows_before - w.shape[0]), (0, 0)))


def _layer(h, mem2, B, T, p, dil_tabs, ret_tabs):
    bf = lambda w: w.astype(BF16)
    ones = _block_ones(RW_WIDTH, RW_HEAD_DIM)
    h = _ffn(h, p['norm_ffn1'], bf(p['ffn1_w13']), bf(p['ffn1_w2']))

    w_in = p['w_in']
    rw_w = jnp.pad(w_in[:, :1824], ((0, 0), (0, RW_PAD_COLS - 1824)))
    mu = jnp.pad(p['rw_mu'], (0, RW_PAD_COLS - 1824))

    w2p = _pad_rows(p['rw_w2'], 0, 128)
    a2p = _pad_rows(p['rw_a2'], 64, 128)
    g2p = _pad_rows(p['rw_g2'], 0, 256)
    r, lw, k, v, a, b, g, bonus = _rwkv_prep(h, p['norm_mix'], bf(rw_w), B, T, mu, p['rw_w0'], w2p, p['rw_a0'], a2p, g2p,
                                             p['rw_k_k'], p['rw_k_a'], p['rw_r_k'], ones)
    y_a = _rwkv_recurrence(r, lw, k, v, a, b, bonus, g, p['rw_ln_w'], p['rw_ln_b'], ones)

    qkvs = _dilated_projection(h, p['norm_mix'], bf(w_in[:, 1824:6432]), B, T, dil_tabs,
                               p['dil_q_norm'], p['dil_k_norm'], ones)
    o, lse = zip(*[_dilated_attention(qkv) for qkv in qkvs])

    y_c = _retention(h, p['norm_mix'], bf(w_in[:, 6432:7968]), B, T, ret_tabs, p['ret_norm'])

    h = _merge(h, T, p['norm_mix'], bf(w_in[:, 7968:]), y_a, o, lse, y_c,
               bf(p['w_branch_rwkv']), bf(p['w_branch_dil']), bf(p['w_branch_ret']), bf(p['w_out']))

    kv = _norm_matmul(mem2, p['norm_mem'], bf(p['xa_wkv']), MEM_LEN, 1024)
    h = _cross_attention(h, B, T, p['norm_xattn'], bf(p['xa_wq']), kv, p['xa_q_norm'], p['xa_k_norm'],
                         bf(p['xa_wo']))
    h = _ffn(h, p['norm_ffn2'], bf(p['ffn2_w13']), bf(p['ffn2_w2']))
    return h


_PARAM_NAMES = ('norm_ffn1', 'ffn1_w13', 'ffn1_w2', 'norm_mix', 'w_in', 'rw_mu', 'rw_w0', 'rw_w2', 'rw_a0',
                'rw_a2', 'rw_g2', 'rw_k_k', 'rw_k_a', 'rw_r_k', 'rw_ln_w', 'rw_ln_b', 'dil_q_norm',
                'dil_k_norm', 'ret_norm', 'w_branch_rwkv', 'w_branch_dil', 'w_branch_ret', 'w_out',
                'norm_xattn', 'norm_mem', 'xa_wq', 'xa_wkv', 'xa_q_norm', 'xa_k_norm', 'xa_wo',
                'norm_ffn2', 'ffn2_w13', 'ffn2_w2')


def kernel(x, mem, norm_ffn1, ffn1_w13, ffn1_w2, norm_mix, w_in, rw_mu, rw_w0, rw_w2, rw_a0, rw_a2, rw_g2, rw_k_k, rw_k_a, rw_r_k, rw_ln_w, rw_ln_b, dil_q_norm, dil_k_norm, ret_norm, w_branch_rwkv, w_branch_dil, w_branch_ret, w_out, norm_xattn, norm_mem, xa_wq, xa_wkv, xa_q_norm, xa_k_norm, xa_wo, norm_ffn2, ffn2_w13, ffn2_w2):
    params = dict(zip(_PARAM_NAMES, (norm_ffn1, ffn1_w13, ffn1_w2, norm_mix, w_in, rw_mu, rw_w0, rw_w2, rw_a0,
                                     rw_a2, rw_g2, rw_k_k, rw_k_a, rw_r_k, rw_ln_w, rw_ln_b, dil_q_norm,
                                     dil_k_norm, ret_norm, w_branch_rwkv, w_branch_dil, w_branch_ret, w_out,
                                     norm_xattn, norm_mem, xa_wq, xa_wkv, xa_q_norm, xa_k_norm, xa_wo,
                                     norm_ffn2, ffn2_w13, ffn2_w2)))
    B, T, D = x.shape
    assert D == D_MODEL and T % 2048 == 0 and mem.shape[1] == MEM_LEN
    depth = norm_ffn1.shape[0]
    dil_tabs = _rope_tables(T, DIL_HEADS, HEAD_DIM, ROPE_DIM, ROPE_THETA)
    ret_tabs = _rope_tables(T, RET_HEADS, RET_QK_DIM, RET_QK_DIM, RET_ROPE_BASE)
    h = x.reshape(B * T, D)
    mem2 = mem.reshape(B * MEM_LEN, D)
    for l in range(depth):
        h = _layer(h, mem2, B, T, {n: params[n][l] for n in _PARAM_NAMES}, dil_tabs, ret_tabs)
    return h.reshape(B, T, D)
```

```python
import functools
import math

import jax
import jax.numpy as jnp
import numpy as np
from jax import lax
from jax.experimental import pallas as pl
from jax.experimental.pallas import tpu as pltpu

F32 = jnp.float32
BF16 = jnp.bfloat16

D_MODEL = 1024
D_FF = 2816
EPS = 1e-6

RW_HEADS = 8
RW_HEAD_DIM = 64
RW_WIDTH = 512
RW_GN_EPS = 64e-5
RW_CHUNK = 64
RW_PASSES_PROD = 1
RW_PASSES_INV = 1
RW_PASSES_APPLY = 1
RW_PASSES_STATE = 1
RW_PAD_COLS = 1920

DIL_PATTERNS = ((128, 1), (512, 4), (2048, 16))
N_DIL = 3
DIL_HEADS = 8
HEAD_DIM = 64
DIL_WIDTH = 512
DIL_BLOCK = 128
DIL_BLOCKS_PER_STEP = 8
LSE_LANES = 16
ROPE_THETA = 500000.0
Q_PRESCALE = HEAD_DIM ** -0.5 * math.log2(math.e)
ROPE_DIM = 16

RET_HEADS = 4
RET_QK_DIM = 64
RET_V_DIM = 128
RET_CHUNK = 128
RET_ROPE_BASE = 10000.0
RET_WIDTH = 512

XA_HEADS = 4
XA_HEAD_DIM = 256
MEM_LEN = 256

VMEM_LIMIT = 56 * 1024 * 1024
LANES = 128


def _cparams(*sem):
    return pltpu.CompilerParams(dimension_semantics=sem, vmem_limit_bytes=VMEM_LIMIT)


def _dot(a, b):
    return jnp.dot(a.astype(BF16), b.astype(BF16), preferred_element_type=F32)


def _split2(x):
    hi = x.astype(BF16)
    lo = (x - hi.astype(F32)).astype(BF16)
    return hi, lo


def _dot3(a, b):
    ah, al = _split2(a)
    bh, bl = _split2(b)
    return (jnp.dot(ah, bh, preferred_element_type=F32)
            + jnp.dot(ah, bl, preferred_element_type=F32)
            + jnp.dot(al, bh, preferred_element_type=F32))


def _rms(x, g):
    return x * lax.rsqrt(jnp.mean(x * x, axis=-1, keepdims=True) + EPS) * g


def _store_lane_groups(ref, x):
    for gl in range(ref.shape[0]):
        ref[gl] = x[:, gl * LANES:(gl + 1) * LANES]


def _block_ones(n, width):
    i = lax.broadcasted_iota(jnp.int32, (n, n), 0) // width
    j = lax.broadcasted_iota(jnp.int32, (n, n), 1) // width
    return jnp.where(i == j, 1.0, 0.0).astype(BF16)


def _ffn_kernel(h_ref, g_ref, w1_ref, w3_ref, w2_ref, o_ref, nrm_ref, acc_ref):
    j = pl.program_id(1)

    @pl.when(j == 0)
    def _():
        nrm_ref[...] = _rms(h_ref[...], g_ref[...]).astype(BF16)
        acc_ref[...] = jnp.zeros_like(acc_ref)

    n = nrm_ref[...]
    a = jnp.dot(n, w1_ref[...], preferred_element_type=F32)
    b = jnp.dot(n, w3_ref[...], preferred_element_type=F32)
    mid = (a * jax.nn.sigmoid(a) * b).astype(BF16)
    acc_ref[...] += jnp.dot(mid, w2_ref[...], preferred_element_type=F32)

    @pl.when(j == pl.num_programs(1) - 1)
    def _():
        o_ref[...] = h_ref[...] + 0.5 * acc_ref[...]


def _ffn(h, g, w13, w2, tm=1024, tf=256):
    M, D = h.shape
    nff = D_FF // tf
    return pl.pallas_call(
        _ffn_kernel,
        grid=(M // tm, nff),
        in_specs=[
            pl.BlockSpec((tm, D), lambda i, j: (i, 0)),
            pl.BlockSpec((1, D), lambda i, j: (0, 0)),
            pl.BlockSpec((D, tf), lambda i, j: (0, j)),
            pl.BlockSpec((D, tf), lambda i, j: (0, j + nff)),
            pl.BlockSpec((tf, D), lambda i, j: (j, 0)),
        ],
        out_specs=pl.BlockSpec((tm, D), lambda i, j: (i, 0)),
        out_shape=jax.ShapeDtypeStruct((M, D), F32),
        scratch_shapes=[pltpu.VMEM((tm, D), BF16), pltpu.VMEM((tm, D), F32)],
        compiler_params=_cparams("parallel", "arbitrary"),
        name="ffn",
    )(h, g.reshape(1, D), w13, w13, w2)


def _nmm_kernel(h_ref, g_ref, w_ref, o_ref, nrm_ref):
    @pl.when(pl.program_id(1) == 0)
    def _():
        nrm_ref[...] = _rms(h_ref[...], g_ref[...]).astype(BF16)

    o_ref[...] = jnp.dot(nrm_ref[...], w_ref[...], preferred_element_type=F32)


def _norm_matmul(h, g, w, tm, tn):
    M, D = h.shape
    N = w.shape[1]
    return pl.pallas_call(
        _nmm_kernel,
        grid=(M // tm, N // tn),
        in_specs=[
            pl.BlockSpec((tm, D), lambda i, j: (i, 0)),
            pl.BlockSpec((1, D), lambda i, j: (0, 0)),
            pl.BlockSpec((D, tn), lambda i, j: (0, j)),
        ],
        out_specs=pl.BlockSpec((tm, tn), lambda i, j: (i, j)),
        out_shape=jax.ShapeDtypeStruct((M, N), F32),
        scratch_shapes=[pltpu.VMEM((tm, D), BF16)],
        compiler_params=_cparams("parallel", "arbitrary"),
        name="norm_matmul",
    )(h, g.reshape(1, D), w)


def _rwprep_kernel(h_ref, gm_ref, w_ref, mu_ref, w0_ref, w2_ref, a0_ref, a2_ref, g2_ref,
                   kk_ref, ka_ref, rk_ref, ones_ref,
                   r_out, lw_out, cum_out, k_out, v_out, a_out, b_out, g_out, bonus_out, last_ref, *, tiles_per_seq):
    i = pl.program_id(0)
    p = jnp.dot(_rms(h_ref[...], gm_ref[...]).astype(BF16), w_ref[...], preferred_element_type=F32)
    tm = p.shape[0]

    @pl.when(i % tiles_per_seq == 0)
    def _():
        last_ref[...] = jnp.zeros_like(last_ref)

    prev_row = last_ref[0:1, :]
    last_ref[0:1, :] = p[tm - 1:tm, :]
    rows = lax.broadcasted_iota(jnp.int32, p.shape, 0)
    shifted = jnp.where(rows == 0, prev_row, pltpu.roll(p, 1, 0))
    pm = p + (shifted - p) * mu_ref[...]

    r = pm[:, 0:512]
    k = pm[:, 512:1024]
    v = pm[:, 1024:1536]
    wa = pm[:, 1536:1664]
    gd = pm[:, 1664:1920]

    w_in = w0_ref[...] + _dot3(jnp.tanh(wa), w2_ref[...])
    w_log = -jax.nn.softplus(-w_in) - 0.5
    lw = -jnp.exp(w_log)
    a_sig = jax.nn.sigmoid(a0_ref[...] + _dot3(wa, a2_ref[...]))

    C = RW_CHUNK
    ci = lax.broadcasted_iota(jnp.int32, (C, C), 0)
    cj = lax.broadcasted_iota(jnp.int32, (C, C), 1)
    tri = jnp.where(cj <= ci, 1.0, 0.0).astype(BF16)
    l1 = lw.astype(BF16)
    rem = lw - l1.astype(F32)
    l2 = rem.astype(BF16)
    l3 = (rem - l2.astype(F32)).astype(BF16)
    td = lambda x: jnp.dot(tri, x, preferred_element_type=F32)
    cum = jnp.concatenate([td(l1[c0:c0 + C]) + td(l2[c0:c0 + C]) + td(l3[c0:c0 + C])
                           for c0 in range(0, tm, C)], axis=0)
    g = _dot(jax.nn.sigmoid(gd), g2_ref[...])

    ones = ones_ref[...]
    kk = k * kk_ref[...]
    ss = _dot(kk * kk, ones)
    kk = kk / jnp.maximum(jnp.sqrt(ss), 1e-12)
    k2 = k * (1.0 + (a_sig - 1.0) * ka_ref[...])
    bonus = _dot(r * k2 * rk_ref[...], ones) * v

    g_out[...] = g
    bonus_out[...] = bonus
    a_vec = -kk
    b_vec = kk * a_sig
    for h in range(RW_HEADS):
        sl = slice(h * RW_HEAD_DIM, (h + 1) * RW_HEAD_DIM)
        r_out[h] = r[:, sl]
        lw_out[h] = lw[:, sl]
        cum_out[h] = cum[:, sl]
        k_out[h] = k2[:, sl]
        v_out[h] = v[:, sl]
        a_out[h] = a_vec[:, sl]
        b_out[h] = b_vec[:, sl]


def _rwkv_prep(h, g_mix, w_rw, B, T, mu, w0, w2p, a0, a2p, g2p, k_k, k_a, r_k, ones, tm=512):
    M, D = h.shape
    tps = T // tm
    row = lambda x: x.reshape(1, -1)
    full = lambda shape: pl.BlockSpec(shape, lambda i: (0,) * len(shape))
    head_spec = pl.BlockSpec((None, RW_HEADS, tm, RW_HEAD_DIM), lambda i: (i // tps, 0, i % tps, 0))
    head_shape = jax.ShapeDtypeStruct((B, RW_HEADS, T, RW_HEAD_DIM), F32)
    tok_spec = pl.BlockSpec((tm, RW_WIDTH), lambda i: (i, 0))
    tok_shape = jax.ShapeDtypeStruct((M, RW_WIDTH), F32)
    return pl.pallas_call(
        functools.partial(_rwprep_kernel, tiles_per_seq=tps),
        grid=(M // tm,),
        in_specs=[
            pl.BlockSpec((tm, D), lambda i: (i, 0)), full((1, D)), full((D, RW_PAD_COLS)),
            full((1, RW_PAD_COLS)), full((1, RW_WIDTH)), full((128, RW_WIDTH)),
            full((1, RW_WIDTH)), full((128, RW_WIDTH)), full((256, RW_WIDTH)),
            full((1, RW_WIDTH)), full((1, RW_WIDTH)), full((1, RW_WIDTH)), full((RW_WIDTH, RW_WIDTH)),
        ],
        out_specs=[head_spec] * 7 + [tok_spec] * 2,
        out_shape=[head_shape] * 7 + [tok_shape] * 2,
        scratch_shapes=[pltpu.VMEM((8, RW_PAD_COLS), F32)],
        compiler_params=_cparams("arbitrary"),
        name="rwkv_prep",
    )(h, row(g_mix), w_rw, row(mu), row(w0), w2p, row(a0), a2p, g2p, row(k_k), row(k_a), row(r_k), ones)


def _bmm(a, b, passes, nt=False):
    spec = 'gik,gjk->gij' if nt else 'gij,gjk->gik'
    e = lambda x, y: jnp.einsum(spec, x, y, preferred_element_type=F32)
    if passes == 1:
        return e(a.astype(BF16), b.astype(BF16))
    ah, al = _split2(a)
    bh, bl = _split2(b)
    return e(ah, bh) + e(ah, bl) + e(al, bh)


def _rwkv_kernel(r_ref, lw_ref, cum_ref, k_ref, v_ref, a_ref, b_ref, bonus_ref, g_ref, lnw_ref, lnb_ref, ones_ref,
                 y_ref, st_ref, ytok_ref):
    C = RW_CHUNK
    H = RW_HEADS
    Tc = r_ref.shape[1]
    nc = Tc // C
    G = H * nc

    @pl.when(pl.program_id(1) == 0)
    def _():
        st_ref[...] = jnp.zeros_like(st_ref)

    shp = lambda ref: ref[...].reshape(G, C, RW_HEAD_DIM)
    r, lw, L, k, v, a, b = (shp(x) for x in (r_ref, lw_ref, cum_ref, k_ref, v_ref, a_ref, b_ref))

    ti = lax.broadcasted_iota(jnp.int32, (C, C), 0)
    tj = lax.broadcasted_iota(jnp.int32, (C, C), 1)
    eye = jnp.where(ti == tj, 1.0, 0.0).astype(F32)
    wi = lax.broadcasted_iota(jnp.int32, (C, 2 * C), 0)
    wj = lax.broadcasted_iota(jnp.int32, (C, 2 * C), 1)
    second = wj >= C
    wjc = jnp.where(second, wj - C, wj)

    Lprev = L - lw
    Ltot = L[:, C - 1:C, :]

    eL = jnp.exp(L)
    enL = jnp.exp(-L)
    eh = jnp.exp(Ltot - L)
    At = a * jnp.exp(Lprev)
    Rt = r * eL
    BKt = jnp.concatenate([b * enL, k * enL], axis=1)
    BKh = jnp.concatenate([b * eh, k * eh], axis=1)

    P4 = _bmm(jnp.concatenate([At, Rt], axis=1), BKt, RW_PASSES_PROD, nt=True)
    top, bot = P4[:, :C, :], P4[:, C:, :]
    N = jnp.where((tj < ti)[None], top[:, :, :C], 0.0)
    AKz = jnp.where((second & (wjc < wi))[None], top, 0.0)
    RBK = jnp.where((wjc <= wi)[None], bot, 0.0)
    RB = RBK[:, :, :C]

    def siblings(s):
        return ((ti // (2 * s)) == (tj // (2 * s))) & ((ti // s) != (tj // s))

    X = eye[None] + jnp.where(siblings(1)[None], N, 0.0)
    s_blk = 2
    while s_blk < C:
        XE = _bmm(X, jnp.where(siblings(s_blk)[None], N, 0.0), RW_PASSES_INV)
        X = X + _bmm(XE, X, RW_PASSES_INV)
        s_blk *= 2

    zv = jnp.concatenate([jnp.zeros_like(v), v], axis=1)
    Abar = _bmm(X, At, RW_PASSES_APPLY)
    W0 = _bmm(X, _bmm(AKz, zv, RW_PASSES_APPLY), RW_PASSES_APPLY)
    wv = jnp.concatenate([W0, v], axis=1)
    Y0 = _bmm(RBK, wv, RW_PASSES_APPLY)
    Rbar = Rt + _bmm(RB, Abar, RW_PASSES_APPLY)
    BKhT = jnp.swapaxes(BKh, 1, 2)
    Mtx = _bmm(BKhT[:, :, :C], Abar, RW_PASSES_APPLY) + eye[None] * jnp.exp(Ltot)
    G0 = _bmm(BKhT, wv, RW_PASSES_APPLY)

    hsplit = lambda x: x.reshape(H, nc, x.shape[1], x.shape[2])
    RM = hsplit(jnp.concatenate([Rbar, Mtx], axis=1))
    Y0, G0 = hsplit(Y0), hsplit(G0)
    St = st_ref[...]
    ys = []
    for c in range(nc):
        both = _bmm(RM[:, c], St, RW_PASSES_STATE)
        ys.append(both[:, :C] + Y0[:, c])
        St = both[:, C:] + G0[:, c]
    st_ref[...] = St
    y = jnp.concatenate(ys, axis=1)
    for h in range(H):
        ytok_ref[:, h * RW_HEAD_DIM:(h + 1) * RW_HEAD_DIM] = y[h]

    y = ytok_ref[...]
    ones = ones_ref[...]
    mean = _dot(y, ones) * (1.0 / RW_HEAD_DIM)
    yc = y - mean
    var = _dot(yc * yc, ones) * (1.0 / RW_HEAD_DIM)
    ya = (yc * lax.rsqrt(var + RW_GN_EPS) * lnw_ref[...] + lnb_ref[...] + bonus_ref[...]) * g_ref[...]
    y_ref[...] = ya.astype(BF16)


def _rwkv_recurrence(r, lw, cum, k, v, a, b, bonus, g, ln_w, ln_b, ones, tc=256):
    B, H, T, Dh = r.shape
    W = H * Dh
    nt = T // tc
    spec = pl.BlockSpec((None, H, tc, Dh), lambda bi, c: (bi, 0, c, 0))
    tok = pl.BlockSpec((tc, W), lambda bi, c: (bi * nt + c, 0))
    const = lambda shape: pl.BlockSpec(shape, lambda bi, c: (0, 0))
    return pl.pallas_call(
        _rwkv_kernel,
        grid=(B, nt),
        in_specs=[spec] * 7 + [tok, tok, const((1, W)), const((1, W)), const((W, W))],
        out_specs=tok,
        out_shape=jax.ShapeDtypeStruct((B * T, W), BF16),
        scratch_shapes=[pltpu.VMEM((H, Dh, Dh), F32), pltpu.VMEM((tc, W), F32)],
        compiler_params=_cparams("parallel", "arbitrary"),
        name="rwkv_recurrence",
    )(r, lw, cum, k, v, a, b, bonus, g, ln_w.reshape(1, W), ln_b.reshape(1, W), ones)


def _rope_tables(T, heads, head_dim, rot_dim, base):
    half = rot_dim // 2
    inv_freq = base ** (-jnp.arange(half, dtype=F32) / half)
    ang = jnp.arange(T).astype(F32)[:, None] * inv_freq[None, :]
    cos, sin = jnp.cos(ang), jnp.sin(ang)
    rest = head_dim - rot_dim
    c = jnp.concatenate([cos, cos, jnp.ones((T, rest), F32)], axis=1)
    s_lo = jnp.concatenate([-sin, jnp.zeros((T, half + rest), F32)], axis=1)
    s_hi = jnp.concatenate([jnp.zeros((T, half), F32), sin, jnp.zeros((T, rest), F32)], axis=1)
    tile = lambda x: jnp.tile(x, (1, heads))
    return tile(c), tile(s_lo), tile(s_hi)


def _apply_rope(x, c, s_lo, s_hi, half):
    n = x.shape[-1]
    return x * c + pltpu.roll(x, n - half, 1) * s_lo + pltpu.roll(x, half, 1) * s_hi


def _dilproj_kernel(h_ref, g_ref, w_ref, c_ref, slo_ref, shi_ref, gq_ref, gk_ref, ones_ref,
                    o0_ref, o1_ref, o2_ref, tmp_ref):
    u = _rms(h_ref[...], g_ref[...]).astype(BF16)
    c, s_lo, s_hi = c_ref[...], slo_ref[...], shi_ref[...]
    W = DIL_WIDTH
    for grp, o_ref in enumerate((o0_ref, o1_ref, o2_ref)):
        dilation = DIL_PATTERNS[grp][1]
        rows = u.shape[0] // dilation
        for which in range(3):
            col = (3 * grp + which) * W
            val = jnp.dot(u, w_ref[:, col:col + W], preferred_element_type=F32)
            if which < 2:
                gain = (gq_ref if which == 0 else gk_ref)[grp:grp + 1, :]
                ms = jnp.dot((val * val).astype(BF16), ones_ref[...], preferred_element_type=F32) * (1.0 / HEAD_DIM)
                val = _apply_rope(val * lax.rsqrt(ms + EPS) * gain, c, s_lo, s_hi, ROPE_DIM // 2)
            if dilation == 1:
                o_ref[which, 0] = val.astype(BF16)
            else:
                _store_lane_groups(tmp_ref, val)
                for res in range(dilation):
                    for gl in range(tmp_ref.shape[0]):
                        piece = tmp_ref[gl, pl.ds(res, rows, stride=dilation), :]
                        o_ref[which, res, :, gl * LANES:(gl + 1) * LANES] = piece.astype(BF16)


def _dilated_projection(h, g_mix, w_dil, B, T, tabs, gq, gk, ones, tm=512):
    M, D = h.shape
    tps = T // tm
    W = DIL_WIDTH
    const = lambda shape: pl.BlockSpec(shape, lambda i: (0,) * len(shape))
    tspec = pl.BlockSpec((tm, W), lambda i: (i % tps, 0))
    out_specs, out_shape = [], []
    for _, d in DIL_PATTERNS:
        out_specs.append(pl.BlockSpec((3, None, d, tm // d, W), lambda i: (0, i // tps, 0, i % tps, 0)))
        out_shape.append(jax.ShapeDtypeStruct((3, B, d, T // d, W), BF16))
    return pl.pallas_call(
        _dilproj_kernel,
        grid=(M // tm,),
        in_specs=[pl.BlockSpec((tm, D), lambda i: (i, 0)), const((1, D)), const((D, 3 * N_DIL * W)),
                  tspec, tspec, tspec, const((N_DIL, W)), const((N_DIL, W)), const((W, W))],
        out_specs=out_specs,
        out_shape=out_shape,
        scratch_shapes=[pltpu.VMEM((W // LANES, tm, LANES), F32)],
        compiler_params=_cparams("parallel"),
        name="dilated_projection",
    )(h, g_mix.reshape(1, D), w_dil, *tabs, jnp.tile(gq * Q_PRESCALE, (1, DIL_HEADS)), jnp.tile(gk, (1, DIL_HEADS)), ones)


def _dilattn_kernel(q_ref, k_ref, v_ref, kprev_ref, vprev_ref, o_ref, lse_ref, *, blocks):
    i = pl.program_id(2)
    Lb = DIL_BLOCK
    PW = 2 * HEAD_DIM
    NP = DIL_WIDTH // PW
    nres = q_ref.shape[0]
    lane = lax.broadcasted_iota(jnp.int32, (Lb, PW), 1)
    low = lane < HEAD_DIM

    qs, ks, vs = [], [], []
    for blk in range(blocks):
        rows = slice(blk * Lb, (blk + 1) * Lb)
        for rr in range(nres):
            for p in range(NP):
                ln = slice(p * PW, (p + 1) * PW)
                q2 = q_ref[rr, rows, ln]
                zero = jnp.zeros_like(q2)
                qs.append(jnp.concatenate([jnp.where(low, q2, zero), jnp.where(low, zero, q2)], axis=0))
                if blk == 0:
                    ks.append(jnp.concatenate([kprev_ref[rr, :, ln], k_ref[rr, rows, ln]], axis=0))
                    vs.append(jnp.concatenate([vprev_ref[rr, :, ln], v_ref[rr, rows, ln]], axis=0))
                else:
                    both = slice((blk - 1) * Lb, (blk + 1) * Lb)
                    ks.append(k_ref[rr, both, ln])
                    vs.append(v_ref[rr, both, ln])
    Q, K, V = jnp.stack(qs), jnp.stack(ks), jnp.stack(vs)
    first = nres * NP

    s = jnp.einsum('gqd,gkd->gqk', Q, K, preferred_element_type=F32)
    qi = lax.broadcasted_iota(jnp.int32, (2 * Lb, 2 * Lb), 0) % Lb
    kj = lax.broadcasted_iota(jnp.int32, (2 * Lb, 2 * Lb), 1)
    window = (kj >= qi) & (kj <= qi + Lb)
    window0 = window & ((kj >= Lb) | (i > 0))
    s = jnp.concatenate([jnp.where(window0[None], s[:first], -jnp.inf),
                         jnp.where(window[None], s[first:], -jnp.inf)], axis=0) if blocks > 1 else \
        jnp.where(window0[None], s, -jnp.inf)
    m = jnp.max(s, axis=-1, keepdims=True)
    e = jnp.exp2(s - m).astype(BF16)
    vx = jnp.concatenate([V, jnp.ones_like(V)], axis=-1)
    nd = jnp.einsum('gqk,gkd->gqd', e, vx, preferred_element_type=F32)
    num, den = nd[:, :, :PW], nd[:, :, PW:]
    o2 = num / den
    l2 = m * math.log(2.0) + jnp.log(den)
    head_of_lane = lane // LSE_LANES
    for blk in range(blocks):
        rows = slice(blk * Lb, (blk + 1) * Lb)
        for rr in range(nres):
            lse_c = jnp.zeros((Lb, PW), F32)
            for p in range(NP):
                g = (blk * nres + rr) * NP + p
                ln = slice(p * PW, (p + 1) * PW)
                o_ref[rr, rows, ln] = jnp.where(low, o2[g, :Lb], o2[g, Lb:]).astype(BF16)
                lse_c = jnp.where(head_of_lane == 2 * p, l2[g, :Lb],
                                  jnp.where(head_of_lane == 2 * p + 1, l2[g, Lb:], lse_c))
            lse_ref[rr, rows, :] = lse_c


def _dilated_attention(qkv):
    _, B, d, Mr, W = qkv.shape
    Lb = DIL_BLOCK
    nb = Mr // Lb
    blocks = min(DIL_BLOCKS_PER_STEP, nb)
    nres = min(DIL_BLOCKS_PER_STEP // blocks, d)
    tile = blocks * Lb

    def cur(which):
        return pl.BlockSpec((None, None, nres, tile, W), lambda b, c, i: (which, b, c, i, 0))

    def prev(which):
        return pl.BlockSpec((None, None, nres, Lb, W),
                            lambda b, c, i: (which, b, c, jnp.maximum(i * blocks - 1, 0), 0))

    ospec = pl.BlockSpec((None, nres, tile, W), lambda b, c, i: (b, c, i, 0))
    lspec = pl.BlockSpec((None, nres, tile, LANES), lambda b, c, i: (b, c, i, 0))
    return pl.pallas_call(
        functools.partial(_dilattn_kernel, blocks=blocks),
        grid=(B, d // nres, nb // blocks),
        in_specs=[cur(0), cur(1), cur(2), prev(1), prev(2)],
        out_specs=[ospec, lspec],
        out_shape=[jax.ShapeDtypeStruct((B, d, Mr, W), BF16), jax.ShapeDtypeStruct((B, d, Mr, LANES), F32)],
        compiler_params=_cparams("parallel", "parallel", "parallel"),
        name=f"dilated_attention_d{d}",
    )(qkv, qkv, qkv, qkv, qkv)


def _ret_kernel(h_ref, gm_ref, w_ref, c_ref, slo_ref, shi_ref, gain_ref, o_ref, st_ref):
    C = RET_CHUNK
    QK = RET_HEADS * RET_QK_DIM

    @pl.when(pl.program_id(1) == 0)
    def _():
        st_ref[...] = jnp.zeros_like(st_ref)

    u = _rms(h_ref[...], gm_ref[...]).astype(BF16)
    p = jnp.dot(u, w_ref[...], preferred_element_type=F32)
    c, s_lo, s_hi = c_ref[...], slo_ref[...], shi_ref[...]
    q = _apply_rope(p[:, :QK], c, s_lo, s_hi, RET_QK_DIM // 2)
    k = _apply_rope(p[:, QK:2 * QK], c, s_lo, s_hi, RET_QK_DIM // 2) * (RET_QK_DIM ** -0.5)
    v = p[:, 2 * QK:2 * QK + RET_WIDTH]
    g = p[:, 2 * QK + RET_WIDTH:]
    gain = gain_ref[...]

    ji = lax.broadcasted_iota(jnp.int32, (C, C), 0)
    jj = lax.broadcasted_iota(jnp.int32, (C, C), 1)
    diff = (ji - jj).astype(F32)
    jcol = lax.broadcasted_iota(jnp.int32, (C, 1), 0).astype(F32)

    for h in range(RET_HEADS):
        lg = math.log(1.0 - 2.0 ** (-5.0 - h))
        qs = slice(h * RET_QK_DIM, (h + 1) * RET_QK_DIM)
        vs = slice(h * RET_V_DIM, (h + 1) * RET_V_DIM)
        decay_in = jnp.where(diff >= 0, jnp.exp(lg * jnp.maximum(diff, 0.0)), 0.0)
        q_dec = jnp.exp(lg * (jcol + 1.0))
        k_dec = jnp.exp(lg * (C - 1.0 - jcol))
        S = st_ref[h]
        for ch in range(p.shape[0] // C):
            rows = slice(ch * C, (ch + 1) * C)
            qh, kh, vh = q[rows, qs], k[rows, qs], v[rows, vs]
            s = lax.dot_general(qh.astype(BF16), kh.astype(BF16), (((1,), (1,)), ((), ())),
                                preferred_element_type=F32) * decay_in
            y = _dot(s, vh) + _dot(qh * q_dec, S)
            kv = lax.dot_general((kh * k_dec).astype(BF16), vh.astype(BF16), (((0,), (0,)), ((), ())),
                                 preferred_element_type=F32)
            S = math.exp(lg * C) * S + kv
            yn = y * lax.rsqrt(jnp.mean(y * y, axis=-1, keepdims=True) + EPS) * gain[:, vs]
            gh = g[rows, vs]
            o_ref[rows, vs] = (gh * jax.nn.sigmoid(gh) * yn).astype(BF16)
        st_ref[h] = S


def _retention(h, g_mix, w_ret, B, T, tabs, gain, tr=512):
    M, D = h.shape
    nt = T // tr
    qk = RET_HEADS * RET_QK_DIM
    const = lambda shape: pl.BlockSpec(shape, lambda b, t: (0,) * len(shape))
    tspec = pl.BlockSpec((tr, qk), lambda b, t: (t, 0))
    return pl.pallas_call(
        _ret_kernel,
        grid=(B, nt),
        in_specs=[pl.BlockSpec((tr, D), lambda b, t: (b * nt + t, 0)), const((1, D)), const(w_ret.shape),
                  tspec, tspec, tspec, const((1, RET_WIDTH))],
        out_specs=pl.BlockSpec((tr, RET_WIDTH), lambda b, t: (b * nt + t, 0)),
        out_shape=jax.ShapeDtypeStruct((M, RET_WIDTH), BF16),
        scratch_shapes=[pltpu.VMEM((RET_HEADS, RET_QK_DIM, RET_V_DIM), F32)],
        compiler_params=_cparams("parallel", "arbitrary"),
        name="retention",
    )(h, g_mix.reshape(1, D), w_ret, *tabs, gain.reshape(1, -1))


def _merge_kernel(h_ref, gm_ref, wg_ref, ya_ref, o0_ref, o1_ref, o2_ref, l0_ref, l1_ref, l2_ref, yc_ref,
                  wa_ref, wb_ref, wc_ref, wo_ref, expand_ref, out_ref, o1_scr, o2_scr, l1_scr, l2_scr):
    h = h_ref[...]
    u = _rms(h, gm_ref[...]).astype(BF16)

    def token_order(ref, scr):
        dil, rows = ref.shape[0], ref.shape[1]
        for c in range(dil):
            blk = ref[c].astype(F32)
            for gl in range(scr.shape[0]):
                scr[gl, pl.ds(c, rows, stride=dil), :] = blk[:, gl * LANES:(gl + 1) * LANES]
        return jnp.concatenate([scr[gl] for gl in range(scr.shape[0])], axis=-1)

    o0, l0 = o0_ref[0].astype(F32), l0_ref[0]
    o1, l1 = token_order(o1_ref, o1_scr), token_order(l1_ref, l1_scr)
    o2, l2 = token_order(o2_ref, o2_scr), token_order(l2_ref, l2_scr)

    mx = jnp.maximum(jnp.maximum(l0, l1), l2)
    e0, e1, e2 = jnp.exp(l0 - mx), jnp.exp(l1 - mx), jnp.exp(l2 - mx)
    inv = 1.0 / (e0 + e1 + e2)

    def spread(w):
        hi, lo = _split2(w)
        return (jnp.dot(hi, expand_ref[...], preferred_element_type=F32)
                + jnp.dot(lo, expand_ref[...], preferred_element_type=F32))

    yb = spread(e0 * inv) * o0 + spread(e1 * inv) * o1 + spread(e2 * inv) * o2

    D = D_MODEL
    gate = lambda i: jax.nn.sigmoid(jnp.dot(u, wg_ref[:, i * D:(i + 1) * D], preferred_element_type=F32))
    merged = (gate(0) * jnp.dot(ya_ref[...], wa_ref[...], preferred_element_type=F32)
              + gate(1) * _dot(yb, wb_ref[...])
              + gate(2) * jnp.dot(yc_ref[...], wc_ref[...], preferred_element_type=F32))
    out_ref[...] = h + _dot(merged, wo_ref[...])


def _merge(h, T, g_mix, w_gate, y_a, o, lse, y_c, wa, wb, wc, wo, tm=512):
    M, D = h.shape
    tps = T // tm
    tok = lambda w: pl.BlockSpec((tm, w), lambda i: (i, 0))
    full = lambda shape: pl.BlockSpec(shape, lambda i: (0, 0))
    W = RW_WIDTH

    def res(group, width):
        d = DIL_PATTERNS[group][1]
        return pl.BlockSpec((None, d, tm // d, width), lambda i: (i // tps, 0, i % tps, 0))

    li = lax.broadcasted_iota(jnp.int32, (LANES, W), 0)
    lj = lax.broadcasted_iota(jnp.int32, (LANES, W), 1)
    expand = jnp.where(li == (lj // HEAD_DIM) * LSE_LANES, 1.0, 0.0).astype(BF16)

    return pl.pallas_call(
        _merge_kernel,
        grid=(M // tm,),
        in_specs=[tok(D), full((1, D)), full((D, 3 * D)), tok(W),
                  res(0, W), res(1, W), res(2, W), res(0, LANES), res(1, LANES), res(2, LANES), tok(W),
                  full((W, D)), full((W, D)), full((W, D)), full((D, D)), full((LANES, W))],
        out_specs=tok(D),
        out_shape=jax.ShapeDtypeStruct((M, D), F32),
        scratch_shapes=[pltpu.VMEM((W // LANES, tm, LANES), F32)] * 2 + [pltpu.VMEM((1, tm, LANES), F32)] * 2,
        compiler_params=_cparams("parallel"),
        name="gated_merge",
    )(h, g_mix.reshape(1, D), w_gate, y_a, o[0], o[1], o[2], lse[0], lse[1], lse[2], y_c, wa, wb, wc, wo, expand)


def _xattn_kernel(h_ref, gx_ref, wq_ref, kv_ref, qn_ref, kn_ref, wo_ref, out_ref, k_scr, v_scr):
    D = D_MODEL

    @pl.when(pl.program_id(1) == 0)
    def _():
        kv = kv_ref[...]
        for hd in range(XA_HEADS):
            sl = slice(hd * XA_HEAD_DIM, (hd + 1) * XA_HEAD_DIM)
            k_scr[:, sl] = _rms(kv[:, sl], kn_ref[...]).astype(BF16)
        v_scr[...] = kv[:, D:].astype(BF16)

    h = h_ref[...]
    hn = _rms(h, gx_ref[...]).astype(BF16)
    q = jnp.dot(hn, wq_ref[...], preferred_element_type=F32)
    outs = []
    for hd in range(XA_HEADS):
        sl = slice(hd * XA_HEAD_DIM, (hd + 1) * XA_HEAD_DIM)
        qh = _rms(q[:, sl], qn_ref[...]).astype(BF16)
        s = lax.dot_general(qh, k_scr[:, sl], (((1,), (1,)), ((), ())), preferred_element_type=F32)
        m = jnp.max(s, axis=-1, keepdims=True)
        e = jnp.exp2(s - m)
        pr = e / jnp.sum(e, axis=-1, keepdims=True)
        outs.append(jnp.dot(pr.astype(BF16), v_scr[:, sl], preferred_element_type=F32))
    o = jnp.concatenate(outs, axis=-1).astype(BF16)
    out_ref[...] = h + jnp.dot(o, wo_ref[...], preferred_element_type=F32)


def _cross_attention(h, B, T, g_x, wq, kv, q_norm, k_norm, wo, tm=512):
    M, D = h.shape
    tps = T // tm
    full = lambda shape: pl.BlockSpec(shape, lambda b, t: (0, 0))
    return pl.pallas_call(
        _xattn_kernel,
        grid=(B, tps),
        in_specs=[
            pl.BlockSpec((tm, D), lambda b, t: (b * tps + t, 0)),
            full((1, D)), full((D, D)),
            pl.BlockSpec((MEM_LEN, 2 * D), lambda b, t: (b, 0)),
            full((1, XA_HEAD_DIM)), full((1, XA_HEAD_DIM)), full((D, D)),
        ],
        out_specs=pl.BlockSpec((tm, D), lambda b, t: (b * tps + t, 0)),
        out_shape=jax.ShapeDtypeStruct((M, D), F32),
        scratch_shapes=[pltpu.VMEM((MEM_LEN, D), BF16), pltpu.VMEM((MEM_LEN, D), BF16)],
        compiler_params=_cparams("parallel", "arbitrary"),
        name="cross_attention",
    )(h, g_x.reshape(1, D), wq, kv, (q_norm * (XA_HEAD_DIM ** -0.5 * math.log2(math.e))).reshape(1, -1),
      k_norm.reshape(1, -1), wo)


def _pad_rows(w, rows_before, total):
    return jnp.pad(w, ((rows_before, total - rows_before - w.shape[0]), (0, 0)))


def _layer(h, mem2, B, T, p, dil_tabs, ret_tabs):
    bf = lambda w: w.astype(BF16)
    ones = _block_ones(RW_WIDTH, RW_HEAD_DIM)
    h = _ffn(h, p['norm_ffn1'], bf(p['ffn1_w13']), bf(p['ffn1_w2']))

    w_in = p['w_in']
    rw_w = jnp.pad(w_in[:, :1824], ((0, 0), (0, RW_PAD_COLS - 1824)))
    mu = jnp.pad(p['rw_mu'], (0, RW_PAD_COLS - 1824))

    w2p = _pad_rows(p['rw_w2'], 0, 128)
    a2p = _pad_rows(p['rw_a2'], 64, 128)
    g2p = _pad_rows(p['rw_g2'], 0, 256)
    r, lw, cum, k, v, a, b, g, bonus = _rwkv_prep(h, p['norm_mix'], bf(rw_w), B, T, mu, p['rw_w0'], w2p, p['rw_a0'], a2p, g2p,
                                             p['rw_k_k'], p['rw_k_a'], p['rw_r_k'], ones)
    y_a = _rwkv_recurrence(r, lw, cum, k, v, a, b, bonus, g, p['rw_ln_w'], p['rw_ln_b'], ones)

    qkvs = _dilated_projection(h, p['norm_mix'], bf(w_in[:, 1824:6432]), B, T, dil_tabs,
                               p['dil_q_norm'], p['dil_k_norm'], ones)
    o, lse = zip(*[_dilated_attention(qkv) for qkv in qkvs])

    y_c = _retention(h, p['norm_mix'], bf(w_in[:, 6432:7968]), B, T, ret_tabs, p['ret_norm'])

    h = _merge(h, T, p['norm_mix'], bf(w_in[:, 7968:]), y_a, o, lse, y_c,
               bf(p['w_branch_rwkv']), bf(p['w_branch_dil']), bf(p['w_branch_ret']), bf(p['w_out']))

    kv = _norm_matmul(mem2, p['norm_mem'], bf(p['xa_wkv']), MEM_LEN, 1024)
    h = _cross_attention(h, B, T, p['norm_xattn'], bf(p['xa_wq']), kv, p['xa_q_norm'], p['xa_k_norm'],
                         bf(p['xa_wo']))
    h = _ffn(h, p['norm_ffn2'], bf(p['ffn2_w13']), bf(p['ffn2_w2']))
    return h


_PARAM_NAMES = ('norm_ffn1', 'ffn1_w13', 'ffn1_w2', 'norm_mix', 'w_in', 'rw_mu', 'rw_w0', 'rw_w2', 'rw_a0',
                'rw_a2', 'rw_g2', 'rw_k_k', 'rw_k_a', 'rw_r_k', 'rw_ln_w', 'rw_ln_b', 'dil_q_norm',
                'dil_k_norm', 'ret_norm', 'w_branch_rwkv', 'w_branch_dil', 'w_branch_ret', 'w_out',
                'norm_xattn', 'norm_mem', 'xa_wq', 'xa_wkv', 'xa_q_norm', 'xa_k_norm', 'xa_wo',
                'norm_ffn2', 'ffn2_w13', 'ffn2_w2')


def kernel(x, mem, norm_ffn1, ffn1_w13, ffn1_w2, norm_mix, w_in, rw_mu, rw_w0, rw_w2, rw_a0, rw_a2, rw_g2, rw_k_k, rw_k_a, rw_r_k, rw_ln_w, rw_ln_b, dil_q_norm, dil_k_norm, ret_norm, w_branch_rwkv, w_branch_dil, w_branch_ret, w_out, norm_xattn, norm_mem, xa_wq, xa_wkv, xa_q_norm, xa_k_norm, xa_wo, norm_ffn2, ffn2_w13, ffn2_w2):
    params = dict(zip(_PARAM_NAMES, (norm_ffn1, ffn1_w13, ffn1_w2, norm_mix, w_in, rw_mu, rw_w0, rw_w2, rw_a0,
                                     rw_a2, rw_g2, rw_k_k, rw_k_a, rw_r_k, rw_ln_w, rw_ln_b, dil_q_norm,
                                     dil_k_norm, ret_norm, w_branch_rwkv, w_branch_dil, w_branch_ret, w_out,
                                     norm_xattn, norm_mem, xa_wq, xa_wkv, xa_q_norm, xa_k_norm, xa_wo,
                                     norm_ffn2, ffn2_w13, ffn2_w2)))
    B, T, D = x.shape
    assert D == D_MODEL and T % 2048 == 0 and mem.shape[1] == MEM_LEN
    depth = norm_ffn1.shape[0]
    dil_tabs = _rope_tables(T, DIL_HEADS, HEAD_DIM, ROPE_DIM, ROPE_THETA)
    ret_tabs = _rope_tables(T, RET_HEADS, RET_QK_DIM, RET_QK_DIM, RET_ROPE_BASE)
    h = x.reshape(B * T, D)
    mem2 = mem.reshape(B * MEM_LEN, D)
    for l in range(depth):
        h = _layer(h, mem2, B, T, {n: params[n][l] for n in _PARAM_NAMES}, dil_tabs, ret_tabs)
    return h.reshape(B, T, D)
```

```python
import functools
import math

import jax
import jax.numpy as jnp
import numpy as np
from jax import lax
from jax.experimental import pallas as pl
from jax.experimental.pallas import tpu as pltpu

F32 = jnp.float32
BF16 = jnp.bfloat16

D_MODEL = 1024
D_FF = 2816
EPS = 1e-6

RW_HEADS = 8
RW_HEAD_DIM = 64
RW_WIDTH = 512
RW_GN_EPS = 64e-5
RW_CHUNK = 64
RW_PASSES_PROD = 1
RW_PASSES_INV = 1
RW_PASSES_APPLY = 1
RW_PASSES_STATE = 1
RW_PAD_COLS = 1920

DIL_PATTERNS = ((128, 1), (512, 4), (2048, 16))
N_DIL = 3
DIL_HEADS = 8
HEAD_DIM = 64
DIL_WIDTH = 512
DIL_BLOCK = 128
DIL_BLOCKS_PER_STEP = 8
LSE_LANES = 16
ROPE_THETA = 500000.0
Q_PRESCALE = HEAD_DIM ** -0.5 * math.log2(math.e)
ROPE_DIM = 16

RET_HEADS = 4
RET_QK_DIM = 64
RET_V_DIM = 128
RET_CHUNK = 128
RET_ROPE_BASE = 10000.0
RET_WIDTH = 512

XA_HEADS = 4
XA_HEAD_DIM = 256
MEM_LEN = 256

VMEM_LIMIT = 56 * 1024 * 1024
LANES = 128


def _cparams(*sem):
    return pltpu.CompilerParams(dimension_semantics=sem, vmem_limit_bytes=VMEM_LIMIT)


def _dot(a, b):
    return jnp.dot(a.astype(BF16), b.astype(BF16), preferred_element_type=F32)


def _split2(x):
    hi = x.astype(BF16)
    lo = (x - hi.astype(F32)).astype(BF16)
    return hi, lo


def _dot3(a, b):
    ah, al = _split2(a)
    bh, bl = _split2(b)
    return (jnp.dot(ah, bh, preferred_element_type=F32)
            + jnp.dot(ah, bl, preferred_element_type=F32)
            + jnp.dot(al, bh, preferred_element_type=F32))


def _rms(x, g):
    return x * lax.rsqrt(jnp.mean(x * x, axis=-1, keepdims=True) + EPS) * g


def _store_lane_groups(ref, x):
    for gl in range(ref.shape[0]):
        ref[gl] = x[:, gl * LANES:(gl + 1) * LANES]


def _block_ones(n, width):
    i = lax.broadcasted_iota(jnp.int32, (n, n), 0) // width
    j = lax.broadcasted_iota(jnp.int32, (n, n), 1) // width
    return jnp.where(i == j, 1.0, 0.0).astype(BF16)


def _ffn_kernel(h_ref, g_ref, w1_ref, w3_ref, w2_ref, o_ref, nrm_ref, acc_ref):
    j = pl.program_id(1)

    @pl.when(j == 0)
    def _():
        nrm_ref[...] = _rms(h_ref[...], g_ref[...]).astype(BF16)
        acc_ref[...] = jnp.zeros_like(acc_ref)

    n = nrm_ref[...]
    a = jnp.dot(n, w1_ref[...], preferred_element_type=F32)
    b = jnp.dot(n, w3_ref[...], preferred_element_type=F32)
    mid = (a * jax.nn.sigmoid(a) * b).astype(BF16)
    acc_ref[...] += jnp.dot(mid, w2_ref[...], preferred_element_type=F32)

    @pl.when(j == pl.num_programs(1) - 1)
    def _():
        o_ref[...] = h_ref[...] + 0.5 * acc_ref[...]


def _ffn(h, g, w13, w2, tm=2048, tf=256):
    M, D = h.shape
    nff = D_FF // tf
    return pl.pallas_call(
        _ffn_kernel,
        grid=(M // tm, nff),
        in_specs=[
            pl.BlockSpec((tm, D), lambda i, j: (i, 0)),
            pl.BlockSpec((1, D), lambda i, j: (0, 0)),
            pl.BlockSpec((D, tf), lambda i, j: (0, j)),
            pl.BlockSpec((D, tf), lambda i, j: (0, j + nff)),
            pl.BlockSpec((tf, D), lambda i, j: (j, 0)),
        ],
        out_specs=pl.BlockSpec((tm, D), lambda i, j: (i, 0)),
        out_shape=jax.ShapeDtypeStruct((M, D), F32),
        scratch_shapes=[pltpu.VMEM((tm, D), BF16), pltpu.VMEM((tm, D), F32)],
        compiler_params=_cparams("parallel", "arbitrary"),
        name="ffn",
    )(h, g.reshape(1, D), w13, w13, w2)


def _nmm_kernel(h_ref, g_ref, w_ref, o_ref, nrm_ref):
    @pl.when(pl.program_id(1) == 0)
    def _():
        nrm_ref[...] = _rms(h_ref[...], g_ref[...]).astype(BF16)

    o_ref[...] = jnp.dot(nrm_ref[...], w_ref[...], preferred_element_type=F32)


def _norm_matmul(h, g, w, tm, tn):
    M, D = h.shape
    N = w.shape[1]
    return pl.pallas_call(
        _nmm_kernel,
        grid=(M // tm, N // tn),
        in_specs=[
            pl.BlockSpec((tm, D), lambda i, j: (i, 0)),
            pl.BlockSpec((1, D), lambda i, j: (0, 0)),
            pl.BlockSpec((D, tn), lambda i, j: (0, j)),
        ],
        out_specs=pl.BlockSpec((tm, tn), lambda i, j: (i, j)),
        out_shape=jax.ShapeDtypeStruct((M, N), F32),
        scratch_shapes=[pltpu.VMEM((tm, D), BF16)],
        compiler_params=_cparams("parallel", "arbitrary"),
        name="norm_matmul",
    )(h, g.reshape(1, D), w)


def _rwprep_kernel(h_ref, gm_ref, w_ref, mu_ref, w0_ref, w2_ref, a0_ref, a2_ref, g2_ref,
                   kk_ref, ka_ref, rk_ref, ones_ref,
                   r_out, lw_out, cum_out, k_out, v_out, a_out, b_out, g_out, bonus_out, last_ref, *, tiles_per_seq):
    i = pl.program_id(0)
    p = jnp.dot(_rms(h_ref[...], gm_ref[...]).astype(BF16), w_ref[...], preferred_element_type=F32)
    tm = p.shape[0]

    @pl.when(i % tiles_per_seq == 0)
    def _():
        last_ref[...] = jnp.zeros_like(last_ref)

    prev_row = last_ref[0:1, :]
    last_ref[0:1, :] = p[tm - 1:tm, :]
    rows = lax.broadcasted_iota(jnp.int32, p.shape, 0)
    shifted = jnp.where(rows == 0, prev_row, pltpu.roll(p, 1, 0))
    pm = p + (shifted - p) * mu_ref[...]

    r = pm[:, 0:512]
    k = pm[:, 512:1024]
    v = pm[:, 1024:1536]
    wa = pm[:, 1536:1664]
    gd = pm[:, 1664:1920]

    w_in = w0_ref[...] + _dot3(jnp.tanh(wa), w2_ref[...])
    lw = -math.exp(-0.5) * jax.nn.sigmoid(w_in)
    a_sig = jax.nn.sigmoid(a0_ref[...] + _dot3(wa, a2_ref[...]))

    C = RW_CHUNK
    ci = lax.broadcasted_iota(jnp.int32, (C, C), 0)
    cj = lax.broadcasted_iota(jnp.int32, (C, C), 1)
    tri = jnp.where(cj <= ci, 1.0, 0.0).astype(BF16)
    l1 = lw.astype(BF16)
    rem = lw - l1.astype(F32)
    l2 = rem.astype(BF16)
    l3 = (rem - l2.astype(F32)).astype(BF16)
    td = lambda x: jnp.dot(tri, x, preferred_element_type=F32)
    cum = jnp.concatenate([td(l1[c0:c0 + C]) + td(l2[c0:c0 + C]) + td(l3[c0:c0 + C])
                           for c0 in range(0, tm, C)], axis=0)
    g = _dot(jax.nn.sigmoid(gd), g2_ref[...])

    ones = ones_ref[...]
    kk = k * kk_ref[...]
    ss = _dot(kk * kk, ones)
    kk = kk * lax.rsqrt(jnp.maximum(ss, 1e-24))
    k2 = k * (1.0 + (a_sig - 1.0) * ka_ref[...])
    bonus = _dot(r * k2 * rk_ref[...], ones) * v

    g_out[...] = g
    bonus_out[...] = bonus
    a_vec = -kk
    b_vec = kk * a_sig
    for h in range(RW_HEADS):
        sl = slice(h * RW_HEAD_DIM, (h + 1) * RW_HEAD_DIM)
        r_out[h] = r[:, sl]
        lw_out[h] = lw[:, sl]
        cum_out[h] = cum[:, sl]
        k_out[h] = k2[:, sl]
        v_out[h] = v[:, sl]
        a_out[h] = a_vec[:, sl]
        b_out[h] = b_vec[:, sl]


def _rwkv_prep(h, g_mix, w_rw, B, T, mu, w0, w2p, a0, a2p, g2p, k_k, k_a, r_k, ones, tm=512):
    M, D = h.shape
    tps = T // tm
    row = lambda x: x.reshape(1, -1)
    full = lambda shape: pl.BlockSpec(shape, lambda i: (0,) * len(shape))
    head_spec = pl.BlockSpec((None, RW_HEADS, tm, RW_HEAD_DIM), lambda i: (i // tps, 0, i % tps, 0))
    head_shape = jax.ShapeDtypeStruct((B, RW_HEADS, T, RW_HEAD_DIM), F32)
    tok_spec = pl.BlockSpec((tm, RW_WIDTH), lambda i: (i, 0))
    tok_shape = jax.ShapeDtypeStruct((M, RW_WIDTH), F32)
    return pl.pallas_call(
        functools.partial(_rwprep_kernel, tiles_per_seq=tps),
        grid=(M // tm,),
        in_specs=[
            pl.BlockSpec((tm, D), lambda i: (i, 0)), full((1, D)), full((D, RW_PAD_COLS)),
            full((1, RW_PAD_COLS)), full((1, RW_WIDTH)), full((128, RW_WIDTH)),
            full((1, RW_WIDTH)), full((128, RW_WIDTH)), full((256, RW_WIDTH)),
            full((1, RW_WIDTH)), full((1, RW_WIDTH)), full((1, RW_WIDTH)), full((RW_WIDTH, RW_WIDTH)),
        ],
        out_specs=[head_spec] * 7 + [tok_spec] * 2,
        out_shape=[head_shape] * 7 + [tok_shape] * 2,
        scratch_shapes=[pltpu.VMEM((8, RW_PAD_COLS), F32)],
        compiler_params=_cparams("arbitrary"),
        name="rwkv_prep",
    )(h, row(g_mix), w_rw, row(mu), row(w0), w2p, row(a0), a2p, g2p, row(k_k), row(k_a), row(r_k), ones)


def _bmm(a, b, passes, nt=False):
    spec = 'gik,gjk->gij' if nt else 'gij,gjk->gik'
    e = lambda x, y: jnp.einsum(spec, x, y, preferred_element_type=F32)
    if passes == 1:
        return e(a.astype(BF16), b.astype(BF16))
    ah, al = _split2(a)
    bh, bl = _split2(b)
    return e(ah, bh) + e(ah, bl) + e(al, bh)


def _rwkv_kernel(r_ref, lw_ref, cum_ref, k_ref, v_ref, a_ref, b_ref, bonus_ref, g_ref, lnw_ref, lnb_ref, ones_ref,
                 y_ref, st_ref, ytok_ref):
    C = RW_CHUNK
    H = RW_HEADS
    Tc = r_ref.shape[1]
    nc = Tc // C
    G = H * nc

    @pl.when(pl.program_id(1) == 0)
    def _():
        st_ref[...] = jnp.zeros_like(st_ref)

    shp = lambda ref: ref[...].reshape(G, C, RW_HEAD_DIM)
    r, lw, L, k, v, a, b = (shp(x) for x in (r_ref, lw_ref, cum_ref, k_ref, v_ref, a_ref, b_ref))

    ti = lax.broadcasted_iota(jnp.int32, (C, C), 0)
    tj = lax.broadcasted_iota(jnp.int32, (C, C), 1)
    eye = jnp.where(ti == tj, 1.0, 0.0).astype(F32)
    wi = lax.broadcasted_iota(jnp.int32, (C, 2 * C), 0)
    wj = lax.broadcasted_iota(jnp.int32, (C, 2 * C), 1)
    second = wj >= C
    wjc = jnp.where(second, wj - C, wj)

    Lprev = L - lw
    Ltot = L[:, C - 1:C, :]

    eL = jnp.exp(L)
    enL = jnp.exp(-L)
    eh = jnp.exp(Ltot - L)
    At = a * jnp.exp(Lprev)
    Rt = r * eL
    BKt = jnp.concatenate([b * enL, k * enL], axis=1)
    BKh = jnp.concatenate([b * eh, k * eh], axis=1)

    P4 = _bmm(jnp.concatenate([At, Rt], axis=1), BKt, RW_PASSES_PROD, nt=True)
    top, bot = P4[:, :C, :], P4[:, C:, :]
    N = jnp.where((tj < ti)[None], top[:, :, :C], 0.0)
    AKz = jnp.where((second & (wjc < wi))[None], top, 0.0)
    RBK = jnp.where((wjc <= wi)[None], bot, 0.0)
    RB = RBK[:, :, :C]

    def siblings(s):
        return ((ti // (2 * s)) == (tj // (2 * s))) & ((ti // s) != (tj // s))

    X = eye[None] + jnp.where(siblings(1)[None], N, 0.0)
    s_blk = 2
    while s_blk < C:
        XE = _bmm(X, jnp.where(siblings(s_blk)[None], N, 0.0), RW_PASSES_INV)
        X = X + _bmm(XE, X, RW_PASSES_INV)
        s_blk *= 2

    zv = jnp.concatenate([jnp.zeros_like(v), v], axis=1)
    Abar = _bmm(X, At, RW_PASSES_APPLY)
    W0 = _bmm(X, _bmm(AKz, zv, RW_PASSES_APPLY), RW_PASSES_APPLY)
    wv = jnp.concatenate([W0, v], axis=1)
    Y0 = _bmm(RBK, wv, RW_PASSES_APPLY)
    Rbar = Rt + _bmm(RB, Abar, RW_PASSES_APPLY)
    BKhT = jnp.swapaxes(BKh, 1, 2)
    Mtx = _bmm(BKhT[:, :, :C], Abar, RW_PASSES_APPLY) + eye[None] * jnp.exp(Ltot)
    G0 = _bmm(BKhT, wv, RW_PASSES_APPLY)

    hsplit = lambda x: x.reshape(H, nc, x.shape[1], x.shape[2])
    RM = hsplit(jnp.concatenate([Rbar, Mtx], axis=1))
    Y0, G0 = hsplit(Y0), hsplit(G0)
    St = st_ref[...]
    ys = []
    for c in range(nc):
        both = _bmm(RM[:, c], St, RW_PASSES_STATE)
        ys.append(both[:, :C] + Y0[:, c])
        St = both[:, C:] + G0[:, c]
    st_ref[...] = St
    y = jnp.concatenate(ys, axis=1)
    for h in range(H):
        ytok_ref[:, h * RW_HEAD_DIM:(h + 1) * RW_HEAD_DIM] = y[h]

    y = ytok_ref[...]
    ones = ones_ref[...]
    mean = _dot(y, ones) * (1.0 / RW_HEAD_DIM)
    yc = y - mean
    var = _dot(yc * yc, ones) * (1.0 / RW_HEAD_DIM)
    ya = (yc * lax.rsqrt(var + RW_GN_EPS) * lnw_ref[...] + lnb_ref[...] + bonus_ref[...]) * g_ref[...]
    y_ref[...] = ya.astype(BF16)


def _rwkv_recurrence(r, lw, cum, k, v, a, b, bonus, g, ln_w, ln_b, ones, tc=256):
    B, H, T, Dh = r.shape
    W = H * Dh
    nt = T // tc
    spec = pl.BlockSpec((None, H, tc, Dh), lambda bi, c: (bi, 0, c, 0))
    tok = pl.BlockSpec((tc, W), lambda bi, c: (bi * nt + c, 0))
    const = lambda shape: pl.BlockSpec(shape, lambda bi, c: (0, 0))
    return pl.pallas_call(
        _rwkv_kernel,
        grid=(B, nt),
        in_specs=[spec] * 7 + [tok, tok, const((1, W)), const((1, W)), const((W, W))],
        out_specs=tok,
        out_shape=jax.ShapeDtypeStruct((B * T, W), BF16),
        scratch_shapes=[pltpu.VMEM((H, Dh, Dh), F32), pltpu.VMEM((tc, W), F32)],
        compiler_params=_cparams("parallel", "arbitrary"),
        name="rwkv_recurrence",
    )(r, lw, cum, k, v, a, b, bonus, g, ln_w.reshape(1, W), ln_b.reshape(1, W), ones)


def _rope_tables(T, heads, head_dim, rot_dim, base):
    half = rot_dim // 2
    inv_freq = base ** (-jnp.arange(half, dtype=F32) / half)
    ang = jnp.arange(T).astype(F32)[:, None] * inv_freq[None, :]
    cos, sin = jnp.cos(ang), jnp.sin(ang)
    rest = head_dim - rot_dim
    c = jnp.concatenate([cos, cos, jnp.ones((T, rest), F32)], axis=1)
    s_lo = jnp.concatenate([-sin, jnp.zeros((T, half + rest), F32)], axis=1)
    s_hi = jnp.concatenate([jnp.zeros((T, half), F32), sin, jnp.zeros((T, rest), F32)], axis=1)
    tile = lambda x: jnp.tile(x, (1, heads))
    return tile(c), tile(s_lo), tile(s_hi)


def _apply_rope(x, c, s_lo, s_hi, half):
    n = x.shape[-1]
    return x * c + pltpu.roll(x, n - half, 1) * s_lo + pltpu.roll(x, half, 1) * s_hi


def _dilproj_kernel(h_ref, g_ref, w_ref, c_ref, slo_ref, shi_ref, gq_ref, gk_ref, ones_ref,
                    o0_ref, o1_ref, o2_ref, tmp_ref):
    u = _rms(h_ref[...], g_ref[...]).astype(BF16)
    c, s_lo, s_hi = c_ref[...], slo_ref[...], shi_ref[...]
    W = DIL_WIDTH
    for grp, o_ref in enumerate((o0_ref, o1_ref, o2_ref)):
        dilation = DIL_PATTERNS[grp][1]
        rows = u.shape[0] // dilation
        for which in range(3):
            col = (3 * grp + which) * W
            val = jnp.dot(u, w_ref[:, col:col + W], preferred_element_type=F32)
            if which < 2:
                gain = (gq_ref if which == 0 else gk_ref)[grp:grp + 1, :]
                ms = jnp.dot((val * val).astype(BF16), ones_ref[...], preferred_element_type=F32) * (1.0 / HEAD_DIM)
                val = _apply_rope(val * lax.rsqrt(ms + EPS) * gain, c, s_lo, s_hi, ROPE_DIM // 2)
            if dilation == 1:
                o_ref[which, 0] = val.astype(BF16)
            else:
                _store_lane_groups(tmp_ref, val)
                for res in range(dilation):
                    for gl in range(tmp_ref.shape[0]):
                        piece = tmp_ref[gl, pl.ds(res, rows, stride=dilation), :]
                        o_ref[which, res, :, gl * LANES:(gl + 1) * LANES] = piece.astype(BF16)


def _dilated_projection(h, g_mix, w_dil, B, T, tabs, gq, gk, ones, tm=512):
    M, D = h.shape
    tps = T // tm
    W = DIL_WIDTH
    const = lambda shape: pl.BlockSpec(shape, lambda i: (0,) * len(shape))
    tspec = pl.BlockSpec((tm, W), lambda i: (i % tps, 0))
    out_specs, out_shape = [], []
    for _, d in DIL_PATTERNS:
        out_specs.append(pl.BlockSpec((3, None, d, tm // d, W), lambda i: (0, i // tps, 0, i % tps, 0)))
        out_shape.append(jax.ShapeDtypeStruct((3, B, d, T // d, W), BF16))
    return pl.pallas_call(
        _dilproj_kernel,
        grid=(M // tm,),
        in_specs=[pl.BlockSpec((tm, D), lambda i: (i, 0)), const((1, D)), const((D, 3 * N_DIL * W)),
                  tspec, tspec, tspec, const((N_DIL, W)), const((N_DIL, W)), const((W, W))],
        out_specs=out_specs,
        out_shape=out_shape,
        scratch_shapes=[pltpu.VMEM((W // LANES, tm, LANES), F32)],
        compiler_params=_cparams("parallel"),
        name="dilated_projection",
    )(h, g_mix.reshape(1, D), w_dil, *tabs, jnp.tile(gq * Q_PRESCALE, (1, DIL_HEADS)), jnp.tile(gk, (1, DIL_HEADS)), ones)


def _dilattn_kernel(q_ref, k_ref, v_ref, kprev_ref, vprev_ref, o_ref, lse_ref, *, blocks):
    i = pl.program_id(2)
    Lb = DIL_BLOCK
    PW = 2 * HEAD_DIM
    NP = DIL_WIDTH // PW
    nres = q_ref.shape[0]
    lane = lax.broadcasted_iota(jnp.int32, (Lb, PW), 1)
    low = lane < HEAD_DIM

    qs, ks, vs = [], [], []
    for blk in range(blocks):
        rows = slice(blk * Lb, (blk + 1) * Lb)
        for rr in range(nres):
            for p in range(NP):
                ln = slice(p * PW, (p + 1) * PW)
                q2 = q_ref[rr, rows, ln]
                zero = jnp.zeros_like(q2)
                qs.append(jnp.concatenate([jnp.where(low, q2, zero), jnp.where(low, zero, q2)], axis=0))
                if blk == 0:
                    ks.append(jnp.concatenate([kprev_ref[rr, :, ln], k_ref[rr, rows, ln]], axis=0))
                    vs.append(jnp.concatenate([vprev_ref[rr, :, ln], v_ref[rr, rows, ln]], axis=0))
                else:
                    both = slice((blk - 1) * Lb, (blk + 1) * Lb)
                    ks.append(k_ref[rr, both, ln])
                    vs.append(v_ref[rr, both, ln])
    Q, K, V = jnp.stack(qs), jnp.stack(ks), jnp.stack(vs)
    first = nres * NP

    s = jnp.einsum('gqd,gkd->gqk', Q, K, preferred_element_type=F32)
    qi = lax.broadcasted_iota(jnp.int32, (2 * Lb, 2 * Lb), 0) % Lb
    kj = lax.broadcasted_iota(jnp.int32, (2 * Lb, 2 * Lb), 1)
    window = (kj >= qi) & (kj <= qi + Lb)
    window0 = window & ((kj >= Lb) | (i > 0))
    s = jnp.concatenate([jnp.where(window0[None], s[:first], -jnp.inf),
                         jnp.where(window[None], s[first:], -jnp.inf)], axis=0) if blocks > 1 else \
        jnp.where(window0[None], s, -jnp.inf)
    m = jnp.max(s, axis=-1, keepdims=True)
    e = jnp.exp2(s - m).astype(BF16)
    vx = jnp.concatenate([V, jnp.ones_like(V)], axis=-1)
    nd = jnp.einsum('gqk,gkd->gqd', e, vx, preferred_element_type=F32)
    num, den = nd[:, :, :PW], nd[:, :, PW:]
    o2 = num / den
    l2 = m * math.log(2.0) + jnp.log(den)
    head_of_lane = lane // LSE_LANES
    for blk in range(blocks):
        rows = slice(blk * Lb, (blk + 1) * Lb)
        for rr in range(nres):
            lse_c = jnp.zeros((Lb, PW), F32)
            for p in range(NP):
                g = (blk * nres + rr) * NP + p
                ln = slice(p * PW, (p + 1) * PW)
                o_ref[rr, rows, ln] = jnp.where(low, o2[g, :Lb], o2[g, Lb:]).astype(BF16)
                lse_c = jnp.where(head_of_lane == 2 * p, l2[g, :Lb],
                                  jnp.where(head_of_lane == 2 * p + 1, l2[g, Lb:], lse_c))
            lse_ref[rr, rows, :] = lse_c


def _dilated_attention(qkv):
    _, B, d, Mr, W = qkv.shape
    Lb = DIL_BLOCK
    nb = Mr // Lb
    blocks = min(DIL_BLOCKS_PER_STEP, nb)
    nres = min(DIL_BLOCKS_PER_STEP // blocks, d)
    tile = blocks * Lb

    def cur(which):
        return pl.BlockSpec((None, None, nres, tile, W), lambda b, c, i: (which, b, c, i, 0))

    def prev(which):
        return pl.BlockSpec((None, None, nres, Lb, W),
                            lambda b, c, i: (which, b, c, jnp.maximum(i * blocks - 1, 0), 0))

    ospec = pl.BlockSpec((None, nres, tile, W), lambda b, c, i: (b, c, i, 0))
    lspec = pl.BlockSpec((None, nres, tile, LANES), lambda b, c, i: (b, c, i, 0))
    return pl.pallas_call(
        functools.partial(_dilattn_kernel, blocks=blocks),
        grid=(B, d // nres, nb // blocks),
        in_specs=[cur(0), cur(1), cur(2), prev(1), prev(2)],
        out_specs=[ospec, lspec],
        out_shape=[jax.ShapeDtypeStruct((B, d, Mr, W), BF16), jax.ShapeDtypeStruct((B, d, Mr, LANES), F32)],
        compiler_params=_cparams("parallel", "parallel", "parallel"),
        name=f"dilated_attention_d{d}",
    )(qkv, qkv, qkv, qkv, qkv)


def _ret_kernel(h_ref, gm_ref, w_ref, c_ref, slo_ref, shi_ref, gain_ref, o_ref, st_ref):
    C = RET_CHUNK
    QK = RET_HEADS * RET_QK_DIM

    @pl.when(pl.program_id(1) == 0)
    def _():
        st_ref[...] = jnp.zeros_like(st_ref)

    u = _rms(h_ref[...], gm_ref[...]).astype(BF16)
    p = jnp.dot(u, w_ref[...], preferred_element_type=F32)
    c, s_lo, s_hi = c_ref[...], slo_ref[...], shi_ref[...]
    q = _apply_rope(p[:, :QK], c, s_lo, s_hi, RET_QK_DIM // 2)
    k = _apply_rope(p[:, QK:2 * QK], c, s_lo, s_hi, RET_QK_DIM // 2) * (RET_QK_DIM ** -0.5)
    v = p[:, 2 * QK:2 * QK + RET_WIDTH]
    g = p[:, 2 * QK + RET_WIDTH:]
    gain = gain_ref[...]

    ji = lax.broadcasted_iota(jnp.int32, (C, C), 0)
    jj = lax.broadcasted_iota(jnp.int32, (C, C), 1)
    diff = (ji - jj).astype(F32)
    jcol = lax.broadcasted_iota(jnp.int32, (C, 1), 0).astype(F32)

    for h in range(RET_HEADS):
        lg = math.log(1.0 - 2.0 ** (-5.0 - h))
        qs = slice(h * RET_QK_DIM, (h + 1) * RET_QK_DIM)
        vs = slice(h * RET_V_DIM, (h + 1) * RET_V_DIM)
        decay_in = jnp.where(diff >= 0, jnp.exp(lg * jnp.maximum(diff, 0.0)), 0.0)
        q_dec = jnp.exp(lg * (jcol + 1.0))
        k_dec = jnp.exp(lg * (C - 1.0 - jcol))
        S = st_ref[h]
        for ch in range(p.shape[0] // C):
            rows = slice(ch * C, (ch + 1) * C)
            qh, kh, vh = q[rows, qs], k[rows, qs], v[rows, vs]
            s = lax.dot_general(qh.astype(BF16), kh.astype(BF16), (((1,), (1,)), ((), ())),
                                preferred_element_type=F32) * decay_in
            y = _dot(s, vh) + _dot(qh * q_dec, S)
            kv = lax.dot_general((kh * k_dec).astype(BF16), vh.astype(BF16), (((0,), (0,)), ((), ())),
                                 preferred_element_type=F32)
            S = math.exp(lg * C) * S + kv
            yn = y * lax.rsqrt(jnp.mean(y * y, axis=-1, keepdims=True) + EPS) * gain[:, vs]
            gh = g[rows, vs]
            o_ref[rows, vs] = (gh * jax.nn.sigmoid(gh) * yn).astype(BF16)
        st_ref[h] = S


def _retention(h, g_mix, w_ret, B, T, tabs, gain, tr=512):
    M, D = h.shape
    nt = T // tr
    qk = RET_HEADS * RET_QK_DIM
    const = lambda shape: pl.BlockSpec(shape, lambda b, t: (0,) * len(shape))
    tspec = pl.BlockSpec((tr, qk), lambda b, t: (t, 0))
    return pl.pallas_call(
        _ret_kernel,
        grid=(B, nt),
        in_specs=[pl.BlockSpec((tr, D), lambda b, t: (b * nt + t, 0)), const((1, D)), const(w_ret.shape),
                  tspec, tspec, tspec, const((1, RET_WIDTH))],
        out_specs=pl.BlockSpec((tr, RET_WIDTH), lambda b, t: (b * nt + t, 0)),
        out_shape=jax.ShapeDtypeStruct((M, RET_WIDTH), BF16),
        scratch_shapes=[pltpu.VMEM((RET_HEADS, RET_QK_DIM, RET_V_DIM), F32)],
        compiler_params=_cparams("parallel", "arbitrary"),
        name="retention",
    )(h, g_mix.reshape(1, D), w_ret, *tabs, gain.reshape(1, -1))


def _merge_kernel(h_ref, gm_ref, wg_ref, ya_ref, o0_ref, o1_ref, o2_ref, l0_ref, l1_ref, l2_ref, yc_ref,
                  wa_ref, wb_ref, wc_ref, wo_ref, expand_ref, out_ref, o1_scr, o2_scr, l1_scr, l2_scr):
    h = h_ref[...]
    u = _rms(h, gm_ref[...]).astype(BF16)

    def token_order(ref, scr):
        dil, rows = ref.shape[0], ref.shape[1]
        for c in range(dil):
            blk = ref[c].astype(F32)
            for gl in range(scr.shape[0]):
                scr[gl, pl.ds(c, rows, stride=dil), :] = blk[:, gl * LANES:(gl + 1) * LANES]
        return jnp.concatenate([scr[gl] for gl in range(scr.shape[0])], axis=-1)

    o0, l0 = o0_ref[0].astype(F32), l0_ref[0]
    o1, l1 = token_order(o1_ref, o1_scr), token_order(l1_ref, l1_scr)
    o2, l2 = token_order(o2_ref, o2_scr), token_order(l2_ref, l2_scr)

    mx = jnp.maximum(jnp.maximum(l0, l1), l2)
    e0, e1, e2 = jnp.exp(l0 - mx), jnp.exp(l1 - mx), jnp.exp(l2 - mx)
    inv = 1.0 / (e0 + e1 + e2)

    def spread(w):
        hi, lo = _split2(w)
        return (jnp.dot(hi, expand_ref[...], preferred_element_type=F32)
                + jnp.dot(lo, expand_ref[...], preferred_element_type=F32))

    yb = spread(e0 * inv) * o0 + spread(e1 * inv) * o1 + spread(e2 * inv) * o2

    D = D_MODEL
    gate = lambda i: jax.nn.sigmoid(jnp.dot(u, wg_ref[:, i * D:(i + 1) * D], preferred_element_type=F32))
    merged = (gate(0) * jnp.dot(ya_ref[...], wa_ref[...], preferred_element_type=F32)
              + gate(1) * _dot(yb, wb_ref[...])
              + gate(2) * jnp.dot(yc_ref[...], wc_ref[...], preferred_element_type=F32))
    out_ref[...] = h + _dot(merged, wo_ref[...])


def _merge(h, T, g_mix, w_gate, y_a, o, lse, y_c, wa, wb, wc, wo, tm=512):
    M, D = h.shape
    tps = T // tm
    tok = lambda w: pl.BlockSpec((tm, w), lambda i: (i, 0))
    full = lambda shape: pl.BlockSpec(shape, lambda i: (0, 0))
    W = RW_WIDTH

    def res(group, width):
        d = DIL_PATTERNS[group][1]
        return pl.BlockSpec((None, d, tm // d, width), lambda i: (i // tps, 0, i % tps, 0))

    li = lax.broadcasted_iota(jnp.int32, (LANES, W), 0)
    lj = lax.broadcasted_iota(jnp.int32, (LANES, W), 1)
    expand = jnp.where(li == (lj // HEAD_DIM) * LSE_LANES, 1.0, 0.0).astype(BF16)

    return pl.pallas_call(
        _merge_kernel,
        grid=(M // tm,),
        in_specs=[tok(D), full((1, D)), full((D, 3 * D)), tok(W),
                  res(0, W), res(1, W), res(2, W), res(0, LANES), res(1, LANES), res(2, LANES), tok(W),
                  full((W, D)), full((W, D)), full((W, D)), full((D, D)), full((LANES, W))],
        out_specs=tok(D),
        out_shape=jax.ShapeDtypeStruct((M, D), F32),
        scratch_shapes=[pltpu.VMEM((W // LANES, tm, LANES), F32)] * 2 + [pltpu.VMEM((1, tm, LANES), F32)] * 2,
        compiler_params=_cparams("parallel"),
        name="gated_merge",
    )(h, g_mix.reshape(1, D), w_gate, y_a, o[0], o[1], o[2], lse[0], lse[1], lse[2], y_c, wa, wb, wc, wo, expand)


def _xattn_kernel(h_ref, gx_ref, wq_ref, kv_ref, qn_ref, kn_ref, wo_ref, out_ref, k_scr, v_scr):
    D = D_MODEL

    @pl.when(pl.program_id(1) == 0)
    def _():
        kv = kv_ref[...]
        for hd in range(XA_HEADS):
            sl = slice(hd * XA_HEAD_DIM, (hd + 1) * XA_HEAD_DIM)
            k_scr[:, sl] = _rms(kv[:, sl], kn_ref[...]).astype(BF16)
        v_scr[...] = kv[:, D:].astype(BF16)

    h = h_ref[...]
    hn = _rms(h, gx_ref[...]).astype(BF16)
    q = jnp.dot(hn, wq_ref[...], preferred_element_type=F32)
    outs = []
    for hd in range(XA_HEADS):
        sl = slice(hd * XA_HEAD_DIM, (hd + 1) * XA_HEAD_DIM)
        qh = _rms(q[:, sl], qn_ref[...]).astype(BF16)
        s = lax.dot_general(qh, k_scr[:, sl], (((1,), (1,)), ((), ())), preferred_element_type=F32)
        m = jnp.max(s, axis=-1, keepdims=True)
        e = jnp.exp2(s - m)
        pr = e / jnp.sum(e, axis=-1, keepdims=True)
        outs.append(jnp.dot(pr.astype(BF16), v_scr[:, sl], preferred_element_type=F32))
    o = jnp.concatenate(outs, axis=-1).astype(BF16)
    out_ref[...] = h + jnp.dot(o, wo_ref[...], preferred_element_type=F32)


def _cross_attention(h, B, T, g_x, wq, kv, q_norm, k_norm, wo, tm=512):
    M, D = h.shape
    tps = T // tm
    full = lambda shape: pl.BlockSpec(shape, lambda b, t: (0, 0))
    return pl.pallas_call(
        _xattn_kernel,
        grid=(B, tps),
        in_specs=[
            pl.BlockSpec((tm, D), lambda b, t: (b * tps + t, 0)),
            full((1, D)), full((D, D)),
            pl.BlockSpec((MEM_LEN, 2 * D), lambda b, t: (b, 0)),
            full((1, XA_HEAD_DIM)), full((1, XA_HEAD_DIM)), full((D, D)),
        ],
        out_specs=pl.BlockSpec((tm, D), lambda b, t: (b * tps + t, 0)),
        out_shape=jax.ShapeDtypeStruct((M, D), F32),
        scratch_shapes=[pltpu.VMEM((MEM_LEN, D), BF16), pltpu.VMEM((MEM_LEN, D), BF16)],
        compiler_params=_cparams("parallel", "arbitrary"),
        name="cross_attention",
    )(h, g_x.reshape(1, D), wq, kv, (q_norm * (XA_HEAD_DIM ** -0.5 * math.log2(math.e))).reshape(1, -1),
      k_norm.reshape(1, -1), wo)


def _pad_rows(w, rows_before, total):
    return jnp.pad(w, ((rows_before, total - rows_before - w.shape[0]), (0, 0)))


def _layer(h, mem2, B, T, p, dil_tabs, ret_tabs):
    bf = lambda w: w.astype(BF16)
    ones = _block_ones(RW_WIDTH, RW_HEAD_DIM)
    h = _ffn(h, p['norm_ffn1'], bf(p['ffn1_w13']), bf(p['ffn1_w2']))

    w_in = p['w_in']
    rw_w = jnp.pad(w_in[:, :1824], ((0, 0), (0, RW_PAD_COLS - 1824)))
    mu = jnp.pad(p['rw_mu'], (0, RW_PAD_COLS - 1824))

    w2p = _pad_rows(p['rw_w2'], 0, 128)
    a2p = _pad_rows(p['rw_a2'], 64, 128)
    g2p = _pad_rows(p['rw_g2'], 0, 256)
    r, lw, cum, k, v, a, b, g, bonus = _rwkv_prep(h, p['norm_mix'], bf(rw_w), B, T, mu, p['rw_w0'], w2p, p['rw_a0'], a2p, g2p,
                                             p['rw_k_k'], p['rw_k_a'], p['rw_r_k'], ones)
    y_a = _rwkv_recurrence(r, lw, cum, k, v, a, b, bonus, g, p['rw_ln_w'], p['rw_ln_b'], ones)

    qkvs = _dilated_projection(h, p['norm_mix'], bf(w_in[:, 1824:6432]), B, T, dil_tabs,
                               p['dil_q_norm'], p['dil_k_norm'], ones)
    o, lse = zip(*[_dilated_attention(qkv) for qkv in qkvs])

    y_c = _retention(h, p['norm_mix'], bf(w_in[:, 6432:7968]), B, T, ret_tabs, p['ret_norm'])

    h = _merge(h, T, p['norm_mix'], bf(w_in[:, 7968:]), y_a, o, lse, y_c,
               bf(p['w_branch_rwkv']), bf(p['w_branch_dil']), bf(p['w_branch_ret']), bf(p['w_out']))

    kv = _norm_matmul(mem2, p['norm_mem'], bf(p['xa_wkv']), MEM_LEN, 1024)
    h = _cross_attention(h, B, T, p['norm_xattn'], bf(p['xa_wq']), kv, p['xa_q_norm'], p['xa_k_norm'],
                         bf(p['xa_wo']))
    h = _ffn(h, p['norm_ffn2'], bf(p['ffn2_w13']), bf(p['ffn2_w2']))
    return h


_PARAM_NAMES = ('norm_ffn1', 'ffn1_w13', 'ffn1_w2', 'norm_mix', 'w_in', 'rw_mu', 'rw_w0', 'rw_w2', 'rw_a0',
                'rw_a2', 'rw_g2', 'rw_k_k', 'rw_k_a', 'rw_r_k', 'rw_ln_w', 'rw_ln_b', 'dil_q_norm',
                'dil_k_norm', 'ret_norm', 'w_branch_rwkv', 'w_branch_dil', 'w_branch_ret', 'w_out',
                'norm_xattn', 'norm_mem', 'xa_wq', 'xa_wkv', 'xa_q_norm', 'xa_k_norm', 'xa_wo',
                'norm_ffn2', 'ffn2_w13', 'ffn2_w2')


def kernel(x, mem, norm_ffn1, ffn1_w13, ffn1_w2, norm_mix, w_in, rw_mu, rw_w0, rw_w2, rw_a0, rw_a2, rw_g2, rw_k_k, rw_k_a, rw_r_k, rw_ln_w, rw_ln_b, dil_q_norm, dil_k_norm, ret_norm, w_branch_rwkv, w_branch_dil, w_branch_ret, w_out, norm_xattn, norm_mem, xa_wq, xa_wkv, xa_q_norm, xa_k_norm, xa_wo, norm_ffn2, ffn2_w13, ffn2_w2):
    params = dict(zip(_PARAM_NAMES, (norm_ffn1, ffn1_w13, ffn1_w2, norm_mix, w_in, rw_mu, rw_w0, rw_w2, rw_a0,
                                     rw_a2, rw_g2, rw_k_k, rw_k_a, rw_r_k, rw_ln_w, rw_ln_b, dil_q_norm,
                                     dil_k_norm, ret_norm, w_branch_rwkv, w_branch_dil, w_branch_ret, w_out,
                                     norm_xattn, norm_mem, xa_wq, xa_wkv, xa_q_norm, xa_k_norm, xa_wo,
                                     norm_ffn2, ffn2_w13, ffn2_w2)))
    B, T, D = x.shape
    assert D == D_MODEL and T % 2048 == 0 and mem.shape[1] == MEM_LEN
    depth = norm_ffn1.shape[0]
    dil_tabs = _rope_tables(T, DIL_HEADS, HEAD_DIM, ROPE_DIM, ROPE_THETA)
    ret_tabs = _rope_tables(T, RET_HEADS, RET_QK_DIM, RET_QK_DIM, RET_ROPE_BASE)
    h = x.reshape(B * T, D)
    mem2 = mem.reshape(B * MEM_LEN, D)
    for l in range(depth):
        h = _layer(h, mem2, B, T, {n: params[n][l] for n in _PARAM_NAMES}, dil_tabs, ret_tabs)
    return h.reshape(B, T, D)
```

```python
import functools
import math

import jax
import jax.numpy as jnp
import numpy as np
from jax import lax
from jax.experimental import pallas as pl
from jax.experimental.pallas import tpu as pltpu

F32 = jnp.float32
BF16 = jnp.bfloat16

D_MODEL = 1024
D_FF = 2816
EPS = 1e-6

RW_HEADS = 8
RW_HEAD_DIM = 64
RW_WIDTH = 512
RW_GN_EPS = 64e-5
RW_CHUNK = 64
RW_PASSES_PROD = 1
RW_PASSES_INV = 1
RW_PASSES_APPLY = 1
RW_PASSES_STATE = 1
RW_PAD_COLS = 1920

DIL_PATTERNS = ((128, 1), (512, 4), (2048, 16))
N_DIL = 3
DIL_HEADS = 8
HEAD_DIM = 64
DIL_WIDTH = 512
DIL_BLOCK = 128
DIL_BLOCKS_PER_STEP = 8
LSE_LANES = 16
ROPE_THETA = 500000.0
Q_PRESCALE = HEAD_DIM ** -0.5 * math.log2(math.e)
ROPE_DIM = 16

RET_HEADS = 4
RET_QK_DIM = 64
RET_V_DIM = 128
RET_CHUNK = 128
RET_ROPE_BASE = 10000.0
RET_WIDTH = 512

XA_HEADS = 4
XA_HEAD_DIM = 256
MEM_LEN = 256

VMEM_LIMIT = 56 * 1024 * 1024
LANES = 128


def _cparams(*sem):
    return pltpu.CompilerParams(dimension_semantics=sem, vmem_limit_bytes=VMEM_LIMIT)


def _dot(a, b):
    return jnp.dot(a.astype(BF16), b.astype(BF16), preferred_element_type=F32)


def _split2(x):
    hi = x.astype(BF16)
    lo = (x - hi.astype(F32)).astype(BF16)
    return hi, lo


def _dot3(a, b):
    ah, al = _split2(a)
    bh, bl = _split2(b)
    return (jnp.dot(ah, bh, preferred_element_type=F32)
            + jnp.dot(ah, bl, preferred_element_type=F32)
            + jnp.dot(al, bh, preferred_element_type=F32))


def _rms(x, g):
    return x * lax.rsqrt(jnp.mean(x * x, axis=-1, keepdims=True) + EPS) * g


def _store_lane_groups(ref, x):
    for gl in range(ref.shape[0]):
        ref[gl] = x[:, gl * LANES:(gl + 1) * LANES]


def _block_ones(n, width):
    i = lax.broadcasted_iota(jnp.int32, (n, n), 0) // width
    j = lax.broadcasted_iota(jnp.int32, (n, n), 1) // width
    return jnp.where(i == j, 1.0, 0.0).astype(BF16)


def _ffn_kernel(h_ref, g_ref, w1_ref, w3_ref, w2_ref, o_ref, nrm_ref, acc_ref):
    j = pl.program_id(1)

    @pl.when(j == 0)
    def _():
        nrm_ref[...] = _rms(h_ref[...], g_ref[...]).astype(BF16)
        acc_ref[...] = jnp.zeros_like(acc_ref)

    n = nrm_ref[...]
    a = jnp.dot(n, w1_ref[...], preferred_element_type=F32)
    b = jnp.dot(n, w3_ref[...], preferred_element_type=F32)
    mid = (a * jax.nn.sigmoid(a) * b).astype(BF16)
    acc_ref[...] += jnp.dot(mid, w2_ref[...], preferred_element_type=F32)

    @pl.when(j == pl.num_programs(1) - 1)
    def _():
        o_ref[...] = h_ref[...] + 0.5 * acc_ref[...]


def _ffn(h, g, w13, w2, tm=1024, tf=1408):
    M, D = h.shape
    nff = D_FF // tf
    return pl.pallas_call(
        _ffn_kernel,
        grid=(M // tm, nff),
        in_specs=[
            pl.BlockSpec((tm, D), lambda i, j: (i, 0)),
            pl.BlockSpec((1, D), lambda i, j: (0, 0)),
            pl.BlockSpec((D, tf), lambda i, j: (0, j)),
            pl.BlockSpec((D, tf), lambda i, j: (0, j + nff)),
            pl.BlockSpec((tf, D), lambda i, j: (j, 0)),
        ],
        out_specs=pl.BlockSpec((tm, D), lambda i, j: (i, 0)),
        out_shape=jax.ShapeDtypeStruct((M, D), F32),
        scratch_shapes=[pltpu.VMEM((tm, D), BF16), pltpu.VMEM((tm, D), F32)],
        compiler_params=_cparams("parallel", "arbitrary"),
        name="ffn",
    )(h, g.reshape(1, D), w13, w13, w2)


def _nmm_kernel(h_ref, g_ref, w_ref, o_ref, nrm_ref):
    @pl.when(pl.program_id(1) == 0)
    def _():
        nrm_ref[...] = _rms(h_ref[...], g_ref[...]).astype(BF16)

    o_ref[...] = jnp.dot(nrm_ref[...], w_ref[...], preferred_element_type=F32)


def _norm_matmul(h, g, w, tm, tn):
    M, D = h.shape
    N = w.shape[1]
    return pl.pallas_call(
        _nmm_kernel,
        grid=(M // tm, N // tn),
        in_specs=[
            pl.BlockSpec((tm, D), lambda i, j: (i, 0)),
            pl.BlockSpec((1, D), lambda i, j: (0, 0)),
            pl.BlockSpec((D, tn), lambda i, j: (0, j)),
        ],
        out_specs=pl.BlockSpec((tm, tn), lambda i, j: (i, j)),
        out_shape=jax.ShapeDtypeStruct((M, N), F32),
        scratch_shapes=[pltpu.VMEM((tm, D), BF16)],
        compiler_params=_cparams("parallel", "arbitrary"),
        name="norm_matmul",
    )(h, g.reshape(1, D), w)


def _rwprep_kernel(h_ref, gm_ref, w_ref, mu_ref, w0_ref, w2_ref, a0_ref, a2_ref, g2_ref,
                   kk_ref, ka_ref, rk_ref, ones_ref,
                   r_out, lw_out, cum_out, k_out, v_out, a_out, b_out, g_out, bonus_out, last_ref, *, tiles_per_seq):
    i = pl.program_id(0)
    p = jnp.dot(_rms(h_ref[...], gm_ref[...]).astype(BF16), w_ref[...], preferred_element_type=F32)
    tm = p.shape[0]

    @pl.when(i % tiles_per_seq == 0)
    def _():
        last_ref[...] = jnp.zeros_like(last_ref)

    prev_row = last_ref[0:1, :]
    last_ref[0:1, :] = p[tm - 1:tm, :]
    rows = lax.broadcasted_iota(jnp.int32, p.shape, 0)
    shifted = jnp.where(rows == 0, prev_row, pltpu.roll(p, 1, 0))
    pm = p + (shifted - p) * mu_ref[...]

    r = pm[:, 0:512]
    k = pm[:, 512:1024]
    v = pm[:, 1024:1536]
    wa = pm[:, 1536:1664]
    gd = pm[:, 1664:1920]

    w_in = w0_ref[...] + _dot3(jnp.tanh(wa), w2_ref[...])
    lw = -math.exp(-0.5) * jax.nn.sigmoid(w_in)
    a_sig = jax.nn.sigmoid(a0_ref[...] + _dot3(wa, a2_ref[...]))

    C = RW_CHUNK
    ci = lax.broadcasted_iota(jnp.int32, (C, C), 0)
    cj = lax.broadcasted_iota(jnp.int32, (C, C), 1)
    tri = jnp.where(cj <= ci, 1.0, 0.0).astype(BF16)
    l1 = lw.astype(BF16)
    rem = lw - l1.astype(F32)
    l2 = rem.astype(BF16)
    l3 = (rem - l2.astype(F32)).astype(BF16)
    td = lambda x: jnp.dot(tri, x, preferred_element_type=F32)
    cum = jnp.concatenate([td(l1[c0:c0 + C]) + td(l2[c0:c0 + C]) + td(l3[c0:c0 + C])
                           for c0 in range(0, tm, C)], axis=0)
    g = _dot(jax.nn.sigmoid(gd), g2_ref[...])

    ones = ones_ref[...]
    kk = k * kk_ref[...]
    ss = _dot(kk * kk, ones)
    kk = kk * lax.rsqrt(jnp.maximum(ss, 1e-24))
    k2 = k * (1.0 + (a_sig - 1.0) * ka_ref[...])
    bonus = _dot(r * k2 * rk_ref[...], ones) * v

    g_out[...] = g
    bonus_out[...] = bonus
    a_vec = -kk
    b_vec = kk * a_sig
    for h in range(RW_HEADS):
        sl = slice(h * RW_HEAD_DIM, (h + 1) * RW_HEAD_DIM)
        r_out[h] = r[:, sl]
        lw_out[h] = lw[:, sl]
        cum_out[h] = cum[:, sl]
        k_out[h] = k2[:, sl]
        v_out[h] = v[:, sl]
        a_out[h] = a_vec[:, sl]
        b_out[h] = b_vec[:, sl]


def _rwkv_prep(h, g_mix, w_rw, B, T, mu, w0, w2p, a0, a2p, g2p, k_k, k_a, r_k, ones, tm=512):
    M, D = h.shape
    tps = T // tm
    row = lambda x: x.reshape(1, -1)
    full = lambda shape: pl.BlockSpec(shape, lambda i: (0,) * len(shape))
    head_spec = pl.BlockSpec((None, RW_HEADS, tm, RW_HEAD_DIM), lambda i: (i // tps, 0, i % tps, 0))
    head_shape = jax.ShapeDtypeStruct((B, RW_HEADS, T, RW_HEAD_DIM), F32)
    tok_spec = pl.BlockSpec((tm, RW_WIDTH), lambda i: (i, 0))
    tok_shape = jax.ShapeDtypeStruct((M, RW_WIDTH), F32)
    return pl.pallas_call(
        functools.partial(_rwprep_kernel, tiles_per_seq=tps),
        grid=(M // tm,),
        in_specs=[
            pl.BlockSpec((tm, D), lambda i: (i, 0)), full((1, D)), full((D, RW_PAD_COLS)),
            full((1, RW_PAD_COLS)), full((1, RW_WIDTH)), full((128, RW_WIDTH)),
            full((1, RW_WIDTH)), full((128, RW_WIDTH)), full((256, RW_WIDTH)),
            full((1, RW_WIDTH)), full((1, RW_WIDTH)), full((1, RW_WIDTH)), full((RW_WIDTH, RW_WIDTH)),
        ],
        out_specs=[head_spec] * 7 + [tok_spec] * 2,
        out_shape=[head_shape] * 7 + [tok_shape] * 2,
        scratch_shapes=[pltpu.VMEM((8, RW_PAD_COLS), F32)],
        compiler_params=_cparams("arbitrary"),
        name="rwkv_prep",
    )(h, row(g_mix), w_rw, row(mu), row(w0), w2p, row(a0), a2p, g2p, row(k_k), row(k_a), row(r_k), ones)


def _bmm(a, b, passes, nt=False):
    spec = 'gik,gjk->gij' if nt else 'gij,gjk->gik'
    e = lambda x, y: jnp.einsum(spec, x, y, preferred_element_type=F32)
    if passes == 1:
        return e(a.astype(BF16), b.astype(BF16))
    ah, al = _split2(a)
    bh, bl = _split2(b)
    return e(ah, bh) + e(ah, bl) + e(al, bh)


def _rwkv_kernel(r_ref, lw_ref, cum_ref, k_ref, v_ref, a_ref, b_ref, bonus_ref, g_ref, lnw_ref, lnb_ref, ones_ref,
                 y_ref, st_ref, ytok_ref):
    C = RW_CHUNK
    H = RW_HEADS
    Tc = r_ref.shape[1]
    nc = Tc // C
    G = H * nc

    @pl.when(pl.program_id(1) == 0)
    def _():
        st_ref[...] = jnp.zeros_like(st_ref)

    shp = lambda ref: ref[...].reshape(G, C, RW_HEAD_DIM)
    r, lw, L, k, v, a, b = (shp(x) for x in (r_ref, lw_ref, cum_ref, k_ref, v_ref, a_ref, b_ref))

    ti = lax.broadcasted_iota(jnp.int32, (C, C), 0)
    tj = lax.broadcasted_iota(jnp.int32, (C, C), 1)
    eye = jnp.where(ti == tj, 1.0, 0.0).astype(F32)
    wi = lax.broadcasted_iota(jnp.int32, (C, 2 * C), 0)
    wj = lax.broadcasted_iota(jnp.int32, (C, 2 * C), 1)
    second = wj >= C
    wjc = jnp.where(second, wj - C, wj)

    Lprev = L - lw
    Ltot = L[:, C - 1:C, :]

    eL = jnp.exp(L)
    enL = jnp.exp(-L)
    eh = jnp.exp(Ltot - L)
    At = a * jnp.exp(Lprev)
    Rt = r * eL
    BKt = jnp.concatenate([b * enL, k * enL], axis=1)
    BKh = jnp.concatenate([b * eh, k * eh], axis=1)

    P4 = _bmm(jnp.concatenate([At, Rt], axis=1), BKt, RW_PASSES_PROD, nt=True)
    top, bot = P4[:, :C, :], P4[:, C:, :]
    N = jnp.where((tj < ti)[None], top[:, :, :C], 0.0)
    AKz = jnp.where((second & (wjc < wi))[None], top, 0.0)
    RBK = jnp.where((wjc <= wi)[None], bot, 0.0)
    RB = RBK[:, :, :C]

    def siblings(s):
        return ((ti // (2 * s)) == (tj // (2 * s))) & ((ti // s) != (tj // s))

    X = eye[None] + jnp.where(siblings(1)[None], N, 0.0)
    s_blk = 2
    while s_blk < C:
        XE = _bmm(X, jnp.where(siblings(s_blk)[None], N, 0.0), RW_PASSES_INV)
        X = X + _bmm(XE, X, RW_PASSES_INV)
        s_blk *= 2

    zv = jnp.concatenate([jnp.zeros_like(v), v], axis=1)
    Abar = _bmm(X, At, RW_PASSES_APPLY)
    W0 = _bmm(X, _bmm(AKz, zv, RW_PASSES_APPLY), RW_PASSES_APPLY)
    wv = jnp.concatenate([W0, v], axis=1)
    Y0 = _bmm(RBK, wv, RW_PASSES_APPLY)
    Rbar = Rt + _bmm(RB, Abar, RW_PASSES_APPLY)
    BKhT = jnp.swapaxes(BKh, 1, 2)
    Mtx = _bmm(BKhT[:, :, :C], Abar, RW_PASSES_APPLY) + eye[None] * jnp.exp(Ltot)
    G0 = _bmm(BKhT, wv, RW_PASSES_APPLY)

    hsplit = lambda x: x.reshape(H, nc, x.shape[1], x.shape[2])
    RM = hsplit(jnp.concatenate([Rbar, Mtx], axis=1))
    Y0, G0 = hsplit(Y0), hsplit(G0)
    St = st_ref[...]
    ys = []
    for c in range(nc):
        both = _bmm(RM[:, c], St, RW_PASSES_STATE)
        ys.append(both[:, :C] + Y0[:, c])
        St = both[:, C:] + G0[:, c]
    st_ref[...] = St
    y = jnp.concatenate(ys, axis=1)
    for h in range(H):
        ytok_ref[:, h * RW_HEAD_DIM:(h + 1) * RW_HEAD_DIM] = y[h]

    y = ytok_ref[...]
    ones = ones_ref[...]
    mean = _dot(y, ones) * (1.0 / RW_HEAD_DIM)
    yc = y - mean
    var = _dot(yc * yc, ones) * (1.0 / RW_HEAD_DIM)
    ya = (yc * lax.rsqrt(var + RW_GN_EPS) * lnw_ref[...] + lnb_ref[...] + bonus_ref[...]) * g_ref[...]
    y_ref[...] = ya.astype(BF16)


def _rwkv_recurrence(r, lw, cum, k, v, a, b, bonus, g, ln_w, ln_b, ones, tc=256):
    B, H, T, Dh = r.shape
    W = H * Dh
    nt = T // tc
    spec = pl.BlockSpec((None, H, tc, Dh), lambda bi, c: (bi, 0, c, 0))
    tok = pl.BlockSpec((tc, W), lambda bi, c: (bi * nt + c, 0))
    const = lambda shape: pl.BlockSpec(shape, lambda bi, c: (0, 0))
    return pl.pallas_call(
        _rwkv_kernel,
        grid=(B, nt),
        in_specs=[spec] * 7 + [tok, tok, const((1, W)), const((1, W)), const((W, W))],
        out_specs=tok,
        out_shape=jax.ShapeDtypeStruct((B * T, W), BF16),
        scratch_shapes=[pltpu.VMEM((H, Dh, Dh), F32), pltpu.VMEM((tc, W), F32)],
        compiler_params=_cparams("parallel", "arbitrary"),
        name="rwkv_recurrence",
    )(r, lw, cum, k, v, a, b, bonus, g, ln_w.reshape(1, W), ln_b.reshape(1, W), ones)


def _rope_tables(T, heads, head_dim, rot_dim, base):
    half = rot_dim // 2
    inv_freq = base ** (-jnp.arange(half, dtype=F32) / half)
    ang = jnp.arange(T).astype(F32)[:, None] * inv_freq[None, :]
    cos, sin = jnp.cos(ang), jnp.sin(ang)
    rest = head_dim - rot_dim
    c = jnp.concatenate([cos, cos, jnp.ones((T, rest), F32)], axis=1)
    s_lo = jnp.concatenate([-sin, jnp.zeros((T, half + rest), F32)], axis=1)
    s_hi = jnp.concatenate([jnp.zeros((T, half), F32), sin, jnp.zeros((T, rest), F32)], axis=1)
    tile = lambda x: jnp.tile(x, (1, heads))
    return tile(c), tile(s_lo), tile(s_hi)


def _apply_rope(x, c, s_lo, s_hi, half):
    n = x.shape[-1]
    return x * c + pltpu.roll(x, n - half, 1) * s_lo + pltpu.roll(x, half, 1) * s_hi


def _dilproj_kernel(h_ref, g_ref, w_ref, c_ref, slo_ref, shi_ref, gq_ref, gk_ref, ones_ref,
                    o0_ref, o1_ref, o2_ref, tmp_ref):
    u = _rms(h_ref[...], g_ref[...]).astype(BF16)
    c, s_lo, s_hi = c_ref[...], slo_ref[...], shi_ref[...]
    W = DIL_WIDTH
    for grp, o_ref in enumerate((o0_ref, o1_ref, o2_ref)):
        dilation = DIL_PATTERNS[grp][1]
        rows = u.shape[0] // dilation
        for which in range(3):
            col = (3 * grp + which) * W
            val = jnp.dot(u, w_ref[:, col:col + W], preferred_element_type=F32)
            if which < 2:
                gain = (gq_ref if which == 0 else gk_ref)[grp:grp + 1, :]
                ms = jnp.dot((val * val).astype(BF16), ones_ref[...], preferred_element_type=F32) * (1.0 / HEAD_DIM)
                val = _apply_rope(val * lax.rsqrt(ms + EPS) * gain, c, s_lo, s_hi, ROPE_DIM // 2)
            if dilation == 1:
                o_ref[which, 0] = val.astype(BF16)
            else:
                _store_lane_groups(tmp_ref, val)
                for res in range(dilation):
                    for gl in range(tmp_ref.shape[0]):
                        piece = tmp_ref[gl, pl.ds(res, rows, stride=dilation), :]
                        o_ref[which, res, :, gl * LANES:(gl + 1) * LANES] = piece.astype(BF16)


def _dilated_projection(h, g_mix, w_dil, B, T, tabs, gq, gk, ones, tm=512):
    M, D = h.shape
    tps = T // tm
    W = DIL_WIDTH
    const = lambda shape: pl.BlockSpec(shape, lambda i: (0,) * len(shape))
    tspec = pl.BlockSpec((tm, W), lambda i: (i % tps, 0))
    out_specs, out_shape = [], []
    for _, d in DIL_PATTERNS:
        out_specs.append(pl.BlockSpec((3, None, d, tm // d, W), lambda i: (0, i // tps, 0, i % tps, 0)))
        out_shape.append(jax.ShapeDtypeStruct((3, B, d, T // d, W), BF16))
    return pl.pallas_call(
        _dilproj_kernel,
        grid=(M // tm,),
        in_specs=[pl.BlockSpec((tm, D), lambda i: (i, 0)), const((1, D)), const((D, 3 * N_DIL * W)),
                  tspec, tspec, tspec, const((N_DIL, W)), const((N_DIL, W)), const((W, W))],
        out_specs=out_specs,
        out_shape=out_shape,
        scratch_shapes=[pltpu.VMEM((W // LANES, tm, LANES), F32)],
        compiler_params=_cparams("parallel"),
        name="dilated_projection",
    )(h, g_mix.reshape(1, D), w_dil, *tabs, jnp.tile(gq * Q_PRESCALE, (1, DIL_HEADS)), jnp.tile(gk, (1, DIL_HEADS)), ones)


def _dilattn_kernel(q_ref, k_ref, v_ref, kprev_ref, vprev_ref, o_ref, lse_ref, *, blocks):
    i = pl.program_id(2)
    Lb = DIL_BLOCK
    PW = 2 * HEAD_DIM
    NP = DIL_WIDTH // PW
    nres = q_ref.shape[0]
    lane = lax.broadcasted_iota(jnp.int32, (Lb, PW), 1)
    low = lane < HEAD_DIM

    qs, ks, vs = [], [], []
    for blk in range(blocks):
        rows = slice(blk * Lb, (blk + 1) * Lb)
        for rr in range(nres):
            for p in range(NP):
                ln = slice(p * PW, (p + 1) * PW)
                q2 = q_ref[rr, rows, ln]
                zero = jnp.zeros_like(q2)
                qs.append(jnp.concatenate([jnp.where(low, q2, zero), jnp.where(low, zero, q2)], axis=0))
                if blk == 0:
                    ks.append(jnp.concatenate([kprev_ref[rr, :, ln], k_ref[rr, rows, ln]], axis=0))
                    vs.append(jnp.concatenate([vprev_ref[rr, :, ln], v_ref[rr, rows, ln]], axis=0))
                else:
                    both = slice((blk - 1) * Lb, (blk + 1) * Lb)
                    ks.append(k_ref[rr, both, ln])
                    vs.append(v_ref[rr, both, ln])
    Q, K, V = jnp.stack(qs), jnp.stack(ks), jnp.stack(vs)
    first = nres * NP

    s = jnp.einsum('gqd,gkd->gqk', Q, K, preferred_element_type=F32)
    qi = lax.broadcasted_iota(jnp.int32, (2 * Lb, 2 * Lb), 0) % Lb
    kj = lax.broadcasted_iota(jnp.int32, (2 * Lb, 2 * Lb), 1)
    window = (kj >= qi) & (kj <= qi + Lb)
    window0 = window & ((kj >= Lb) | (i > 0))
    s = jnp.concatenate([jnp.where(window0[None], s[:first], -jnp.inf),
                         jnp.where(window[None], s[first:], -jnp.inf)], axis=0) if blocks > 1 else \
        jnp.where(window0[None], s, -jnp.inf)
    m = jnp.max(s, axis=-1, keepdims=True)
    e = jnp.exp2(s - m).astype(BF16)
    vx = jnp.concatenate([V, jnp.ones_like(V)], axis=-1)
    nd = jnp.einsum('gqk,gkd->gqd', e, vx, preferred_element_type=F32)
    num, den = nd[:, :, :PW], nd[:, :, PW:]
    o2 = num / den
    l2 = m * math.log(2.0) + jnp.log(den)
    head_of_lane = lane // LSE_LANES
    for blk in range(blocks):
        rows = slice(blk * Lb, (blk + 1) * Lb)
        for rr in range(nres):
            lse_c = jnp.zeros((Lb, PW), F32)
            for p in range(NP):
                g = (blk * nres + rr) * NP + p
                ln = slice(p * PW, (p + 1) * PW)
                o_ref[rr, rows, ln] = jnp.where(low, o2[g, :Lb], o2[g, Lb:]).astype(BF16)
                lse_c = jnp.where(head_of_lane == 2 * p, l2[g, :Lb],
                                  jnp.where(head_of_lane == 2 * p + 1, l2[g, Lb:], lse_c))
            lse_ref[rr, rows, :] = lse_c


def _dilated_attention(qkv):
    _, B, d, Mr, W = qkv.shape
    Lb = DIL_BLOCK
    nb = Mr // Lb
    blocks = min(DIL_BLOCKS_PER_STEP, nb)
    nres = min(DIL_BLOCKS_PER_STEP // blocks, d)
    tile = blocks * Lb

    def cur(which):
        return pl.BlockSpec((None, None, nres, tile, W), lambda b, c, i: (which, b, c, i, 0))

    def prev(which):
        return pl.BlockSpec((None, None, nres, Lb, W),
                            lambda b, c, i: (which, b, c, jnp.maximum(i * blocks - 1, 0), 0))

    ospec = pl.BlockSpec((None, nres, tile, W), lambda b, c, i: (b, c, i, 0))
    lspec = pl.BlockSpec((None, nres, tile, LANES), lambda b, c, i: (b, c, i, 0))
    return pl.pallas_call(
        functools.partial(_dilattn_kernel, blocks=blocks),
        grid=(B, d // nres, nb // blocks),
        in_specs=[cur(0), cur(1), cur(2), prev(1), prev(2)],
        out_specs=[ospec, lspec],
        out_shape=[jax.ShapeDtypeStruct((B, d, Mr, W), BF16), jax.ShapeDtypeStruct((B, d, Mr, LANES), F32)],
        compiler_params=_cparams("parallel", "parallel", "parallel"),
        name=f"dilated_attention_d{d}",
    )(qkv, qkv, qkv, qkv, qkv)


def _ret_kernel(h_ref, gm_ref, w_ref, c_ref, slo_ref, shi_ref, gain_ref, o_ref, st_ref):
    C = RET_CHUNK
    QK = RET_HEADS * RET_QK_DIM

    @pl.when(pl.program_id(1) == 0)
    def _():
        st_ref[...] = jnp.zeros_like(st_ref)

    u = _rms(h_ref[...], gm_ref[...]).astype(BF16)
    p = jnp.dot(u, w_ref[...], preferred_element_type=F32)
    c, s_lo, s_hi = c_ref[...], slo_ref[...], shi_ref[...]
    q = _apply_rope(p[:, :QK], c, s_lo, s_hi, RET_QK_DIM // 2)
    k = _apply_rope(p[:, QK:2 * QK], c, s_lo, s_hi, RET_QK_DIM // 2) * (RET_QK_DIM ** -0.5)
    v = p[:, 2 * QK:2 * QK + RET_WIDTH]
    g = p[:, 2 * QK + RET_WIDTH:]
    gain = gain_ref[...]

    ji = lax.broadcasted_iota(jnp.int32, (C, C), 0)
    jj = lax.broadcasted_iota(jnp.int32, (C, C), 1)
    diff = (ji - jj).astype(F32)
    jcol = lax.broadcasted_iota(jnp.int32, (C, 1), 0).astype(F32)

    for h in range(RET_HEADS):
        lg = math.log(1.0 - 2.0 ** (-5.0 - h))
        qs = slice(h * RET_QK_DIM, (h + 1) * RET_QK_DIM)
        vs = slice(h * RET_V_DIM, (h + 1) * RET_V_DIM)
        decay_in = jnp.where(diff >= 0, jnp.exp(lg * jnp.maximum(diff, 0.0)), 0.0)
        q_dec = jnp.exp(lg * (jcol + 1.0))
        k_dec = jnp.exp(lg * (C - 1.0 - jcol))
        S = st_ref[h]
        for ch in range(p.shape[0] // C):
            rows = slice(ch * C, (ch + 1) * C)
            qh, kh, vh = q[rows, qs], k[rows, qs], v[rows, vs]
            s = lax.dot_general(qh.astype(BF16), kh.astype(BF16), (((1,), (1,)), ((), ())),
                                preferred_element_type=F32) * decay_in
            y = _dot(s, vh) + _dot(qh * q_dec, S)
            kv = lax.dot_general((kh * k_dec).astype(BF16), vh.astype(BF16), (((0,), (0,)), ((), ())),
                                 preferred_element_type=F32)
            S = math.exp(lg * C) * S + kv
            yn = y * lax.rsqrt(jnp.mean(y * y, axis=-1, keepdims=True) + EPS) * gain[:, vs]
            gh = g[rows, vs]
            o_ref[rows, vs] = (gh * jax.nn.sigmoid(gh) * yn).astype(BF16)
        st_ref[h] = S


def _retention(h, g_mix, w_ret, B, T, tabs, gain, tr=1024):
    M, D = h.shape
    nt = T // tr
    qk = RET_HEADS * RET_QK_DIM
    const = lambda shape: pl.BlockSpec(shape, lambda b, t: (0,) * len(shape))
    tspec = pl.BlockSpec((tr, qk), lambda b, t: (t, 0))
    return pl.pallas_call(
        _ret_kernel,
        grid=(B, nt),
        in_specs=[pl.BlockSpec((tr, D), lambda b, t: (b * nt + t, 0)), const((1, D)), const(w_ret.shape),
                  tspec, tspec, tspec, const((1, RET_WIDTH))],
        out_specs=pl.BlockSpec((tr, RET_WIDTH), lambda b, t: (b * nt + t, 0)),
        out_shape=jax.ShapeDtypeStruct((M, RET_WIDTH), BF16),
        scratch_shapes=[pltpu.VMEM((RET_HEADS, RET_QK_DIM, RET_V_DIM), F32)],
        compiler_params=_cparams("parallel", "arbitrary"),
        name="retention",
    )(h, g_mix.reshape(1, D), w_ret, *tabs, gain.reshape(1, -1))


def _merge_kernel(h_ref, gm_ref, wg_ref, ya_ref, o0_ref, o1_ref, o2_ref, l0_ref, l1_ref, l2_ref, yc_ref,
                  wa_ref, wb_ref, wc_ref, wo_ref, expand_ref, out_ref, o1_scr, o2_scr, l1_scr, l2_scr):
    h = h_ref[...]
    u = _rms(h, gm_ref[...]).astype(BF16)

    def token_order(ref, scr):
        dil, rows = ref.shape[0], ref.shape[1]
        for c in range(dil):
            blk = ref[c].astype(F32)
            for gl in range(scr.shape[0]):
                scr[gl, pl.ds(c, rows, stride=dil), :] = blk[:, gl * LANES:(gl + 1) * LANES]
        return jnp.concatenate([scr[gl] for gl in range(scr.shape[0])], axis=-1)

    o0, l0 = o0_ref[0].astype(F32), l0_ref[0]
    o1, l1 = token_order(o1_ref, o1_scr), token_order(l1_ref, l1_scr)
    o2, l2 = token_order(o2_ref, o2_scr), token_order(l2_ref, l2_scr)

    mx = jnp.maximum(jnp.maximum(l0, l1), l2)
    e0, e1, e2 = jnp.exp(l0 - mx), jnp.exp(l1 - mx), jnp.exp(l2 - mx)
    inv = 1.0 / (e0 + e1 + e2)

    def spread(w):
        hi, lo = _split2(w)
        return (jnp.dot(hi, expand_ref[...], preferred_element_type=F32)
                + jnp.dot(lo, expand_ref[...], preferred_element_type=F32))

    yb = spread(e0 * inv) * o0 + spread(e1 * inv) * o1 + spread(e2 * inv) * o2

    D = D_MODEL
    gate = lambda i: jax.nn.sigmoid(jnp.dot(u, wg_ref[:, i * D:(i + 1) * D], preferred_element_type=F32))
    merged = (gate(0) * jnp.dot(ya_ref[...], wa_ref[...], preferred_element_type=F32)
              + gate(1) * _dot(yb, wb_ref[...])
              + gate(2) * jnp.dot(yc_ref[...], wc_ref[...], preferred_element_type=F32))
    out_ref[...] = h + _dot(merged, wo_ref[...])


def _merge(h, T, g_mix, w_gate, y_a, o, lse, y_c, wa, wb, wc, wo, tm=512):
    M, D = h.shape
    tps = T // tm
    tok = lambda w: pl.BlockSpec((tm, w), lambda i: (i, 0))
    full = lambda shape: pl.BlockSpec(shape, lambda i: (0, 0))
    W = RW_WIDTH

    def res(group, width):
        d = DIL_PATTERNS[group][1]
        return pl.BlockSpec((None, d, tm // d, width), lambda i: (i // tps, 0, i % tps, 0))

    li = lax.broadcasted_iota(jnp.int32, (LANES, W), 0)
    lj = lax.broadcasted_iota(jnp.int32, (LANES, W), 1)
    expand = jnp.where(li == (lj // HEAD_DIM) * LSE_LANES, 1.0, 0.0).astype(BF16)

    return pl.pallas_call(
        _merge_kernel,
        grid=(M // tm,),
        in_specs=[tok(D), full((1, D)), full((D, 3 * D)), tok(W),
                  res(0, W), res(1, W), res(2, W), res(0, LANES), res(1, LANES), res(2, LANES), tok(W),
                  full((W, D)), full((W, D)), full((W, D)), full((D, D)), full((LANES, W))],
        out_specs=tok(D),
        out_shape=jax.ShapeDtypeStruct((M, D), F32),
        scratch_shapes=[pltpu.VMEM((W // LANES, tm, LANES), F32)] * 2 + [pltpu.VMEM((1, tm, LANES), F32)] * 2,
        compiler_params=_cparams("parallel"),
        name="gated_merge",
    )(h, g_mix.reshape(1, D), w_gate, y_a, o[0], o[1], o[2], lse[0], lse[1], lse[2], y_c, wa, wb, wc, wo, expand)


def _xattn_kernel(h_ref, gx_ref, wq_ref, kv_ref, qn_ref, kn_ref, wo_ref, out_ref, k_scr, v_scr):
    D = D_MODEL

    @pl.when(pl.program_id(1) == 0)
    def _():
        kv = kv_ref[...]
        for hd in range(XA_HEADS):
            sl = slice(hd * XA_HEAD_DIM, (hd + 1) * XA_HEAD_DIM)
            k_scr[:, sl] = _rms(kv[:, sl], kn_ref[...]).astype(BF16)
        v_scr[...] = kv[:, D:].astype(BF16)

    h = h_ref[...]
    hn = _rms(h, gx_ref[...]).astype(BF16)
    q = jnp.dot(hn, wq_ref[...], preferred_element_type=F32)
    outs = []
    for hd in range(XA_HEADS):
        sl = slice(hd * XA_HEAD_DIM, (hd + 1) * XA_HEAD_DIM)
        qh = _rms(q[:, sl], qn_ref[...]).astype(BF16)
        s = lax.dot_general(qh, k_scr[:, sl], (((1,), (1,)), ((), ())), preferred_element_type=F32)
        m = jnp.max(s, axis=-1, keepdims=True)
        e = jnp.exp2(s - m)
        pr = e / jnp.sum(e, axis=-1, keepdims=True)
        outs.append(jnp.dot(pr.astype(BF16), v_scr[:, sl], preferred_element_type=F32))
    o = jnp.concatenate(outs, axis=-1).astype(BF16)
    out_ref[...] = h + jnp.dot(o, wo_ref[...], preferred_element_type=F32)


def _cross_attention(h, B, T, g_x, wq, kv, q_norm, k_norm, wo, tm=1024):
    M, D = h.shape
    tps = T // tm
    full = lambda shape: pl.BlockSpec(shape, lambda b, t: (0, 0))
    return pl.pallas_call(
        _xattn_kernel,
        grid=(B, tps),
        in_specs=[
            pl.BlockSpec((tm, D), lambda b, t: (b * tps + t, 0)),
            full((1, D)), full((D, D)),
            pl.BlockSpec((MEM_LEN, 2 * D), lambda b, t: (b, 0)),
            full((1, XA_HEAD_DIM)), full((1, XA_HEAD_DIM)), full((D, D)),
        ],
        out_specs=pl.BlockSpec((tm, D), lambda b, t: (b * tps + t, 0)),
        out_shape=jax.ShapeDtypeStruct((M, D), F32),
        scratch_shapes=[pltpu.VMEM((MEM_LEN, D), BF16), pltpu.VMEM((MEM_LEN, D), BF16)],
        compiler_params=_cparams("parallel", "arbitrary"),
        name="cross_attention",
    )(h, g_x.reshape(1, D), wq, kv, (q_norm * (XA_HEAD_DIM ** -0.5 * math.log2(math.e))).reshape(1, -1),
      k_norm.reshape(1, -1), wo)


def _pad_rows(w, rows_before, total):
    return jnp.pad(w, ((rows_before, total - rows_before - w.shape[0]), (0, 0)))


def _layer(h, mem2, B, T, p, dil_tabs, ret_tabs):
    bf = lambda w: w.astype(BF16)
    ones = _block_ones(RW_WIDTH, RW_HEAD_DIM)
    h = _ffn(h, p['norm_ffn1'], bf(p['ffn1_w13']), bf(p['ffn1_w2']))

    w_in = p['w_in']
    rw_w = jnp.pad(w_in[:, :1824], ((0, 0), (0, RW_PAD_COLS - 1824)))
    mu = jnp.pad(p['rw_mu'], (0, RW_PAD_COLS - 1824))

    w2p = _pad_rows(p['rw_w2'], 0, 128)
    a2p = _pad_rows(p['rw_a2'], 64, 128)
    g2p = _pad_rows(p['rw_g2'], 0, 256)
    r, lw, cum, k, v, a, b, g, bonus = _rwkv_prep(h, p['norm_mix'], bf(rw_w), B, T, mu, p['rw_w0'], w2p, p['rw_a0'], a2p, g2p,
                                             p['rw_k_k'], p['rw_k_a'], p['rw_r_k'], ones)
    y_a = _rwkv_recurrence(r, lw, cum, k, v, a, b, bonus, g, p['rw_ln_w'], p['rw_ln_b'], ones)

    qkvs = _dilated_projection(h, p['norm_mix'], bf(w_in[:, 1824:6432]), B, T, dil_tabs,
                               p['dil_q_norm'], p['dil_k_norm'], ones)
    o, lse = zip(*[_dilated_attention(qkv) for qkv in qkvs])

    y_c = _retention(h, p['norm_mix'], bf(w_in[:, 6432:7968]), B, T, ret_tabs, p['ret_norm'])

    h = _merge(h, T, p['norm_mix'], bf(w_in[:, 7968:]), y_a, o, lse, y_c,
               bf(p['w_branch_rwkv']), bf(p['w_branch_dil']), bf(p['w_branch_ret']), bf(p['w_out']))

    kv = _norm_matmul(mem2, p['norm_mem'], bf(p['xa_wkv']), MEM_LEN, 1024)
    h = _cross_attention(h, B, T, p['norm_xattn'], bf(p['xa_wq']), kv, p['xa_q_norm'], p['xa_k_norm'],
                         bf(p['xa_wo']))
    h = _ffn(h, p['norm_ffn2'], bf(p['ffn2_w13']), bf(p['ffn2_w2']))
    return h


_PARAM_NAMES = ('norm_ffn1', 'ffn1_w13', 'ffn1_w2', 'norm_mix', 'w_in', 'rw_mu', 'rw_w0', 'rw_w2', 'rw_a0',
                'rw_a2', 'rw_g2', 'rw_k_k', 'rw_k_a', 'rw_r_k', 'rw_ln_w', 'rw_ln_b', 'dil_q_norm',
                'dil_k_norm', 'ret_norm', 'w_branch_rwkv', 'w_branch_dil', 'w_branch_ret', 'w_out',
                'norm_xattn', 'norm_mem', 'xa_wq', 'xa_wkv', 'xa_q_norm', 'xa_k_norm', 'xa_wo',
                'norm_ffn2', 'ffn2_w13', 'ffn2_w2')


def kernel(x, mem, norm_ffn1, ffn1_w13, ffn1_w2, norm_mix, w_in, rw_mu, rw_w0, rw_w2, rw_a0, rw_a2, rw_g2, rw_k_k, rw_k_a, rw_r_k, rw_ln_w, rw_ln_b, dil_q_norm, dil_k_norm, ret_norm, w_branch_rwkv, w_branch_dil, w_branch_ret, w_out, norm_xattn, norm_mem, xa_wq, xa_wkv, xa_q_norm, xa_k_norm, xa_wo, norm_ffn2, ffn2_w13, ffn2_w2):
    params = dict(zip(_PARAM_NAMES, (norm_ffn1, ffn1_w13, ffn1_w2, norm_mix, w_in, rw_mu, rw_w0, rw_w2, rw_a0,
                                     rw_a2, rw_g2, rw_k_k, rw_k_a, rw_r_k, rw_ln_w, rw_ln_b, dil_q_norm,
                                     dil_k_norm, ret_norm, w_branch_rwkv, w_branch_dil, w_branch_ret, w_out,
                                     norm_xattn, norm_mem, xa_wq, xa_wkv, xa_q_norm, xa_k_norm, xa_wo,
                                     norm_ffn2, ffn2_w13, ffn2_w2)))
    B, T, D = x.shape
    assert D == D_MODEL and T % 2048 == 0 and mem.shape[1] == MEM_LEN
    depth = norm_ffn1.shape[0]
    dil_tabs = _rope_tables(T, DIL_HEADS, HEAD_DIM, ROPE_DIM, ROPE_THETA)
    ret_tabs = _rope_tables(T, RET_HEADS, RET_QK_DIM, RET_QK_DIM, RET_ROPE_BASE)
    h = x.reshape(B * T, D)
    mem2 = mem.reshape(B * MEM_LEN, D)
    for l in range(depth):
        h = _layer(h, mem2, B, T, {n: params[n][l] for n in _PARAM_NAMES}, dil_tabs, ret_tabs)
    return h.reshape(B, T, D)
```

```python
import functools
import math

import jax
import jax.numpy as jnp
from jax import lax
from jax.experimental import pallas as pl
from jax.experimental.pallas import tpu as pltpu

F32 = jnp.float32
BF16 = jnp.bfloat16

D_MODEL = 1024
D_FF = 2816
EPS = 1e-6

RW_HEADS = 8
RW_HEAD_DIM = 64
RW_WIDTH = 512
RW_GN_EPS = 64e-5
RW_CHUNK = 64
RW_LORA = 64
RW_GATE_LORA = 160
RW_IN_COLS = 3 * RW_WIDTH + 2 * RW_LORA + RW_GATE_LORA
RW_LORA_PAD = 128
RW_GATE_PAD = 256
RW_PAD_COLS = 3 * RW_WIDTH + RW_LORA_PAD + RW_GATE_PAD

DIL_PATTERNS = ((128, 1), (512, 4), (2048, 16))
N_DIL = 3
DIL_HEADS = 8
HEAD_DIM = 64
DIL_WIDTH = 512
DIL_BLOCK = 128
DIL_BLOCKS_PER_STEP = 8
LSE_LANES = 16
ROPE_THETA = 500000.0
Q_PRESCALE = HEAD_DIM ** -0.5 * math.log2(math.e)
ROPE_DIM = 16

RET_HEADS = 4
RET_QK_DIM = 64
RET_V_DIM = 128
RET_CHUNK = 128
RET_ROPE_BASE = 10000.0
RET_WIDTH = 512

XA_HEADS = 4
XA_HEAD_DIM = 256
MEM_LEN = 256

V7X_VMEM_BYTES = 64 * 1024 * 1024
VMEM_LIMIT = V7X_VMEM_BYTES - 8 * 1024 * 1024
LANES = 128

TILE_FFN = 1024
TILE_FFN_COLS = 1408
TILE_RWKV_PREP = 512
TILE_RWKV_SCAN = 256
TILE_DIL_PROJ = 512
TILE_RETENTION = 1024
TILE_MERGE = 512
TILE_XATTN = 1024


def _cparams(*sem):
    return pltpu.CompilerParams(dimension_semantics=sem, vmem_limit_bytes=VMEM_LIMIT)


def _dot(a, b):
    return jnp.dot(a.astype(BF16), b.astype(BF16), preferred_element_type=F32)


def _split2(x):
    hi = x.astype(BF16)
    lo = (x - hi.astype(F32)).astype(BF16)
    return hi, lo


def _dot3(a, b):
    ah, al = _split2(a)
    bh, bl = _split2(b)
    return (jnp.dot(ah, bh, preferred_element_type=F32)
            + jnp.dot(ah, bl, preferred_element_type=F32)
            + jnp.dot(al, bh, preferred_element_type=F32))


def _rms(x, g):
    return x * lax.rsqrt(jnp.mean(x * x, axis=-1, keepdims=True) + EPS) * g


def _store_lane_groups(ref, x):
    for gl in range(ref.shape[0]):
        ref[gl] = x[:, gl * LANES:(gl + 1) * LANES]


def _block_ones(n, width):
    i = lax.broadcasted_iota(jnp.int32, (n, n), 0) // width
    j = lax.broadcasted_iota(jnp.int32, (n, n), 1) // width
    return jnp.where(i == j, 1.0, 0.0).astype(BF16)


def _ffn_kernel(h_ref, g_ref, w1_ref, w3_ref, w2_ref, o_ref, nrm_ref, acc_ref):
    j = pl.program_id(1)

    @pl.when(j == 0)
    def _():
        nrm_ref[...] = _rms(h_ref[...], g_ref[...]).astype(BF16)
        acc_ref[...] = jnp.zeros_like(acc_ref)

    n = nrm_ref[...]
    a = jnp.dot(n, w1_ref[...], preferred_element_type=F32)
    b = jnp.dot(n, w3_ref[...], preferred_element_type=F32)
    mid = (a * jax.nn.sigmoid(a) * b).astype(BF16)
    acc_ref[...] += jnp.dot(mid, w2_ref[...], preferred_element_type=F32)

    @pl.when(j == pl.num_programs(1) - 1)
    def _():
        o_ref[...] = h_ref[...] + 0.5 * acc_ref[...]


def _ffn(h, g, w13, w2, tm=TILE_FFN, tf=TILE_FFN_COLS):
    M, D = h.shape
    nff = D_FF // tf
    return pl.pallas_call(
        _ffn_kernel,
        grid=(M // tm, nff),
        in_specs=[
            pl.BlockSpec((tm, D), lambda i, j: (i, 0)),
            pl.BlockSpec((1, D), lambda i, j: (0, 0)),
            pl.BlockSpec((D, tf), lambda i, j: (0, j)),
            pl.BlockSpec((D, tf), lambda i, j: (0, j + nff)),
            pl.BlockSpec((tf, D), lambda i, j: (j, 0)),
        ],
        out_specs=pl.BlockSpec((tm, D), lambda i, j: (i, 0)),
        out_shape=jax.ShapeDtypeStruct((M, D), F32),
        scratch_shapes=[pltpu.VMEM((tm, D), BF16), pltpu.VMEM((tm, D), F32)],
        compiler_params=_cparams("parallel", "arbitrary"),
        name="ffn",
    )(h, g.reshape(1, D), w13, w13, w2)


def _nmm_kernel(h_ref, g_ref, w_ref, o_ref, nrm_ref):
    @pl.when(pl.program_id(1) == 0)
    def _():
        nrm_ref[...] = _rms(h_ref[...], g_ref[...]).astype(BF16)

    o_ref[...] = jnp.dot(nrm_ref[...], w_ref[...], preferred_element_type=F32)


def _norm_matmul(h, g, w, tm, tn):
    M, D = h.shape
    N = w.shape[1]
    return pl.pallas_call(
        _nmm_kernel,
        grid=(M // tm, N // tn),
        in_specs=[
            pl.BlockSpec((tm, D), lambda i, j: (i, 0)),
            pl.BlockSpec((1, D), lambda i, j: (0, 0)),
            pl.BlockSpec((D, tn), lambda i, j: (0, j)),
        ],
        out_specs=pl.BlockSpec((tm, tn), lambda i, j: (i, j)),
        out_shape=jax.ShapeDtypeStruct((M, N), F32),
        scratch_shapes=[pltpu.VMEM((tm, D), BF16)],
        compiler_params=_cparams("parallel", "arbitrary"),
        name="norm_matmul",
    )(h, g.reshape(1, D), w)


def _rwprep_kernel(h_ref, gm_ref, w_ref, mu_ref, w0_ref, w2_ref, a0_ref, a2_ref, g2_ref,
                   kk_ref, ka_ref, rk_ref, ones_ref,
                   r_out, lw_out, cum_out, k_out, v_out, a_out, b_out, g_out, bonus_out, last_ref, *, tiles_per_seq):
    i = pl.program_id(0)
    p = jnp.dot(_rms(h_ref[...], gm_ref[...]).astype(BF16), w_ref[...], preferred_element_type=F32)
    tm = p.shape[0]

    @pl.when(i % tiles_per_seq == 0)
    def _():
        last_ref[...] = jnp.zeros_like(last_ref)

    prev_row = last_ref[0:1, :]
    last_ref[0:1, :] = p[tm - 1:tm, :]
    rows = lax.broadcasted_iota(jnp.int32, p.shape, 0)
    shifted = jnp.where(rows == 0, prev_row, pltpu.roll(p, 1, 0))
    pm = p + (shifted - p) * mu_ref[...]

    W = RW_WIDTH
    r, k, v = pm[:, 0:W], pm[:, W:2 * W], pm[:, 2 * W:3 * W]
    wa = pm[:, 3 * W:3 * W + RW_LORA_PAD]
    gd = pm[:, 3 * W + RW_LORA_PAD:]

    w_in = w0_ref[...] + _dot3(jnp.tanh(wa), w2_ref[...])
    lw = -math.exp(-0.5) * jax.nn.sigmoid(w_in)
    a_sig = jax.nn.sigmoid(a0_ref[...] + _dot3(wa, a2_ref[...]))

    C = RW_CHUNK
    ci = lax.broadcasted_iota(jnp.int32, (C, C), 0)
    cj = lax.broadcasted_iota(jnp.int32, (C, C), 1)
    tri = jnp.where(cj <= ci, 1.0, 0.0).astype(BF16)
    l1 = lw.astype(BF16)
    rem = lw - l1.astype(F32)
    l2 = rem.astype(BF16)
    l3 = (rem - l2.astype(F32)).astype(BF16)
    td = lambda x: jnp.dot(tri, x, preferred_element_type=F32)
    cum = jnp.concatenate([td(l1[c0:c0 + C]) + td(l2[c0:c0 + C]) + td(l3[c0:c0 + C])
                           for c0 in range(0, tm, C)], axis=0)
    g = _dot(jax.nn.sigmoid(gd), g2_ref[...])

    ones = ones_ref[...]
    kk = k * kk_ref[...]
    ss = _dot(kk * kk, ones)
    kk = kk * lax.rsqrt(jnp.maximum(ss, 1e-24))
    k2 = k * (1.0 + (a_sig - 1.0) * ka_ref[...])
    bonus = _dot(r * k2 * rk_ref[...], ones) * v

    g_out[...] = g
    bonus_out[...] = bonus
    a_vec = -kk
    b_vec = kk * a_sig
    for h in range(RW_HEADS):
        sl = slice(h * RW_HEAD_DIM, (h + 1) * RW_HEAD_DIM)
        r_out[h] = r[:, sl]
        lw_out[h] = lw[:, sl]
        cum_out[h] = cum[:, sl]
        k_out[h] = k2[:, sl]
        v_out[h] = v[:, sl]
        a_out[h] = a_vec[:, sl]
        b_out[h] = b_vec[:, sl]


def _rwkv_prep(h, g_mix, w_rw, B, T, mu, w0, w2p, a0, a2p, g2p, k_k, k_a, r_k, ones, tm=TILE_RWKV_PREP):
    M, D = h.shape
    tps = T // tm
    row = lambda x: x.reshape(1, -1)
    full = lambda shape: pl.BlockSpec(shape, lambda i: (0,) * len(shape))
    head_spec = pl.BlockSpec((None, RW_HEADS, tm, RW_HEAD_DIM), lambda i: (i // tps, 0, i % tps, 0))
    head_shape = jax.ShapeDtypeStruct((B, RW_HEADS, T, RW_HEAD_DIM), F32)
    tok_spec = pl.BlockSpec((tm, RW_WIDTH), lambda i: (i, 0))
    tok_shape = jax.ShapeDtypeStruct((M, RW_WIDTH), F32)
    return pl.pallas_call(
        functools.partial(_rwprep_kernel, tiles_per_seq=tps),
        grid=(M // tm,),
        in_specs=[
            pl.BlockSpec((tm, D), lambda i: (i, 0)), full((1, D)), full((D, RW_PAD_COLS)),
            full((1, RW_PAD_COLS)), full((1, RW_WIDTH)), full((RW_LORA_PAD, RW_WIDTH)),
            full((1, RW_WIDTH)), full((RW_LORA_PAD, RW_WIDTH)), full((RW_GATE_PAD, RW_WIDTH)),
            full((1, RW_WIDTH)), full((1, RW_WIDTH)), full((1, RW_WIDTH)), full((RW_WIDTH, RW_WIDTH)),
        ],
        out_specs=[head_spec] * 7 + [tok_spec] * 2,
        out_shape=[head_shape] * 7 + [tok_shape] * 2,
        scratch_shapes=[pltpu.VMEM((8, RW_PAD_COLS), F32)],
        compiler_params=_cparams("arbitrary"),
        name="rwkv_prep",
    )(h, row(g_mix), w_rw, row(mu), row(w0), w2p, row(a0), a2p, g2p, row(k_k), row(k_a), row(r_k), ones)


def _bmm(a, b, nt=False):
    spec = 'gik,gjk->gij' if nt else 'gij,gjk->gik'
    return jnp.einsum(spec, a.astype(BF16), b.astype(BF16), preferred_element_type=F32)


def _rwkv_kernel(r_ref, lw_ref, cum_ref, k_ref, v_ref, a_ref, b_ref, bonus_ref, g_ref, lnw_ref, lnb_ref, avg_ref,
                 y_ref, st_ref, ytok_ref):
    C = RW_CHUNK
    H = RW_HEADS
    Tc = r_ref.shape[1]
    nc = Tc // C
    G = H * nc

    @pl.when(pl.program_id(1) == 0)
    def _():
        st_ref[...] = jnp.zeros_like(st_ref)

    shp = lambda ref: ref[...].reshape(G, C, RW_HEAD_DIM)
    r, lw, L, k, v, a, b = (shp(x) for x in (r_ref, lw_ref, cum_ref, k_ref, v_ref, a_ref, b_ref))

    ti = lax.broadcasted_iota(jnp.int32, (C, C), 0)
    tj = lax.broadcasted_iota(jnp.int32, (C, C), 1)
    eye = jnp.where(ti == tj, 1.0, 0.0).astype(F32)
    wi = lax.broadcasted_iota(jnp.int32, (C, 2 * C), 0)
    wj = lax.broadcasted_iota(jnp.int32, (C, 2 * C), 1)
    second = wj >= C
    wjc = jnp.where(second, wj - C, wj)

    Lprev = L - lw
    Ltot = L[:, C - 1:C, :]

    eL = jnp.exp(L)
    enL = jnp.exp(-L)
    eh = jnp.exp(Ltot - L)
    At = a * jnp.exp(Lprev)
    Rt = r * eL
    BKt = jnp.concatenate([b * enL, k * enL], axis=1)
    BKh = jnp.concatenate([b * eh, k * eh], axis=1)

    P4 = _bmm(jnp.concatenate([At, Rt], axis=1), BKt, nt=True)
    top, bot = P4[:, :C, :], P4[:, C:, :]
    N = jnp.where((tj < ti)[None], top[:, :, :C], 0.0)
    AKz = jnp.where((second & (wjc < wi))[None], top, 0.0)
    RBK = jnp.where((wjc <= wi)[None], bot, 0.0)
    RB = RBK[:, :, :C]

    def siblings(s):
        return ((ti // (2 * s)) == (tj // (2 * s))) & ((ti // s) != (tj // s))

    X = eye[None] + jnp.where(siblings(1)[None], N, 0.0)
    s_blk = 2
    while s_blk < C:
        XE = _bmm(X, jnp.where(siblings(s_blk)[None], N, 0.0))
        X = X + _bmm(XE, X)
        s_blk *= 2

    zv = jnp.concatenate([jnp.zeros_like(v), v], axis=1)
    Abar = _bmm(X, At)
    W0 = _bmm(X, _bmm(AKz, zv))
    wv = jnp.concatenate([W0, v], axis=1)
    Y0 = _bmm(RBK, wv)
    Rbar = Rt + _bmm(RB, Abar)
    BKhT = jnp.swapaxes(BKh, 1, 2)
    Mtx = _bmm(BKhT[:, :, :C], Abar) + eye[None] * jnp.exp(Ltot)
    G0 = _bmm(BKhT, wv)

    hsplit = lambda x: x.reshape(H, nc, x.shape[1], x.shape[2])
    RM = hsplit(jnp.concatenate([Rbar, Mtx], axis=1))
    Y0, G0 = hsplit(Y0), hsplit(G0)
    St = st_ref[...]
    ys = []
    for c in range(nc):
        both = _bmm(RM[:, c], St)
        ys.append(both[:, :C] + Y0[:, c])
        St = both[:, C:] + G0[:, c]
    st_ref[...] = St
    y = jnp.concatenate(ys, axis=1)
    for h in range(H):
        ytok_ref[:, h * RW_HEAD_DIM:(h + 1) * RW_HEAD_DIM] = y[h]

    y = ytok_ref[...]
    avg = avg_ref[...]
    mean = _dot(y, avg)
    yc = y - mean
    var = _dot(yc * yc, avg)
    ya = (yc * lax.rsqrt(var + RW_GN_EPS) * lnw_ref[...] + lnb_ref[...] + bonus_ref[...]) * g_ref[...]
    y_ref[...] = ya.astype(BF16)


def _rwkv_recurrence(r, lw, cum, k, v, a, b, bonus, g, ln_w, ln_b, avg, tc=TILE_RWKV_SCAN):
    B, H, T, Dh = r.shape
    W = H * Dh
    nt = T // tc
    spec = pl.BlockSpec((None, H, tc, Dh), lambda bi, c: (bi, 0, c, 0))
    tok = pl.BlockSpec((tc, W), lambda bi, c: (bi * nt + c, 0))
    const = lambda shape: pl.BlockSpec(shape, lambda bi, c: (0, 0))
    return pl.pallas_call(
        _rwkv_kernel,
        grid=(B, nt),
        in_specs=[spec] * 7 + [tok, tok, const((1, W)), const((1, W)), const((W, W))],
        out_specs=tok,
        out_shape=jax.ShapeDtypeStruct((B * T, W), BF16),
        scratch_shapes=[pltpu.VMEM((H, Dh, Dh), F32), pltpu.VMEM((tc, W), F32)],
        compiler_params=_cparams("parallel", "arbitrary"),
        name="rwkv_recurrence",
    )(r, lw, cum, k, v, a, b, bonus, g, ln_w.reshape(1, W), ln_b.reshape(1, W), avg)


def _rope_tables(T, heads, head_dim, rot_dim, base):
    half = rot_dim // 2
    inv_freq = base ** (-jnp.arange(half, dtype=F32) / half)
    ang = jnp.arange(T).astype(F32)[:, None] * inv_freq[None, :]
    cos, sin = jnp.cos(ang), jnp.sin(ang)
    rest = head_dim - rot_dim
    c = jnp.concatenate([cos, cos, jnp.ones((T, rest), F32)], axis=1)
    s_lo = jnp.concatenate([-sin, jnp.zeros((T, half + rest), F32)], axis=1)
    s_hi = jnp.concatenate([jnp.zeros((T, half), F32), sin, jnp.zeros((T, rest), F32)], axis=1)
    tile = lambda x: jnp.tile(x, (1, heads))
    return tile(c), tile(s_lo), tile(s_hi)


def _apply_rope(x, c, s_lo, s_hi, half):
    out = []
    for gl in range(x.shape[-1] // LANES):
        ln = slice(gl * LANES, (gl + 1) * LANES)
        xg = x[:, ln]
        out.append(xg * c[:, ln] + pltpu.roll(xg, LANES - half, 1) * s_lo[:, ln] + pltpu.roll(xg, half, 1) * s_hi[:, ln])
    return jnp.concatenate(out, axis=-1)


def _dilproj_kernel(h_ref, g_ref, w_ref, c_ref, slo_ref, shi_ref, gq_ref, gk_ref, avg_ref,
                    o0_ref, o1_ref, o2_ref, tmp_ref):
    u = _rms(h_ref[...], g_ref[...]).astype(BF16)
    c, s_lo, s_hi = c_ref[...], slo_ref[...], shi_ref[...]
    W = DIL_WIDTH
    for grp, o_ref in enumerate((o0_ref, o1_ref, o2_ref)):
        dilation = DIL_PATTERNS[grp][1]
        rows = u.shape[0] // dilation
        for which in range(3):
            col = (3 * grp + which) * W
            val = jnp.dot(u, w_ref[:, col:col + W], preferred_element_type=F32)
            if which < 2:
                gain = (gq_ref if which == 0 else gk_ref)[grp:grp + 1, :]
                ms = jnp.dot((val * val).astype(BF16), avg_ref[...], preferred_element_type=F32)
                val = _apply_rope(val * lax.rsqrt(ms + EPS) * gain, c, s_lo, s_hi, ROPE_DIM // 2)
            if dilation == 1:
                o_ref[which, 0] = val.astype(BF16)
            else:
                _store_lane_groups(tmp_ref, val)
                for res in range(dilation):
                    for gl in range(tmp_ref.shape[0]):
                        piece = tmp_ref[gl, pl.ds(res, rows, stride=dilation), :]
                        o_ref[which, res, :, gl * LANES:(gl + 1) * LANES] = piece.astype(BF16)


def _dilated_projection(h, g_mix, w_dil, B, T, tabs, gq, gk, avg, tm=TILE_DIL_PROJ):
    M, D = h.shape
    tps = T // tm
    W = DIL_WIDTH
    const = lambda shape: pl.BlockSpec(shape, lambda i: (0,) * len(shape))
    tspec = pl.BlockSpec((tm, W), lambda i: (i % tps, 0))
    out_specs, out_shape = [], []
    for _, d in DIL_PATTERNS:
        out_specs.append(pl.BlockSpec((3, None, d, tm // d, W), lambda i: (0, i // tps, 0, i % tps, 0)))
        out_shape.append(jax.ShapeDtypeStruct((3, B, d, T // d, W), BF16))
    return pl.pallas_call(
        _dilproj_kernel,
        grid=(M // tm,),
        in_specs=[pl.BlockSpec((tm, D), lambda i: (i, 0)), const((1, D)), const((D, 3 * N_DIL * W)),
                  tspec, tspec, tspec, const((N_DIL, W)), const((N_DIL, W)), const((W, W))],
        out_specs=out_specs,
        out_shape=out_shape,
        scratch_shapes=[pltpu.VMEM((W // LANES, tm, LANES), F32)],
        compiler_params=_cparams("parallel"),
        name="dilated_projection",
    )(h, g_mix.reshape(1, D), w_dil, *tabs, jnp.tile(gq * Q_PRESCALE, (1, DIL_HEADS)), jnp.tile(gk, (1, DIL_HEADS)), avg)


def _dilattn_kernel(q_ref, k_ref, v_ref, kprev_ref, vprev_ref, o_ref, lse_ref, *, blocks):
    i = pl.program_id(2)
    Lb = DIL_BLOCK
    PW = 2 * HEAD_DIM
    NP = DIL_WIDTH // PW
    nres = q_ref.shape[0]
    lane = lax.broadcasted_iota(jnp.int32, (Lb, PW), 1)
    low = lane < HEAD_DIM

    qs, ks, vs = [], [], []
    for blk in range(blocks):
        rows = slice(blk * Lb, (blk + 1) * Lb)
        for rr in range(nres):
            for p in range(NP):
                ln = slice(p * PW, (p + 1) * PW)
                q2 = q_ref[rr, rows, ln]
                zero = jnp.zeros_like(q2)
                qs.append(jnp.concatenate([jnp.where(low, q2, zero), jnp.where(low, zero, q2)], axis=0))
                if blk == 0:
                    ks.append(jnp.concatenate([kprev_ref[rr, :, ln], k_ref[rr, rows, ln]], axis=0))
                    vs.append(jnp.concatenate([vprev_ref[rr, :, ln], v_ref[rr, rows, ln]], axis=0))
                else:
                    both = slice((blk - 1) * Lb, (blk + 1) * Lb)
                    ks.append(k_ref[rr, both, ln])
                    vs.append(v_ref[rr, both, ln])
    Q, K, V = jnp.stack(qs), jnp.stack(ks), jnp.stack(vs)
    first = nres * NP

    s = jnp.einsum('gqd,gkd->gqk', Q, K, preferred_element_type=F32)
    qi = lax.broadcasted_iota(jnp.int32, (2 * Lb, 2 * Lb), 0) % Lb
    kj = lax.broadcasted_iota(jnp.int32, (2 * Lb, 2 * Lb), 1)
    window = (kj >= qi) & (kj <= qi + Lb)
    window0 = window & ((kj >= Lb) | (i > 0))
    s = jnp.concatenate([jnp.where(window0[None], s[:first], -jnp.inf),
                         jnp.where(window[None], s[first:], -jnp.inf)], axis=0) if blocks > 1 else \
        jnp.where(window0[None], s, -jnp.inf)
    m = jnp.max(s, axis=-1, keepdims=True)
    e = jnp.exp2(s - m).astype(BF16)
    vx = jnp.concatenate([V, jnp.ones_like(V)], axis=-1)
    nd = jnp.einsum('gqk,gkd->gqd', e, vx, preferred_element_type=F32)
    num, den = nd[:, :, :PW], nd[:, :, PW:]
    o2 = num / den
    l2 = m * math.log(2.0) + jnp.log(den)
    head_of_lane = lane // LSE_LANES
    for blk in range(blocks):
        rows = slice(blk * Lb, (blk + 1) * Lb)
        for rr in range(nres):
            lse_c = jnp.zeros((Lb, PW), F32)
            for p in range(NP):
                g = (blk * nres + rr) * NP + p
                ln = slice(p * PW, (p + 1) * PW)
                o_ref[rr, rows, ln] = jnp.where(low, o2[g, :Lb], o2[g, Lb:]).astype(BF16)
                lse_c = jnp.where(head_of_lane == 2 * p, l2[g, :Lb],
                                  jnp.where(head_of_lane == 2 * p + 1, l2[g, Lb:], lse_c))
            lse_ref[rr, rows, :] = lse_c


def _dilated_attention(qkv):
    _, B, d, Mr, W = qkv.shape
    Lb = DIL_BLOCK
    nb = Mr // Lb
    blocks = min(DIL_BLOCKS_PER_STEP, nb)
    nres = min(DIL_BLOCKS_PER_STEP // blocks, d)
    tile = blocks * Lb

    def cur(which):
        return pl.BlockSpec((None, None, nres, tile, W), lambda b, c, i: (which, b, c, i, 0))

    def prev(which):
        return pl.BlockSpec((None, None, nres, Lb, W),
                            lambda b, c, i: (which, b, c, jnp.maximum(i * blocks - 1, 0), 0))

    ospec = pl.BlockSpec((None, nres, tile, W), lambda b, c, i: (b, c, i, 0))
    lspec = pl.BlockSpec((None, nres, tile, LANES), lambda b, c, i: (b, c, i, 0))
    return pl.pallas_call(
        functools.partial(_dilattn_kernel, blocks=blocks),
        grid=(B, d // nres, nb // blocks),
        in_specs=[cur(0), cur(1), cur(2), prev(1), prev(2)],
        out_specs=[ospec, lspec],
        out_shape=[jax.ShapeDtypeStruct((B, d, Mr, W), BF16), jax.ShapeDtypeStruct((B, d, Mr, LANES), F32)],
        compiler_params=_cparams("parallel", "parallel", "parallel"),
        name=f"dilated_attention_d{d}",
    )(qkv, qkv, qkv, qkv, qkv)


def _ret_kernel(h_ref, gm_ref, w_ref, c_ref, slo_ref, shi_ref, gain_ref, o_ref, st_ref):
    C = RET_CHUNK
    QK = RET_HEADS * RET_QK_DIM

    @pl.when(pl.program_id(1) == 0)
    def _():
        st_ref[...] = jnp.zeros_like(st_ref)

    u = _rms(h_ref[...], gm_ref[...]).astype(BF16)
    p = jnp.dot(u, w_ref[...], preferred_element_type=F32)
    c, s_lo, s_hi = c_ref[...], slo_ref[...], shi_ref[...]
    q = _apply_rope(p[:, :QK], c, s_lo, s_hi, RET_QK_DIM // 2)
    k = _apply_rope(p[:, QK:2 * QK], c, s_lo, s_hi, RET_QK_DIM // 2) * (RET_QK_DIM ** -0.5)
    v = p[:, 2 * QK:2 * QK + RET_WIDTH]
    g = p[:, 2 * QK + RET_WIDTH:]
    gain = gain_ref[...]

    ji = lax.broadcasted_iota(jnp.int32, (C, C), 0)
    jj = lax.broadcasted_iota(jnp.int32, (C, C), 1)
    diff = (ji - jj).astype(F32)
    jcol = lax.broadcasted_iota(jnp.int32, (C, 1), 0).astype(F32)

    for h in range(RET_HEADS):
        lg = math.log(1.0 - 2.0 ** (-5.0 - h))
        qs = slice(h * RET_QK_DIM, (h + 1) * RET_QK_DIM)
        vs = slice(h * RET_V_DIM, (h + 1) * RET_V_DIM)
        decay_in = jnp.where(diff >= 0, jnp.exp(lg * jnp.maximum(diff, 0.0)), 0.0)
        q_dec = jnp.exp(lg * (jcol + 1.0))
        k_dec = jnp.exp(lg * (C - 1.0 - jcol))
        S = st_ref[h]
        for ch in range(p.shape[0] // C):
            rows = slice(ch * C, (ch + 1) * C)
            qh, kh, vh = q[rows, qs], k[rows, qs], v[rows, vs]
            s = lax.dot_general(qh.astype(BF16), kh.astype(BF16), (((1,), (1,)), ((), ())),
                                preferred_element_type=F32) * decay_in
            y = _dot(s, vh) + _dot(qh * q_dec, S)
            kv = lax.dot_general((kh * k_dec).astype(BF16), vh.astype(BF16), (((0,), (0,)), ((), ())),
                                 preferred_element_type=F32)
            S = math.exp(lg * C) * S + kv
            yn = y * lax.rsqrt(jnp.mean(y * y, axis=-1, keepdims=True) + EPS) * gain[:, vs]
            gh = g[rows, vs]
            o_ref[rows, vs] = (gh * jax.nn.sigmoid(gh) * yn).astype(BF16)
        st_ref[h] = S


def _retention(h, g_mix, w_ret, B, T, tabs, gain, tr=TILE_RETENTION):
    M, D = h.shape
    nt = T // tr
    qk = RET_HEADS * RET_QK_DIM
    const = lambda shape: pl.BlockSpec(shape, lambda b, t: (0,) * len(shape))
    tspec = pl.BlockSpec((tr, qk), lambda b, t: (t, 0))
    return pl.pallas_call(
        _ret_kernel,
        grid=(B, nt),
        in_specs=[pl.BlockSpec((tr, D), lambda b, t: (b * nt + t, 0)), const((1, D)), const(w_ret.shape),
                  tspec, tspec, tspec, const((1, RET_WIDTH))],
        out_specs=pl.BlockSpec((tr, RET_WIDTH), lambda b, t: (b * nt + t, 0)),
        out_shape=jax.ShapeDtypeStruct((M, RET_WIDTH), BF16),
        scratch_shapes=[pltpu.VMEM((RET_HEADS, RET_QK_DIM, RET_V_DIM), F32)],
        compiler_params=_cparams("parallel", "arbitrary"),
        name="retention",
    )(h, g_mix.reshape(1, D), w_ret, *tabs, gain.reshape(1, -1))


def _merge_kernel(h_ref, gm_ref, wg_ref, ya_ref, o0_ref, o1_ref, o2_ref, l0_ref, l1_ref, l2_ref, yc_ref,
                  wa_ref, wb_ref, wc_ref, wo_ref, expand_ref, out_ref, o1_scr, o2_scr, l1_scr, l2_scr):
    h = h_ref[...]
    u = _rms(h, gm_ref[...]).astype(BF16)

    def token_order(ref, scr):
        dil, rows = ref.shape[0], ref.shape[1]
        for c in range(dil):
            blk = ref[c].astype(F32)
            for gl in range(scr.shape[0]):
                scr[gl, pl.ds(c, rows, stride=dil), :] = blk[:, gl * LANES:(gl + 1) * LANES]
        return jnp.concatenate([scr[gl] for gl in range(scr.shape[0])], axis=-1)

    o0, l0 = o0_ref[0].astype(F32), l0_ref[0]
    o1, l1 = token_order(o1_ref, o1_scr), token_order(l1_ref, l1_scr)
    o2, l2 = token_order(o2_ref, o2_scr), token_order(l2_ref, l2_scr)

    mx = jnp.maximum(jnp.maximum(l0, l1), l2)
    e0, e1, e2 = jnp.exp(l0 - mx), jnp.exp(l1 - mx), jnp.exp(l2 - mx)
    inv = 1.0 / (e0 + e1 + e2)

    def spread(w):
        hi, lo = _split2(w)
        return (jnp.dot(hi, expand_ref[...], preferred_element_type=F32)
                + jnp.dot(lo, expand_ref[...], preferred_element_type=F32))

    yb = spread(e0 * inv) * o0 + spread(e1 * inv) * o1 + spread(e2 * inv) * o2

    D = D_MODEL
    gate = lambda i: jax.nn.sigmoid(jnp.dot(u, wg_ref[:, i * D:(i + 1) * D], preferred_element_type=F32))
    merged = (gate(0) * jnp.dot(ya_ref[...], wa_ref[...], preferred_element_type=F32)
              + gate(1) * _dot(yb, wb_ref[...])
              + gate(2) * jnp.dot(yc_ref[...], wc_ref[...], preferred_element_type=F32))
    out_ref[...] = h + _dot(merged, wo_ref[...])


def _merge(h, T, g_mix, w_gate, y_a, o, lse, y_c, wa, wb, wc, wo, tm=TILE_MERGE):
    M, D = h.shape
    tps = T // tm
    tok = lambda w: pl.BlockSpec((tm, w), lambda i: (i, 0))
    full = lambda shape: pl.BlockSpec(shape, lambda i: (0, 0))
    W = RW_WIDTH

    def res(group, width):
        d = DIL_PATTERNS[group][1]
        return pl.BlockSpec((None, d, tm // d, width), lambda i: (i // tps, 0, i % tps, 0))

    li = lax.broadcasted_iota(jnp.int32, (LANES, W), 0)
    lj = lax.broadcasted_iota(jnp.int32, (LANES, W), 1)
    expand = jnp.where(li == (lj // HEAD_DIM) * LSE_LANES, 1.0, 0.0).astype(BF16)

    return pl.pallas_call(
        _merge_kernel,
        grid=(M // tm,),
        in_specs=[tok(D), full((1, D)), full((D, 3 * D)), tok(W),
                  res(0, W), res(1, W), res(2, W), res(0, LANES), res(1, LANES), res(2, LANES), tok(W),
                  full((W, D)), full((W, D)), full((W, D)), full((D, D)), full((LANES, W))],
        out_specs=tok(D),
        out_shape=jax.ShapeDtypeStruct((M, D), F32),
        scratch_shapes=[pltpu.VMEM((W // LANES, tm, LANES), F32)] * 2 + [pltpu.VMEM((1, tm, LANES), F32)] * 2,
        compiler_params=_cparams("parallel"),
        name="gated_merge",
    )(h, g_mix.reshape(1, D), w_gate, y_a, o[0], o[1], o[2], lse[0], lse[1], lse[2], y_c, wa, wb, wc, wo, expand)


def _xattn_kernel(h_ref, gx_ref, wq_ref, kv_ref, qn_ref, kn_ref, wo_ref, out_ref, k_scr, v_scr):
    D = D_MODEL

    @pl.when(pl.program_id(1) == 0)
    def _():
        kv = kv_ref[...]
        for hd in range(XA_HEADS):
            sl = slice(hd * XA_HEAD_DIM, (hd + 1) * XA_HEAD_DIM)
            k_scr[:, sl] = _rms(kv[:, sl], kn_ref[...]).astype(BF16)
        v_scr[...] = kv[:, D:].astype(BF16)

    h = h_ref[...]
    hn = _rms(h, gx_ref[...]).astype(BF16)
    q = jnp.dot(hn, wq_ref[...], preferred_element_type=F32)
    outs = []
    for hd in range(XA_HEADS):
        sl = slice(hd * XA_HEAD_DIM, (hd + 1) * XA_HEAD_DIM)
        qh = _rms(q[:, sl], qn_ref[...]).astype(BF16)
        s = lax.dot_general(qh, k_scr[:, sl], (((1,), (1,)), ((), ())), preferred_element_type=F32)
        m = jnp.max(s, axis=-1, keepdims=True)
        e = jnp.exp2(s - m)
        pr = e / jnp.sum(e, axis=-1, keepdims=True)
        outs.append(jnp.dot(pr.astype(BF16), v_scr[:, sl], preferred_element_type=F32))
    o = jnp.concatenate(outs, axis=-1).astype(BF16)
    out_ref[...] = h + jnp.dot(o, wo_ref[...], preferred_element_type=F32)


def _cross_attention(h, B, T, g_x, wq, kv, q_norm, k_norm, wo, tm=TILE_XATTN):
    M, D = h.shape
    tps = T // tm
    full = lambda shape: pl.BlockSpec(shape, lambda b, t: (0, 0))
    return pl.pallas_call(
        _xattn_kernel,
        grid=(B, tps),
        in_specs=[
            pl.BlockSpec((tm, D), lambda b, t: (b * tps + t, 0)),
            full((1, D)), full((D, D)),
            pl.BlockSpec((MEM_LEN, 2 * D), lambda b, t: (b, 0)),
            full((1, XA_HEAD_DIM)), full((1, XA_HEAD_DIM)), full((D, D)),
        ],
        out_specs=pl.BlockSpec((tm, D), lambda b, t: (b * tps + t, 0)),
        out_shape=jax.ShapeDtypeStruct((M, D), F32),
        scratch_shapes=[pltpu.VMEM((MEM_LEN, D), BF16), pltpu.VMEM((MEM_LEN, D), BF16)],
        compiler_params=_cparams("parallel", "arbitrary"),
        name="cross_attention",
    )(h, g_x.reshape(1, D), wq, kv, (q_norm * (XA_HEAD_DIM ** -0.5 * math.log2(math.e))).reshape(1, -1),
      k_norm.reshape(1, -1), wo)


def _pad_rows(w, rows_before, total):
    return jnp.pad(w, ((rows_before, total - rows_before - w.shape[0]), (0, 0)))


def _layer(h, mem2, B, T, p, dil_tabs, ret_tabs):
    bf = lambda w: w.astype(BF16)
    ones = _block_ones(RW_WIDTH, RW_HEAD_DIM)
    avg = ones * (1.0 / RW_HEAD_DIM)
    h = _ffn(h, p['norm_ffn1'], bf(p['ffn1_w13']), bf(p['ffn1_w2']))

    w_in = p['w_in']
    c_dil = RW_IN_COLS
    c_ret = c_dil + 3 * N_DIL * DIL_WIDTH
    c_gate = c_ret + 2 * RET_HEADS * RET_QK_DIM + 2 * RET_WIDTH
    rw_w = jnp.pad(w_in[:, :c_dil], ((0, 0), (0, RW_PAD_COLS - RW_IN_COLS)))
    mu = jnp.pad(p['rw_mu'], (0, RW_PAD_COLS - RW_IN_COLS))

    w2p = _pad_rows(p['rw_w2'], 0, RW_LORA_PAD)
    a2p = _pad_rows(p['rw_a2'], RW_LORA, RW_LORA_PAD)
    g2p = _pad_rows(p['rw_g2'], 0, RW_GATE_PAD)
    r, lw, cum, k, v, a, b, g, bonus = _rwkv_prep(h, p['norm_mix'], bf(rw_w), B, T, mu, p['rw_w0'], w2p, p['rw_a0'], a2p, g2p,
                                             p['rw_k_k'], p['rw_k_a'], p['rw_r_k'], ones)
    y_a = _rwkv_recurrence(r, lw, cum, k, v, a, b, bonus, g, p['rw_ln_w'], p['rw_ln_b'], avg)

    qkvs = _dilated_projection(h, p['norm_mix'], bf(w_in[:, c_dil:c_ret]), B, T, dil_tabs,
                               p['dil_q_norm'], p['dil_k_norm'], avg)
    o, lse = zip(*[_dilated_attention(qkv) for qkv in qkvs])

    y_c = _retention(h, p['norm_mix'], bf(w_in[:, c_ret:c_gate]), B, T, ret_tabs, p['ret_norm'])

    h = _merge(h, T, p['norm_mix'], bf(w_in[:, c_gate:]), y_a, o, lse, y_c,
               bf(p['w_branch_rwkv']), bf(p['w_branch_dil']), bf(p['w_branch_ret']), bf(p['w_out']))

    kv = _norm_matmul(mem2, p['norm_mem'], bf(p['xa_wkv']), MEM_LEN, 1024)
    h = _cross_attention(h, B, T, p['norm_xattn'], bf(p['xa_wq']), kv, p['xa_q_norm'], p['xa_k_norm'],
                         bf(p['xa_wo']))
    h = _ffn(h, p['norm_ffn2'], bf(p['ffn2_w13']), bf(p['ffn2_w2']))
    return h


_PARAM_NAMES = ('norm_ffn1', 'ffn1_w13', 'ffn1_w2', 'norm_mix', 'w_in', 'rw_mu', 'rw_w0', 'rw_w2', 'rw_a0',
                'rw_a2', 'rw_g2', 'rw_k_k', 'rw_k_a', 'rw_r_k', 'rw_ln_w', 'rw_ln_b', 'dil_q_norm',
                'dil_k_norm', 'ret_norm', 'w_branch_rwkv', 'w_branch_dil', 'w_branch_ret', 'w_out',
                'norm_xattn', 'norm_mem', 'xa_wq', 'xa_wkv', 'xa_q_norm', 'xa_k_norm', 'xa_wo',
                'norm_ffn2', 'ffn2_w13', 'ffn2_w2')


def kernel(x, mem, norm_ffn1, ffn1_w13, ffn1_w2, norm_mix, w_in, rw_mu, rw_w0, rw_w2, rw_a0, rw_a2, rw_g2, rw_k_k, rw_k_a, rw_r_k, rw_ln_w, rw_ln_b, dil_q_norm, dil_k_norm, ret_norm, w_branch_rwkv, w_branch_dil, w_branch_ret, w_out, norm_xattn, norm_mem, xa_wq, xa_wkv, xa_q_norm, xa_k_norm, xa_wo, norm_ffn2, ffn2_w13, ffn2_w2):
    params = dict(zip(_PARAM_NAMES, (norm_ffn1, ffn1_w13, ffn1_w2, norm_mix, w_in, rw_mu, rw_w0, rw_w2, rw_a0,
                                     rw_a2, rw_g2, rw_k_k, rw_k_a, rw_r_k, rw_ln_w, rw_ln_b, dil_q_norm,
                                     dil_k_norm, ret_norm, w_branch_rwkv, w_branch_dil, w_branch_ret, w_out,
                                     norm_xattn, norm_mem, xa_wq, xa_wkv, xa_q_norm, xa_k_norm, xa_wo,
                                     norm_ffn2, ffn2_w13, ffn2_w2)))
    B, T, D = x.shape
    assert D == D_MODEL and T % (DIL_BLOCK * DIL_PATTERNS[-1][1]) == 0 and mem.shape[1] == MEM_LEN
    depth = norm_ffn1.shape[0]
    dil_tabs = _rope_tables(T, DIL_HEADS, HEAD_DIM, ROPE_DIM, ROPE_THETA)
    ret_tabs = _rope_tables(T, RET_HEADS, RET_QK_DIM, RET_QK_DIM, RET_ROPE_BASE)
    h = x.reshape(B * T, D)
    mem2 = mem.reshape(B * MEM_LEN, D)
    for l in range(depth):
        h = _layer(h, mem2, B, T, {n: params[n][l] for n in _PARAM_NAMES}, dil_tabs, ret_tabs)
    return h.reshape(B, T, D)
```

```python
import functools
import math

import jax
import jax.numpy as jnp
from jax import lax
from jax.experimental import pallas as pl
from jax.experimental.pallas import tpu as pltpu

F32 = jnp.float32
BF16 = jnp.bfloat16

D_MODEL = 1024
D_FF = 2816
EPS = 1e-6

RW_HEADS = 8
RW_HEAD_DIM = 64
RW_WIDTH = 512
RW_GN_EPS = 64e-5
RW_CHUNK = 64
RW_LORA = 64
RW_GATE_LORA = 160
RW_IN_COLS = 3 * RW_WIDTH + 2 * RW_LORA + RW_GATE_LORA
RW_LORA_PAD = 128
RW_GATE_PAD = 256
RW_PAD_COLS = 3 * RW_WIDTH + RW_LORA_PAD + RW_GATE_PAD

DIL_PATTERNS = ((128, 1), (512, 4), (2048, 16))
N_DIL = 3
DIL_HEADS = 8
HEAD_DIM = 64
DIL_WIDTH = 512
DIL_BLOCK = 128
DIL_BLOCKS_PER_STEP = 8
LSE_LANES = 16
ROPE_THETA = 500000.0
Q_PRESCALE = HEAD_DIM ** -0.5 * math.log2(math.e)
ROPE_DIM = 16

RET_HEADS = 4
RET_QK_DIM = 64
RET_V_DIM = 128
RET_CHUNK = 128
RET_ROPE_BASE = 10000.0
RET_WIDTH = 512

XA_HEADS = 4
XA_HEAD_DIM = 256
MEM_LEN = 256

V7X_VMEM_BYTES = 64 * 1024 * 1024
VMEM_LIMIT = V7X_VMEM_BYTES - 8 * 1024 * 1024
LANES = 128

TILE_FFN = 1024
TILE_FFN_COLS = 1408
TILE_RWKV_PREP = 512
TILE_RWKV_SCAN = 512
TILE_DIL_PROJ = 512
TILE_RETENTION = 1024
TILE_MERGE = 512
TILE_XATTN = 1024


def _cparams(*sem):
    return pltpu.CompilerParams(dimension_semantics=sem, vmem_limit_bytes=VMEM_LIMIT)


def _dot(a, b):
    return jnp.dot(a.astype(BF16), b.astype(BF16), preferred_element_type=F32)


def _split2(x):
    hi = x.astype(BF16)
    lo = (x - hi.astype(F32)).astype(BF16)
    return hi, lo


def _dot3(a, b):
    ah, al = _split2(a)
    bh, bl = _split2(b)
    return (jnp.dot(ah, bh, preferred_element_type=F32)
            + jnp.dot(ah, bl, preferred_element_type=F32)
            + jnp.dot(al, bh, preferred_element_type=F32))


def _rms(x, g):
    return x * lax.rsqrt(jnp.mean(x * x, axis=-1, keepdims=True) + EPS) * g


def _store_lane_groups(ref, x):
    for gl in range(ref.shape[0]):
        ref[gl] = x[:, gl * LANES:(gl + 1) * LANES]


def _block_ones(n, width):
    i = lax.broadcasted_iota(jnp.int32, (n, n), 0) // width
    j = lax.broadcasted_iota(jnp.int32, (n, n), 1) // width
    return jnp.where(i == j, 1.0, 0.0).astype(BF16)


def _ffn_kernel(h_ref, g_ref, w1_ref, w3_ref, w2_ref, o_ref, nrm_ref, acc_ref):
    j = pl.program_id(1)
    last = pl.num_programs(1) - 1

    @pl.when(j == 0)
    def _():
        nrm_ref[...] = _rms(h_ref[...], g_ref[...]).astype(BF16)

    n = nrm_ref[...]
    a = jnp.dot(n, w1_ref[...], preferred_element_type=F32)
    b = jnp.dot(n, w3_ref[...], preferred_element_type=F32)
    mid = (a * jax.nn.sigmoid(a) * b).astype(BF16)
    part = jnp.dot(mid, w2_ref[...], preferred_element_type=F32)

    @pl.when(j == 0)
    def _():
        acc_ref[...] = part

    @pl.when((j > 0) & (j < last))
    def _():
        acc_ref[...] += part

    @pl.when(j == last)
    def _():
        o_ref[...] = h_ref[...] + 0.5 * (acc_ref[...] + part)


def _ffn(h, g, w13, w2, tm=TILE_FFN, tf=TILE_FFN_COLS):
    M, D = h.shape
    nff = D_FF // tf
    assert nff >= 2
    return pl.pallas_call(
        _ffn_kernel,
        grid=(M // tm, nff),
        in_specs=[
            pl.BlockSpec((tm, D), lambda i, j: (i, 0)),
            pl.BlockSpec((1, D), lambda i, j: (0, 0)),
            pl.BlockSpec((D, tf), lambda i, j: (0, j)),
            pl.BlockSpec((D, tf), lambda i, j: (0, j + nff)),
            pl.BlockSpec((tf, D), lambda i, j: (j, 0)),
        ],
        out_specs=pl.BlockSpec((tm, D), lambda i, j: (i, 0)),
        out_shape=jax.ShapeDtypeStruct((M, D), F32),
        scratch_shapes=[pltpu.VMEM((tm, D), BF16), pltpu.VMEM((tm, D), F32)],
        compiler_params=_cparams("parallel", "arbitrary"),
        name="ffn",
    )(h, g.reshape(1, D), w13, w13, w2)


def _nmm_kernel(h_ref, g_ref, w_ref, o_ref, nrm_ref):
    @pl.when(pl.program_id(1) == 0)
    def _():
        nrm_ref[...] = _rms(h_ref[...], g_ref[...]).astype(BF16)

    o_ref[...] = jnp.dot(nrm_ref[...], w_ref[...], preferred_element_type=F32)


def _norm_matmul(h, g, w, tm, tn):
    M, D = h.shape
    N = w.shape[1]
    return pl.pallas_call(
        _nmm_kernel,
        grid=(M // tm, N // tn),
        in_specs=[
            pl.BlockSpec((tm, D), lambda i, j: (i, 0)),
            pl.BlockSpec((1, D), lambda i, j: (0, 0)),
            pl.BlockSpec((D, tn), lambda i, j: (0, j)),
        ],
        out_specs=pl.BlockSpec((tm, tn), lambda i, j: (i, j)),
        out_shape=jax.ShapeDtypeStruct((M, N), F32),
        scratch_shapes=[pltpu.VMEM((tm, D), BF16)],
        compiler_params=_cparams("parallel", "arbitrary"),
        name="norm_matmul",
    )(h, g.reshape(1, D), w)


def _rwprep_kernel(h_ref, gm_ref, w_ref, mu_ref, w0_ref, w2_ref, a0_ref, a2_ref, g2_ref,
                   kk_ref, ka_ref, rk_ref, ones_ref,
                   r_out, lw_out, cum_out, k_out, v_out, a_out, b_out, g_out, bonus_out, last_ref, *, tiles_per_seq):
    i = pl.program_id(0)
    p = jnp.dot(_rms(h_ref[...], gm_ref[...]).astype(BF16), w_ref[...], preferred_element_type=F32)
    tm = p.shape[0]

    @pl.when(i % tiles_per_seq == 0)
    def _():
        last_ref[...] = jnp.zeros_like(last_ref)

    prev_row = last_ref[0:1, :]
    last_ref[0:1, :] = p[tm - 1:tm, :]
    rows = lax.broadcasted_iota(jnp.int32, p.shape, 0)
    shifted = jnp.where(rows == 0, prev_row, pltpu.roll(p, 1, 0))
    pm = p + (shifted - p) * mu_ref[...]

    W = RW_WIDTH
    r, k, v = pm[:, 0:W], pm[:, W:2 * W], pm[:, 2 * W:3 * W]
    wa = pm[:, 3 * W:3 * W + RW_LORA_PAD]
    gd = pm[:, 3 * W + RW_LORA_PAD:]

    w_in = w0_ref[...] + _dot3(jnp.tanh(wa), w2_ref[...])
    lw = -math.exp(-0.5) * jax.nn.sigmoid(w_in)
    a_sig = jax.nn.sigmoid(a0_ref[...] + _dot3(wa, a2_ref[...]))

    C = RW_CHUNK
    ci = lax.broadcasted_iota(jnp.int32, (C, C), 0)
    cj = lax.broadcasted_iota(jnp.int32, (C, C), 1)
    tri = jnp.where(cj <= ci, 1.0, 0.0).astype(BF16)
    l1 = lw.astype(BF16)
    rem = lw - l1.astype(F32)
    l2 = rem.astype(BF16)
    l3 = (rem - l2.astype(F32)).astype(BF16)
    td = lambda x: jnp.dot(tri, x, preferred_element_type=F32)
    cum = jnp.concatenate([td(l1[c0:c0 + C]) + td(l2[c0:c0 + C]) + td(l3[c0:c0 + C])
                           for c0 in range(0, tm, C)], axis=0)
    g = _dot(jax.nn.sigmoid(gd), g2_ref[...])

    ones = ones_ref[...]
    kk = k * kk_ref[...]
    ss = _dot(kk * kk, ones)
    kk = kk * lax.rsqrt(jnp.maximum(ss, 1e-24))
    k2 = k * (1.0 + (a_sig - 1.0) * ka_ref[...])
    bonus = _dot(r * k2 * rk_ref[...], ones) * v

    g_out[...] = g
    bonus_out[...] = bonus
    a_vec = -kk
    b_vec = kk * a_sig
    for h in range(RW_HEADS):
        sl = slice(h * RW_HEAD_DIM, (h + 1) * RW_HEAD_DIM)
        r_out[h] = r[:, sl]
        lw_out[h] = lw[:, sl]
        cum_out[h] = cum[:, sl]
        k_out[h] = k2[:, sl]
        v_out[h] = v[:, sl]
        a_out[h] = a_vec[:, sl]
        b_out[h] = b_vec[:, sl]


def _rwkv_prep(h, g_mix, w_rw, B, T, mu, w0, w2p, a0, a2p, g2p, k_k, k_a, r_k, ones, tm=TILE_RWKV_PREP):
    M, D = h.shape
    tps = T // tm
    row = lambda x: x.reshape(1, -1)
    full = lambda shape: pl.BlockSpec(shape, lambda i: (0,) * len(shape))
    head_spec = pl.BlockSpec((None, RW_HEADS, tm, RW_HEAD_DIM), lambda i: (i // tps, 0, i % tps, 0))
    head_shape = jax.ShapeDtypeStruct((B, RW_HEADS, T, RW_HEAD_DIM), F32)
    tok_spec = pl.BlockSpec((tm, RW_WIDTH), lambda i: (i, 0))
    tok_shape = jax.ShapeDtypeStruct((M, RW_WIDTH), F32)
    return pl.pallas_call(
        functools.partial(_rwprep_kernel, tiles_per_seq=tps),
        grid=(M // tm,),
        in_specs=[
            pl.BlockSpec((tm, D), lambda i: (i, 0)), full((1, D)), full((D, RW_PAD_COLS)),
            full((1, RW_PAD_COLS)), full((1, RW_WIDTH)), full((RW_LORA_PAD, RW_WIDTH)),
            full((1, RW_WIDTH)), full((RW_LORA_PAD, RW_WIDTH)), full((RW_GATE_PAD, RW_WIDTH)),
            full((1, RW_WIDTH)), full((1, RW_WIDTH)), full((1, RW_WIDTH)), full((RW_WIDTH, RW_WIDTH)),
        ],
        out_specs=[head_spec] * 7 + [tok_spec] * 2,
        out_shape=[head_shape] * 7 + [tok_shape] * 2,
        scratch_shapes=[pltpu.VMEM((8, RW_PAD_COLS), F32)],
        compiler_params=_cparams("arbitrary"),
        name="rwkv_prep",
    )(h, row(g_mix), w_rw, row(mu), row(w0), w2p, row(a0), a2p, g2p, row(k_k), row(k_a), row(r_k), ones)


def _bmm(a, b, nt=False):
    spec = 'gik,gjk->gij' if nt else 'gij,gjk->gik'
    return jnp.einsum(spec, a.astype(BF16), b.astype(BF16), preferred_element_type=F32)


def _rwkv_kernel(r_ref, lw_ref, cum_ref, k_ref, v_ref, a_ref, b_ref, bonus_ref, g_ref, lnw_ref, lnb_ref, avg_ref,
                 y_ref, st_ref, ytok_ref):
    C = RW_CHUNK
    H = RW_HEADS
    Tc = r_ref.shape[1]
    nc = Tc // C
    G = H * nc

    @pl.when(pl.program_id(1) == 0)
    def _():
        st_ref[...] = jnp.zeros_like(st_ref)

    shp = lambda ref: ref[...].reshape(G, C, RW_HEAD_DIM)
    r, lw, L, k, v, a, b = (shp(x) for x in (r_ref, lw_ref, cum_ref, k_ref, v_ref, a_ref, b_ref))

    ti = lax.broadcasted_iota(jnp.int32, (C, C), 0)
    tj = lax.broadcasted_iota(jnp.int32, (C, C), 1)
    eye = jnp.where(ti == tj, 1.0, 0.0).astype(F32)
    wi = lax.broadcasted_iota(jnp.int32, (C, 2 * C), 0)
    wj = lax.broadcasted_iota(jnp.int32, (C, 2 * C), 1)
    second = wj >= C
    wjc = jnp.where(second, wj - C, wj)

    Lprev = L - lw
    Ltot = L[:, C - 1:C, :]

    eL = jnp.exp(L)
    enL = jnp.exp(-L)
    eh = jnp.exp(Ltot - L)
    At = a * jnp.exp(Lprev)
    Rt = r * eL
    BKt = jnp.concatenate([b * enL, k * enL], axis=1)
    BKh = jnp.concatenate([b * eh, k * eh], axis=1)

    P4 = _bmm(jnp.concatenate([At, Rt], axis=1), BKt, nt=True)
    top, bot = P4[:, :C, :], P4[:, C:, :]
    N = jnp.where((tj < ti)[None], top[:, :, :C], 0.0)
    AKz = jnp.where((second & (wjc < wi))[None], top, 0.0)
    RBK = jnp.where((wjc <= wi)[None], bot, 0.0)
    RB = RBK[:, :, :C]

    def siblings(s):
        return ((ti // (2 * s)) == (tj // (2 * s))) & ((ti // s) != (tj // s))

    X = eye[None] + jnp.where(siblings(1)[None], N, 0.0)
    s_blk = 2
    while s_blk < C:
        XE = _bmm(X, jnp.where(siblings(s_blk)[None], N, 0.0))
        X = X + _bmm(XE, X)
        s_blk *= 2

    zv = jnp.concatenate([jnp.zeros_like(v), v], axis=1)
    Abar = _bmm(X, At)
    W0 = _bmm(X, _bmm(AKz, zv))
    wv = jnp.concatenate([W0, v], axis=1)
    Y0 = _bmm(RBK, wv)
    Rbar = Rt + _bmm(RB, Abar)
    BKhT = jnp.swapaxes(BKh, 1, 2)
    Mtx = _bmm(BKhT[:, :, :C], Abar) + eye[None] * jnp.exp(Ltot)
    G0 = _bmm(BKhT, wv)

    hsplit = lambda x: x.reshape(H, nc, x.shape[1], x.shape[2])
    RM = hsplit(jnp.concatenate([Rbar, Mtx], axis=1))
    Y0, G0 = hsplit(Y0), hsplit(G0)
    St = st_ref[...]
    ys = []
    for c in range(nc):
        both = _bmm(RM[:, c], St)
        ys.append(both[:, :C] + Y0[:, c])
        St = both[:, C:] + G0[:, c]
    st_ref[...] = St
    y = jnp.concatenate(ys, axis=1)
    for h in range(H):
        ytok_ref[:, h * RW_HEAD_DIM:(h + 1) * RW_HEAD_DIM] = y[h]

    y = ytok_ref[...]
    avg = avg_ref[...]
    mean = _dot(y, avg)
    yc = y - mean
    var = _dot(yc * yc, avg)
    ya = (yc * lax.rsqrt(var + RW_GN_EPS) * lnw_ref[...] + lnb_ref[...] + bonus_ref[...]) * g_ref[...]
    y_ref[...] = ya.astype(BF16)


def _rwkv_recurrence(r, lw, cum, k, v, a, b, bonus, g, ln_w, ln_b, avg, tc=TILE_RWKV_SCAN):
    B, H, T, Dh = r.shape
    W = H * Dh
    nt = T // tc
    spec = pl.BlockSpec((None, H, tc, Dh), lambda bi, c: (bi, 0, c, 0))
    tok = pl.BlockSpec((tc, W), lambda bi, c: (bi * nt + c, 0))
    const = lambda shape: pl.BlockSpec(shape, lambda bi, c: (0, 0))
    return pl.pallas_call(
        _rwkv_kernel,
        grid=(B, nt),
        in_specs=[spec] * 7 + [tok, tok, const((1, W)), const((1, W)), const((W, W))],
        out_specs=tok,
        out_shape=jax.ShapeDtypeStruct((B * T, W), BF16),
        scratch_shapes=[pltpu.VMEM((H, Dh, Dh), F32), pltpu.VMEM((tc, W), F32)],
        compiler_params=_cparams("parallel", "arbitrary"),
        name="rwkv_recurrence",
    )(r, lw, cum, k, v, a, b, bonus, g, ln_w.reshape(1, W), ln_b.reshape(1, W), avg)


def _rope_tables(T, heads, head_dim, rot_dim, base):
    half = rot_dim // 2
    inv_freq = base ** (-jnp.arange(half, dtype=F32) / half)
    ang = jnp.arange(T).astype(F32)[:, None] * inv_freq[None, :]
    cos, sin = jnp.cos(ang), jnp.sin(ang)
    rest = head_dim - rot_dim
    c = jnp.concatenate([cos, cos, jnp.ones((T, rest), F32)], axis=1)
    s_lo = jnp.concatenate([-sin, jnp.zeros((T, half + rest), F32)], axis=1)
    s_hi = jnp.concatenate([jnp.zeros((T, half), F32), sin, jnp.zeros((T, rest), F32)], axis=1)
    tile = lambda x: jnp.tile(x, (1, heads))
    return tile(c), tile(s_lo), tile(s_hi)


def _apply_rope(x, c, s_lo, s_hi, half):
    out = []
    for gl in range(x.shape[-1] // LANES):
        ln = slice(gl * LANES, (gl + 1) * LANES)
        xg = x[:, ln]
        out.append(xg * c[:, ln] + pltpu.roll(xg, LANES - half, 1) * s_lo[:, ln] + pltpu.roll(xg, half, 1) * s_hi[:, ln])
    return jnp.concatenate(out, axis=-1)


def _dilproj_kernel(h_ref, g_ref, w_ref, c_ref, slo_ref, shi_ref, gq_ref, gk_ref, avg_ref,
                    o0_ref, o1_ref, o2_ref, tmp_ref):
    u = _rms(h_ref[...], g_ref[...]).astype(BF16)
    c, s_lo, s_hi = c_ref[...], slo_ref[...], shi_ref[...]
    W = DIL_WIDTH
    for grp, o_ref in enumerate((o0_ref, o1_ref, o2_ref)):
        dilation = DIL_PATTERNS[grp][1]
        rows = u.shape[0] // dilation
        for which in range(3):
            col = (3 * grp + which) * W
            val = jnp.dot(u, w_ref[:, col:col + W], preferred_element_type=F32)
            if which < 2:
                gain = (gq_ref if which == 0 else gk_ref)[grp:grp + 1, :]
                ms = jnp.dot((val * val).astype(BF16), avg_ref[...], preferred_element_type=F32)
                val = _apply_rope(val * lax.rsqrt(ms + EPS) * gain, c, s_lo, s_hi, ROPE_DIM // 2)
            if dilation == 1:
                o_ref[which, 0] = val.astype(BF16)
            else:
                _store_lane_groups(tmp_ref, val)
                for res in range(dilation):
                    for gl in range(tmp_ref.shape[0]):
                        piece = tmp_ref[gl, pl.ds(res, rows, stride=dilation), :]
                        o_ref[which, res, :, gl * LANES:(gl + 1) * LANES] = piece.astype(BF16)


def _dilated_projection(h, g_mix, w_dil, B, T, tabs, gq, gk, avg, tm=TILE_DIL_PROJ):
    M, D = h.shape
    tps = T // tm
    W = DIL_WIDTH
    const = lambda shape: pl.BlockSpec(shape, lambda i: (0,) * len(shape))
    tspec = pl.BlockSpec((tm, W), lambda i: (i % tps, 0))
    out_specs, out_shape = [], []
    for _, d in DIL_PATTERNS:
        out_specs.append(pl.BlockSpec((3, None, d, tm // d, W), lambda i: (0, i // tps, 0, i % tps, 0)))
        out_shape.append(jax.ShapeDtypeStruct((3, B, d, T // d, W), BF16))
    return pl.pallas_call(
        _dilproj_kernel,
        grid=(M // tm,),
        in_specs=[pl.BlockSpec((tm, D), lambda i: (i, 0)), const((1, D)), const((D, 3 * N_DIL * W)),
                  tspec, tspec, tspec, const((N_DIL, W)), const((N_DIL, W)), const((W, W))],
        out_specs=out_specs,
        out_shape=out_shape,
        scratch_shapes=[pltpu.VMEM((W // LANES, tm, LANES), F32)],
        compiler_params=_cparams("parallel"),
        name="dilated_projection",
    )(h, g_mix.reshape(1, D), w_dil, *tabs, jnp.tile(gq * Q_PRESCALE, (1, DIL_HEADS)), jnp.tile(gk, (1, DIL_HEADS)), avg)


def _dilattn_kernel(q_ref, k_ref, v_ref, kprev_ref, vprev_ref, o_ref, lse_ref, *, blocks):
    i = pl.program_id(2)
    Lb = DIL_BLOCK
    PW = 2 * HEAD_DIM
    NP = DIL_WIDTH // PW
    nres = q_ref.shape[0]
    lane = lax.broadcasted_iota(jnp.int32, (Lb, PW), 1)
    low = lane < HEAD_DIM

    qs, ks, vs = [], [], []
    for blk in range(blocks):
        rows = slice(blk * Lb, (blk + 1) * Lb)
        for rr in range(nres):
            for p in range(NP):
                ln = slice(p * PW, (p + 1) * PW)
                q2 = q_ref[rr, rows, ln]
                zero = jnp.zeros_like(q2)
                qs.append(jnp.concatenate([jnp.where(low, q2, zero), jnp.where(low, zero, q2)], axis=0))
                if blk == 0:
                    ks.append(jnp.concatenate([kprev_ref[rr, :, ln], k_ref[rr, rows, ln]], axis=0))
                    vs.append(jnp.concatenate([vprev_ref[rr, :, ln], v_ref[rr, rows, ln]], axis=0))
                else:
                    both = slice((blk - 1) * Lb, (blk + 1) * Lb)
                    ks.append(k_ref[rr, both, ln])
                    vs.append(v_ref[rr, both, ln])
    Q, K, V = jnp.stack(qs), jnp.stack(ks), jnp.stack(vs)
    first = nres * NP

    s = jnp.einsum('gqd,gkd->gqk', Q, K, preferred_element_type=F32)
    qi = lax.broadcasted_iota(jnp.int32, (2 * Lb, 2 * Lb), 0) % Lb
    kj = lax.broadcasted_iota(jnp.int32, (2 * Lb, 2 * Lb), 1)
    window = (kj >= qi) & (kj <= qi + Lb)
    window0 = window & ((kj >= Lb) | (i > 0))
    s = jnp.concatenate([jnp.where(window0[None], s[:first], -jnp.inf),
                         jnp.where(window[None], s[first:], -jnp.inf)], axis=0) if blocks > 1 else \
        jnp.where(window0[None], s, -jnp.inf)
    m = jnp.max(s, axis=-1, keepdims=True)
    e = jnp.exp2(s - m).astype(BF16)
    vx = jnp.concatenate([V, jnp.ones_like(V)], axis=-1)
    nd = jnp.einsum('gqk,gkd->gqd', e, vx, preferred_element_type=F32)
    num, den = nd[:, :, :PW], nd[:, :, PW:]
    o2 = num / den
    l2 = m * math.log(2.0) + jnp.log(den)
    head_of_lane = lane // LSE_LANES
    for blk in range(blocks):
        rows = slice(blk * Lb, (blk + 1) * Lb)
        for rr in range(nres):
            lse_c = jnp.zeros((Lb, PW), F32)
            for p in range(NP):
                g = (blk * nres + rr) * NP + p
                ln = slice(p * PW, (p + 1) * PW)
                o_ref[rr, rows, ln] = jnp.where(low, o2[g, :Lb], o2[g, Lb:]).astype(BF16)
                lse_c = jnp.where(head_of_lane == 2 * p, l2[g, :Lb],
                                  jnp.where(head_of_lane == 2 * p + 1, l2[g, Lb:], lse_c))
            lse_ref[rr, rows, :] = lse_c


def _dilated_attention(qkv):
    _, B, d, Mr, W = qkv.shape
    Lb = DIL_BLOCK
    nb = Mr // Lb
    blocks = min(DIL_BLOCKS_PER_STEP, nb)
    nres = min(DIL_BLOCKS_PER_STEP // blocks, d)
    tile = blocks * Lb

    def cur(which):
        return pl.BlockSpec((None, None, nres, tile, W), lambda b, c, i: (which, b, c, i, 0))

    def prev(which):
        return pl.BlockSpec((None, None, nres, Lb, W),
                            lambda b, c, i: (which, b, c, jnp.maximum(i * blocks - 1, 0), 0))

    ospec = pl.BlockSpec((None, nres, tile, W), lambda b, c, i: (b, c, i, 0))
    lspec = pl.BlockSpec((None, nres, tile, LANES), lambda b, c, i: (b, c, i, 0))
    return pl.pallas_call(
        functools.partial(_dilattn_kernel, blocks=blocks),
        grid=(B, d // nres, nb // blocks),
        in_specs=[cur(0), cur(1), cur(2), prev(1), prev(2)],
        out_specs=[ospec, lspec],
        out_shape=[jax.ShapeDtypeStruct((B, d, Mr, W), BF16), jax.ShapeDtypeStruct((B, d, Mr, LANES), F32)],
        compiler_params=_cparams("parallel", "parallel", "parallel"),
        name=f"dilated_attention_d{d}",
    )(qkv, qkv, qkv, qkv, qkv)


def _ret_kernel(h_ref, gm_ref, w_ref, c_ref, slo_ref, shi_ref, gain_ref, o_ref, st_ref):
    C = RET_CHUNK
    QK = RET_HEADS * RET_QK_DIM

    @pl.when(pl.program_id(1) == 0)
    def _():
        st_ref[...] = jnp.zeros_like(st_ref)

    u = _rms(h_ref[...], gm_ref[...]).astype(BF16)
    p = jnp.dot(u, w_ref[...], preferred_element_type=F32)
    c, s_lo, s_hi = c_ref[...], slo_ref[...], shi_ref[...]
    q = _apply_rope(p[:, :QK], c, s_lo, s_hi, RET_QK_DIM // 2)
    k = _apply_rope(p[:, QK:2 * QK], c, s_lo, s_hi, RET_QK_DIM // 2) * (RET_QK_DIM ** -0.5)
    v = p[:, 2 * QK:2 * QK + RET_WIDTH]
    g = p[:, 2 * QK + RET_WIDTH:]
    gain = gain_ref[...]

    ji = lax.broadcasted_iota(jnp.int32, (C, C), 0)
    jj = lax.broadcasted_iota(jnp.int32, (C, C), 1)
    diff = (ji - jj).astype(F32)
    jcol = lax.broadcasted_iota(jnp.int32, (C, 1), 0).astype(F32)

    for h in range(RET_HEADS):
        lg = math.log(1.0 - 2.0 ** (-5.0 - h))
        qs = slice(h * RET_QK_DIM, (h + 1) * RET_QK_DIM)
        vs = slice(h * RET_V_DIM, (h + 1) * RET_V_DIM)
        decay_in = jnp.where(diff >= 0, jnp.exp(lg * jnp.maximum(diff, 0.0)), 0.0)
        q_dec = jnp.exp(lg * (jcol + 1.0))
        k_dec = jnp.exp(lg * (C - 1.0 - jcol))
        S = st_ref[h]
        for ch in range(p.shape[0] // C):
            rows = slice(ch * C, (ch + 1) * C)
            qh, kh, vh = q[rows, qs], k[rows, qs], v[rows, vs]
            s = lax.dot_general(qh.astype(BF16), kh.astype(BF16), (((1,), (1,)), ((), ())),
                                preferred_element_type=F32) * decay_in
            y = _dot(s, vh) + _dot(qh * q_dec, S)
            kv = lax.dot_general((kh * k_dec).astype(BF16), vh.astype(BF16), (((0,), (0,)), ((), ())),
                                 preferred_element_type=F32)
            S = math.exp(lg * C) * S + kv
            yn = y * lax.rsqrt(jnp.mean(y * y, axis=-1, keepdims=True) + EPS) * gain[:, vs]
            gh = g[rows, vs]
            o_ref[rows, vs] = (gh * jax.nn.sigmoid(gh) * yn).astype(BF16)
        st_ref[h] = S


def _retention(h, g_mix, w_ret, B, T, tabs, gain, tr=TILE_RETENTION):
    M, D = h.shape
    nt = T // tr
    qk = RET_HEADS * RET_QK_DIM
    const = lambda shape: pl.BlockSpec(shape, lambda b, t: (0,) * len(shape))
    tspec = pl.BlockSpec((tr, qk), lambda b, t: (t, 0))
    return pl.pallas_call(
        _ret_kernel,
        grid=(B, nt),
        in_specs=[pl.BlockSpec((tr, D), lambda b, t: (b * nt + t, 0)), const((1, D)), const(w_ret.shape),
                  tspec, tspec, tspec, const((1, RET_WIDTH))],
        out_specs=pl.BlockSpec((tr, RET_WIDTH), lambda b, t: (b * nt + t, 0)),
        out_shape=jax.ShapeDtypeStruct((M, RET_WIDTH), BF16),
        scratch_shapes=[pltpu.VMEM((RET_HEADS, RET_QK_DIM, RET_V_DIM), F32)],
        compiler_params=_cparams("parallel", "arbitrary"),
        name="retention",
    )(h, g_mix.reshape(1, D), w_ret, *tabs, gain.reshape(1, -1))


def _merge_kernel(h_ref, gm_ref, wg_ref, ya_ref, o0_ref, o1_ref, o2_ref, l0_ref, l1_ref, l2_ref, yc_ref,
                  wa_ref, wb_ref, wc_ref, wo_ref, expand_ref, out_ref, o1_scr, o2_scr, l1_scr, l2_scr):
    h = h_ref[...]
    u = _rms(h, gm_ref[...]).astype(BF16)

    def token_order(ref, scr):
        dil, rows = ref.shape[0], ref.shape[1]
        for c in range(dil):
            blk = ref[c].astype(F32)
            for gl in range(scr.shape[0]):
                scr[gl, pl.ds(c, rows, stride=dil), :] = blk[:, gl * LANES:(gl + 1) * LANES]
        return jnp.concatenate([scr[gl] for gl in range(scr.shape[0])], axis=-1)

    o0, l0 = o0_ref[0].astype(F32), l0_ref[0]
    o1, l1 = token_order(o1_ref, o1_scr), token_order(l1_ref, l1_scr)
    o2, l2 = token_order(o2_ref, o2_scr), token_order(l2_ref, l2_scr)

    mx = jnp.maximum(jnp.maximum(l0, l1), l2)
    e0, e1, e2 = jnp.exp(l0 - mx), jnp.exp(l1 - mx), jnp.exp(l2 - mx)
    inv = 1.0 / (e0 + e1 + e2)

    def spread(w):
        hi, lo = _split2(w)
        return (jnp.dot(hi, expand_ref[...], preferred_element_type=F32)
                + jnp.dot(lo, expand_ref[...], preferred_element_type=F32))

    yb = spread(e0 * inv) * o0 + spread(e1 * inv) * o1 + spread(e2 * inv) * o2

    D = D_MODEL
    gate = lambda i: jax.nn.sigmoid(jnp.dot(u, wg_ref[:, i * D:(i + 1) * D], preferred_element_type=F32))
    merged = (gate(0) * jnp.dot(ya_ref[...], wa_ref[...], preferred_element_type=F32)
              + gate(1) * _dot(yb, wb_ref[...])
              + gate(2) * jnp.dot(yc_ref[...], wc_ref[...], preferred_element_type=F32))
    out_ref[...] = h + _dot(merged, wo_ref[...])


def _merge(h, T, g_mix, w_gate, y_a, o, lse, y_c, wa, wb, wc, wo, tm=TILE_MERGE):
    M, D = h.shape
    tps = T // tm
    tok = lambda w: pl.BlockSpec((tm, w), lambda i: (i, 0))
    full = lambda shape: pl.BlockSpec(shape, lambda i: (0, 0))
    W = RW_WIDTH

    def res(group, width):
        d = DIL_PATTERNS[group][1]
        return pl.BlockSpec((None, d, tm // d, width), lambda i: (i // tps, 0, i % tps, 0))

    li = lax.broadcasted_iota(jnp.int32, (LANES, W), 0)
    lj = lax.broadcasted_iota(jnp.int32, (LANES, W), 1)
    expand = jnp.where(li == (lj // HEAD_DIM) * LSE_LANES, 1.0, 0.0).astype(BF16)

    return pl.pallas_call(
        _merge_kernel,
        grid=(M // tm,),
        in_specs=[tok(D), full((1, D)), full((D, 3 * D)), tok(W),
                  res(0, W), res(1, W), res(2, W), res(0, LANES), res(1, LANES), res(2, LANES), tok(W),
                  full((W, D)), full((W, D)), full((W, D)), full((D, D)), full((LANES, W))],
        out_specs=tok(D),
        out_shape=jax.ShapeDtypeStruct((M, D), F32),
        scratch_shapes=[pltpu.VMEM((W // LANES, tm, LANES), F32)] * 2 + [pltpu.VMEM((1, tm, LANES), F32)] * 2,
        compiler_params=_cparams("parallel"),
        name="gated_merge",
    )(h, g_mix.reshape(1, D), w_gate, y_a, o[0], o[1], o[2], lse[0], lse[1], lse[2], y_c, wa, wb, wc, wo, expand)


def _xattn_kernel(h_ref, gx_ref, wq_ref, kv_ref, qn_ref, kn_ref, wo_ref, out_ref, k_scr, v_scr):
    D = D_MODEL

    @pl.when(pl.program_id(1) == 0)
    def _():
        kv = kv_ref[...]
        for hd in range(XA_HEADS):
            sl = slice(hd * XA_HEAD_DIM, (hd + 1) * XA_HEAD_DIM)
            k_scr[:, sl] = _rms(kv[:, sl], kn_ref[...]).astype(BF16)
        v_scr[...] = kv[:, D:].astype(BF16)

    h = h_ref[...]
    hn = _rms(h, gx_ref[...]).astype(BF16)
    q = jnp.dot(hn, wq_ref[...], preferred_element_type=F32)
    outs = []
    for hd in range(XA_HEADS):
        sl = slice(hd * XA_HEAD_DIM, (hd + 1) * XA_HEAD_DIM)
        qh = _rms(q[:, sl], qn_ref[...]).astype(BF16)
        s = lax.dot_general(qh, k_scr[:, sl], (((1,), (1,)), ((), ())), preferred_element_type=F32)
        m = jnp.max(s, axis=-1, keepdims=True)
        e = jnp.exp2(s - m)
        pr = e / jnp.sum(e, axis=-1, keepdims=True)
        outs.append(jnp.dot(pr.astype(BF16), v_scr[:, sl], preferred_element_type=F32))
    o = jnp.concatenate(outs, axis=-1).astype(BF16)
    out_ref[...] = h + jnp.dot(o, wo_ref[...], preferred_element_type=F32)


def _cross_attention(h, B, T, g_x, wq, kv, q_norm, k_norm, wo, tm=TILE_XATTN):
    M, D = h.shape
    tps = T // tm
    full = lambda shape: pl.BlockSpec(shape, lambda b, t: (0, 0))
    return pl.pallas_call(
        _xattn_kernel,
        grid=(B, tps),
        in_specs=[
            pl.BlockSpec((tm, D), lambda b, t: (b * tps + t, 0)),
            full((1, D)), full((D, D)),
            pl.BlockSpec((MEM_LEN, 2 * D), lambda b, t: (b, 0)),
            full((1, XA_HEAD_DIM)), full((1, XA_HEAD_DIM)), full((D, D)),
        ],
        out_specs=pl.BlockSpec((tm, D), lambda b, t: (b * tps + t, 0)),
        out_shape=jax.ShapeDtypeStruct((M, D), F32),
        scratch_shapes=[pltpu.VMEM((MEM_LEN, D), BF16), pltpu.VMEM((MEM_LEN, D), BF16)],
        compiler_params=_cparams("parallel", "arbitrary"),
        name="cross_attention",
    )(h, g_x.reshape(1, D), wq, kv, (q_norm * (XA_HEAD_DIM ** -0.5 * math.log2(math.e))).reshape(1, -1),
      k_norm.reshape(1, -1), wo)


def _pad_rows(w, rows_before, total):
    return jnp.pad(w, ((rows_before, total - rows_before - w.shape[0]), (0, 0)))


def _layer(h, mem2, B, T, p, dil_tabs, ret_tabs):
    bf = lambda w: w.astype(BF16)
    ones = _block_ones(RW_WIDTH, RW_HEAD_DIM)
    avg = ones * (1.0 / RW_HEAD_DIM)
    h = _ffn(h, p['norm_ffn1'], bf(p['ffn1_w13']), bf(p['ffn1_w2']))

    w_in = p['w_in']
    c_dil = RW_IN_COLS
    c_ret = c_dil + 3 * N_DIL * DIL_WIDTH
    c_gate = c_ret + 2 * RET_HEADS * RET_QK_DIM + 2 * RET_WIDTH
    rw_w = jnp.pad(w_in[:, :c_dil], ((0, 0), (0, RW_PAD_COLS - RW_IN_COLS)))
    mu = jnp.pad(p['rw_mu'], (0, RW_PAD_COLS - RW_IN_COLS))

    w2p = _pad_rows(p['rw_w2'], 0, RW_LORA_PAD)
    a2p = _pad_rows(p['rw_a2'], RW_LORA, RW_LORA_PAD)
    g2p = _pad_rows(p['rw_g2'], 0, RW_GATE_PAD)
    r, lw, cum, k, v, a, b, g, bonus = _rwkv_prep(h, p['norm_mix'], bf(rw_w), B, T, mu, p['rw_w0'], w2p, p['rw_a0'], a2p, g2p,
                                             p['rw_k_k'], p['rw_k_a'], p['rw_r_k'], ones)
    y_a = _rwkv_recurrence(r, lw, cum, k, v, a, b, bonus, g, p['rw_ln_w'], p['rw_ln_b'], avg)

    qkvs = _dilated_projection(h, p['norm_mix'], bf(w_in[:, c_dil:c_ret]), B, T, dil_tabs,
                               p['dil_q_norm'], p['dil_k_norm'], avg)
    o, lse = zip(*[_dilated_attention(qkv) for qkv in qkvs])

    y_c = _retention(h, p['norm_mix'], bf(w_in[:, c_ret:c_gate]), B, T, ret_tabs, p['ret_norm'])

    h = _merge(h, T, p['norm_mix'], bf(w_in[:, c_gate:]), y_a, o, lse, y_c,
               bf(p['w_branch_rwkv']), bf(p['w_branch_dil']), bf(p['w_branch_ret']), bf(p['w_out']))

    kv = _norm_matmul(mem2, p['norm_mem'], bf(p['xa_wkv']), MEM_LEN, 1024)
    h = _cross_attention(h, B, T, p['norm_xattn'], bf(p['xa_wq']), kv, p['xa_q_norm'], p['xa_k_norm'],
                         bf(p['xa_wo']))
    h = _ffn(h, p['norm_ffn2'], bf(p['ffn2_w13']), bf(p['ffn2_w2']))
    return h


_PARAM_NAMES = ('norm_ffn1', 'ffn1_w13', 'ffn1_w2', 'norm_mix', 'w_in', 'rw_mu', 'rw_w0', 'rw_w2', 'rw_a0',
                'rw_a2', 'rw_g2', 'rw_k_k', 'rw_k_a', 'rw_r_k', 'rw_ln_w', 'rw_ln_b', 'dil_q_norm',
                'dil_k_norm', 'ret_norm', 'w_branch_rwkv', 'w_branch_dil', 'w_branch_ret', 'w_out',
                'norm_xattn', 'norm_mem', 'xa_wq', 'xa_wkv', 'xa_q_norm', 'xa_k_norm', 'xa_wo',
                'norm_ffn2', 'ffn2_w13', 'ffn2_w2')


def kernel(x, mem, norm_ffn1, ffn1_w13, ffn1_w2, norm_mix, w_in, rw_mu, rw_w0, rw_w2, rw_a0, rw_a2, rw_g2, rw_k_k, rw_k_a, rw_r_k, rw_ln_w, rw_ln_b, dil_q_norm, dil_k_norm, ret_norm, w_branch_rwkv, w_branch_dil, w_branch_ret, w_out, norm_xattn, norm_mem, xa_wq, xa_wkv, xa_q_norm, xa_k_norm, xa_wo, norm_ffn2, ffn2_w13, ffn2_w2):
    params = dict(zip(_PARAM_NAMES, (norm_ffn1, ffn1_w13, ffn1_w2, norm_mix, w_in, rw_mu, rw_w0, rw_w2, rw_a0,
                                     rw_a2, rw_g2, rw_k_k, rw_k_a, rw_r_k, rw_ln_w, rw_ln_b, dil_q_norm,
                                     dil_k_norm, ret_norm, w_branch_rwkv, w_branch_dil, w_branch_ret, w_out,
                                     norm_xattn, norm_mem, xa_wq, xa_wkv, xa_q_norm, xa_k_norm, xa_wo,
                                     norm_ffn2, ffn2_w13, ffn2_w2)))
    B, T, D = x.shape
    assert D == D_MODEL and T % (DIL_BLOCK * DIL_PATTERNS[-1][1]) == 0 and mem.shape[1] == MEM_LEN
    depth = norm_ffn1.shape[0]
    dil_tabs = _rope_tables(T, DIL_HEADS, HEAD_DIM, ROPE_DIM, ROPE_THETA)
    ret_tabs = _rope_tables(T, RET_HEADS, RET_QK_DIM, RET_QK_DIM, RET_ROPE_BASE)
    h = x.reshape(B * T, D)
    mem2 = mem.reshape(B * MEM_LEN, D)
    for l in range(depth):
        h = _layer(h, mem2, B, T, {n: params[n][l] for n in _PARAM_NAMES}, dil_tabs, ret_tabs)
    return h.reshape(B, T, D)
```

```python
import functools
import math

import jax
import jax.numpy as jnp
from jax import lax
from jax.experimental import pallas as pl
from jax.experimental.pallas import tpu as pltpu

F32 = jnp.float32
BF16 = jnp.bfloat16

D_MODEL = 1024
D_FF = 2816
EPS = 1e-6

RW_HEADS = 8
RW_HEAD_DIM = 64
RW_WIDTH = 512
RW_GN_EPS = 64e-5
RW_CHUNK = 64
RW_LORA = 64
RW_GATE_LORA = 160
RW_IN_COLS = 3 * RW_WIDTH + 2 * RW_LORA + RW_GATE_LORA
RW_LORA_PAD = 128
RW_GATE_PAD = 256
RW_PAD_COLS = 3 * RW_WIDTH + RW_LORA_PAD + RW_GATE_PAD

DIL_PATTERNS = ((128, 1), (512, 4), (2048, 16))
N_DIL = 3
DIL_HEADS = 8
HEAD_DIM = 64
DIL_WIDTH = 512
DIL_BLOCK = 128
DIL_BLOCKS_PER_STEP = 8
LSE_LANES = 16
ROPE_THETA = 500000.0
Q_PRESCALE = HEAD_DIM ** -0.5 * math.log2(math.e)
ROPE_DIM = 16

RET_HEADS = 4
RET_QK_DIM = 64
RET_V_DIM = 128
RET_CHUNK = 128
RET_ROPE_BASE = 10000.0
RET_WIDTH = 512

XA_HEADS = 4
XA_HEAD_DIM = 256
MEM_LEN = 256

V7X_VMEM_BYTES = 64 * 1024 * 1024
VMEM_LIMIT = V7X_VMEM_BYTES - 8 * 1024 * 1024
LANES = 128

TILE_FFN = 1024
TILE_FFN_COLS = 1408
TILE_RWKV_PREP = 512
TILE_RWKV_SCAN = 512
TILE_DIL_PROJ = 512
TILE_RETENTION = 1024
TILE_MERGE = 512
TILE_XATTN = 1024


def _cparams(*sem):
    return pltpu.CompilerParams(dimension_semantics=sem, vmem_limit_bytes=VMEM_LIMIT)


def _dot(a, b):
    return jnp.dot(a.astype(BF16), b.astype(BF16), preferred_element_type=F32)


def _split2(x):
    hi = x.astype(BF16)
    lo = (x - hi.astype(F32)).astype(BF16)
    return hi, lo


def _dot3(a, b):
    ah, al = _split2(a)
    bh, bl = _split2(b)
    return (jnp.dot(ah, bh, preferred_element_type=F32)
            + jnp.dot(ah, bl, preferred_element_type=F32)
            + jnp.dot(al, bh, preferred_element_type=F32))


def _rms(x, g):
    return x * lax.rsqrt(jnp.mean(x * x, axis=-1, keepdims=True) + EPS) * g


def _store_lane_groups(ref, x):
    for gl in range(ref.shape[0]):
        ref[gl] = x[:, gl * LANES:(gl + 1) * LANES]


def _block_ones(n, width):
    i = lax.broadcasted_iota(jnp.int32, (n, n), 0) // width
    j = lax.broadcasted_iota(jnp.int32, (n, n), 1) // width
    return jnp.where(i == j, 1.0, 0.0).astype(BF16)


def _ffn_kernel(h_ref, g_ref, w1_ref, w3_ref, w2_ref, *rest, emit_norm):
    if emit_norm:
        gn_ref, o_ref, u_ref, nrm_ref, acc_ref = rest
    else:
        o_ref, nrm_ref, acc_ref = rest
    j = pl.program_id(1)
    last = pl.num_programs(1) - 1

    @pl.when(j == 0)
    def _():
        nrm_ref[...] = _rms(h_ref[...], g_ref[...]).astype(BF16)

    n = nrm_ref[...]
    a = jnp.dot(n, w1_ref[...], preferred_element_type=F32)
    b = jnp.dot(n, w3_ref[...], preferred_element_type=F32)
    mid = (a * jax.nn.sigmoid(a) * b).astype(BF16)
    part = jnp.dot(mid, w2_ref[...], preferred_element_type=F32)

    @pl.when(j == 0)
    def _():
        acc_ref[...] = part

    @pl.when((j > 0) & (j < last))
    def _():
        acc_ref[...] += part

    @pl.when(j == last)
    def _():
        out = h_ref[...] + 0.5 * (acc_ref[...] + part)
        o_ref[...] = out
        if emit_norm:
            u_ref[...] = _rms(out, gn_ref[...]).astype(BF16)


def _ffn(h, g, w13, w2, g_next=None, tm=TILE_FFN, tf=TILE_FFN_COLS):
    M, D = h.shape
    nff = D_FF // tf
    assert nff >= 2
    emit_norm = g_next is not None
    tok = pl.BlockSpec((tm, D), lambda i, j: (i, 0))
    row = pl.BlockSpec((1, D), lambda i, j: (0, 0))
    in_specs = [tok, row,
                pl.BlockSpec((D, tf), lambda i, j: (0, j)),
                pl.BlockSpec((D, tf), lambda i, j: (0, j + nff)),
                pl.BlockSpec((tf, D), lambda i, j: (j, 0))]
    args = [h, g.reshape(1, D), w13, w13, w2]
    out_specs, out_shape = tok, jax.ShapeDtypeStruct((M, D), F32)
    if emit_norm:
        in_specs.append(row)
        args.append(g_next.reshape(1, D))
        out_specs, out_shape = [tok, tok], [out_shape, jax.ShapeDtypeStruct((M, D), BF16)]
    return pl.pallas_call(
        functools.partial(_ffn_kernel, emit_norm=emit_norm),
        grid=(M // tm, nff),
        in_specs=in_specs,
        out_specs=out_specs,
        out_shape=out_shape,
        scratch_shapes=[pltpu.VMEM((tm, D), BF16), pltpu.VMEM((tm, D), F32)],
        compiler_params=_cparams("parallel", "arbitrary"),
        name="ffn",
    )(*args)


def _nmm_kernel(h_ref, g_ref, w_ref, o_ref, nrm_ref):
    @pl.when(pl.program_id(1) == 0)
    def _():
        nrm_ref[...] = _rms(h_ref[...], g_ref[...]).astype(BF16)

    o_ref[...] = jnp.dot(nrm_ref[...], w_ref[...], preferred_element_type=F32)


def _norm_matmul(h, g, w, tm, tn):
    M, D = h.shape
    N = w.shape[1]
    return pl.pallas_call(
        _nmm_kernel,
        grid=(M // tm, N // tn),
        in_specs=[
            pl.BlockSpec((tm, D), lambda i, j: (i, 0)),
            pl.BlockSpec((1, D), lambda i, j: (0, 0)),
            pl.BlockSpec((D, tn), lambda i, j: (0, j)),
        ],
        out_specs=pl.BlockSpec((tm, tn), lambda i, j: (i, j)),
        out_shape=jax.ShapeDtypeStruct((M, N), F32),
        scratch_shapes=[pltpu.VMEM((tm, D), BF16)],
        compiler_params=_cparams("parallel", "arbitrary"),
        name="norm_matmul",
    )(h, g.reshape(1, D), w)


def _rwprep_kernel(u_ref, w_ref, mu_ref, w0_ref, w2_ref, a0_ref, a2_ref, g2_ref,
                   kk_ref, ka_ref, rk_ref, ones_ref,
                   r_out, lw_out, cum_out, k_out, v_out, a_out, b_out, g_out, bonus_out, last_ref, *, tiles_per_seq):
    i = pl.program_id(0)
    p = jnp.dot(u_ref[...], w_ref[...], preferred_element_type=F32)
    tm = p.shape[0]

    @pl.when(i % tiles_per_seq == 0)
    def _():
        last_ref[...] = jnp.zeros_like(last_ref)

    prev_row = last_ref[0:1, :]
    last_ref[0:1, :] = p[tm - 1:tm, :]
    rows = lax.broadcasted_iota(jnp.int32, p.shape, 0)
    shifted = jnp.where(rows == 0, prev_row, pltpu.roll(p, 1, 0))
    pm = p + (shifted - p) * mu_ref[...]

    W = RW_WIDTH
    r, k, v = pm[:, 0:W], pm[:, W:2 * W], pm[:, 2 * W:3 * W]
    wa = pm[:, 3 * W:3 * W + RW_LORA_PAD]
    gd = pm[:, 3 * W + RW_LORA_PAD:]

    w_in = w0_ref[...] + _dot3(jnp.tanh(wa), w2_ref[...])
    lw = -math.exp(-0.5) * jax.nn.sigmoid(w_in)
    a_sig = jax.nn.sigmoid(a0_ref[...] + _dot3(wa, a2_ref[...]))

    C = RW_CHUNK
    ci = lax.broadcasted_iota(jnp.int32, (C, C), 0)
    cj = lax.broadcasted_iota(jnp.int32, (C, C), 1)
    tri = jnp.where(cj <= ci, 1.0, 0.0).astype(BF16)
    l1 = lw.astype(BF16)
    rem = lw - l1.astype(F32)
    l2 = rem.astype(BF16)
    l3 = (rem - l2.astype(F32)).astype(BF16)
    td = lambda x: jnp.dot(tri, x, preferred_element_type=F32)
    cum = jnp.concatenate([td(l1[c0:c0 + C]) + td(l2[c0:c0 + C]) + td(l3[c0:c0 + C])
                           for c0 in range(0, tm, C)], axis=0)
    g = _dot(jax.nn.sigmoid(gd), g2_ref[...])

    ones = ones_ref[...]
    kk = k * kk_ref[...]
    ss = _dot(kk * kk, ones)
    kk = kk * lax.rsqrt(jnp.maximum(ss, 1e-24))
    k2 = k * (1.0 + (a_sig - 1.0) * ka_ref[...])
    bonus = _dot(r * k2 * rk_ref[...], ones) * v

    g_out[...] = g
    bonus_out[...] = bonus
    a_vec = -kk
    b_vec = kk * a_sig
    for h in range(RW_HEADS):
        sl = slice(h * RW_HEAD_DIM, (h + 1) * RW_HEAD_DIM)
        r_out[h] = r[:, sl]
        lw_out[h] = lw[:, sl]
        cum_out[h] = cum[:, sl]
        k_out[h] = k2[:, sl]
        v_out[h] = v[:, sl]
        a_out[h] = a_vec[:, sl]
        b_out[h] = b_vec[:, sl]


def _rwkv_prep(u, w_rw, B, T, mu, w0, w2p, a0, a2p, g2p, k_k, k_a, r_k, ones, tm=TILE_RWKV_PREP):
    M, D = u.shape
    tps = T // tm
    row = lambda x: x.reshape(1, -1)
    full = lambda shape: pl.BlockSpec(shape, lambda i: (0,) * len(shape))
    head_spec = pl.BlockSpec((None, RW_HEADS, tm, RW_HEAD_DIM), lambda i: (i // tps, 0, i % tps, 0))
    head_shape = jax.ShapeDtypeStruct((B, RW_HEADS, T, RW_HEAD_DIM), F32)
    tok_spec = pl.BlockSpec((tm, RW_WIDTH), lambda i: (i, 0))
    tok_shape = jax.ShapeDtypeStruct((M, RW_WIDTH), F32)
    return pl.pallas_call(
        functools.partial(_rwprep_kernel, tiles_per_seq=tps),
        grid=(M // tm,),
        in_specs=[
            pl.BlockSpec((tm, D), lambda i: (i, 0)), full((D, RW_PAD_COLS)),
            full((1, RW_PAD_COLS)), full((1, RW_WIDTH)), full((RW_LORA_PAD, RW_WIDTH)),
            full((1, RW_WIDTH)), full((RW_LORA_PAD, RW_WIDTH)), full((RW_GATE_PAD, RW_WIDTH)),
            full((1, RW_WIDTH)), full((1, RW_WIDTH)), full((1, RW_WIDTH)), full((RW_WIDTH, RW_WIDTH)),
        ],
        out_specs=[head_spec] * 7 + [tok_spec] * 2,
        out_shape=[head_shape] * 7 + [tok_shape] * 2,
        scratch_shapes=[pltpu.VMEM((8, RW_PAD_COLS), F32)],
        compiler_params=_cparams("arbitrary"),
        name="rwkv_prep",
    )(u, w_rw, row(mu), row(w0), w2p, row(a0), a2p, g2p, row(k_k), row(k_a), row(r_k), ones)


def _bmm(a, b, nt=False):
    spec = 'gik,gjk->gij' if nt else 'gij,gjk->gik'
    return jnp.einsum(spec, a.astype(BF16), b.astype(BF16), preferred_element_type=F32)


def _rwkv_kernel(r_ref, lw_ref, cum_ref, k_ref, v_ref, a_ref, b_ref, bonus_ref, g_ref, lnw_ref, lnb_ref, avg_ref,
                 y_ref, st_ref, ytok_ref):
    C = RW_CHUNK
    H = RW_HEADS
    Tc = r_ref.shape[1]
    nc = Tc // C
    G = H * nc

    @pl.when(pl.program_id(1) == 0)
    def _():
        st_ref[...] = jnp.zeros_like(st_ref)

    shp = lambda ref: ref[...].reshape(G, C, RW_HEAD_DIM)
    r, lw, L, k, v, a, b = (shp(x) for x in (r_ref, lw_ref, cum_ref, k_ref, v_ref, a_ref, b_ref))

    ti = lax.broadcasted_iota(jnp.int32, (C, C), 0)
    tj = lax.broadcasted_iota(jnp.int32, (C, C), 1)
    eye = jnp.where(ti == tj, 1.0, 0.0).astype(F32)
    wi = lax.broadcasted_iota(jnp.int32, (C, 2 * C), 0)
    wj = lax.broadcasted_iota(jnp.int32, (C, 2 * C), 1)
    second = wj >= C
    wjc = jnp.where(second, wj - C, wj)

    Lprev = L - lw
    Ltot = L[:, C - 1:C, :]

    eL = jnp.exp(L)
    enL = jnp.exp(-L)
    eh = jnp.exp(Ltot - L)
    At = a * jnp.exp(Lprev)
    Rt = r * eL
    BKt = jnp.concatenate([b * enL, k * enL], axis=1)
    BKh = jnp.concatenate([b * eh, k * eh], axis=1)

    P4 = _bmm(jnp.concatenate([At, Rt], axis=1), BKt, nt=True)
    top, bot = P4[:, :C, :], P4[:, C:, :]
    N = jnp.where((tj < ti)[None], top[:, :, :C], 0.0)
    AKz = jnp.where((second & (wjc < wi))[None], top, 0.0)
    RBK = jnp.where((wjc <= wi)[None], bot, 0.0)
    RB = RBK[:, :, :C]

    def siblings(s):
        return ((ti // (2 * s)) == (tj // (2 * s))) & ((ti // s) != (tj // s))

    X = eye[None] + jnp.where(siblings(1)[None], N, 0.0)
    s_blk = 2
    while s_blk < C:
        XE = _bmm(X, jnp.where(siblings(s_blk)[None], N, 0.0))
        X = X + _bmm(XE, X)
        s_blk *= 2

    zv = jnp.concatenate([jnp.zeros_like(v), v], axis=1)
    Abar = _bmm(X, At)
    W0 = _bmm(X, _bmm(AKz, zv))
    wv = jnp.concatenate([W0, v], axis=1)
    Y0 = _bmm(RBK, wv)
    Rbar = Rt + _bmm(RB, Abar)
    BKhT = jnp.swapaxes(BKh, 1, 2)
    Mtx = _bmm(BKhT[:, :, :C], Abar) + eye[None] * jnp.exp(Ltot)
    G0 = _bmm(BKhT, wv)

    hsplit = lambda x: x.reshape(H, nc, x.shape[1], x.shape[2])
    RM = hsplit(jnp.concatenate([Rbar, Mtx], axis=1))
    Y0, G0 = hsplit(Y0), hsplit(G0)
    St = st_ref[...]
    ys = []
    for c in range(nc):
        both = _bmm(RM[:, c], St)
        ys.append(both[:, :C] + Y0[:, c])
        St = both[:, C:] + G0[:, c]
    st_ref[...] = St
    y = jnp.concatenate(ys, axis=1)
    for h in range(H):
        ytok_ref[:, h * RW_HEAD_DIM:(h + 1) * RW_HEAD_DIM] = y[h]

    y = ytok_ref[...]
    avg = avg_ref[...]
    mean = _dot(y, avg)
    yc = y - mean
    var = _dot(yc * yc, avg)
    ya = (yc * lax.rsqrt(var + RW_GN_EPS) * lnw_ref[...] + lnb_ref[...] + bonus_ref[...]) * g_ref[...]
    y_ref[...] = ya.astype(BF16)


def _rwkv_recurrence(r, lw, cum, k, v, a, b, bonus, g, ln_w, ln_b, avg, tc=TILE_RWKV_SCAN):
    B, H, T, Dh = r.shape
    W = H * Dh
    nt = T // tc
    spec = pl.BlockSpec((None, H, tc, Dh), lambda bi, c: (bi, 0, c, 0))
    tok = pl.BlockSpec((tc, W), lambda bi, c: (bi * nt + c, 0))
    const = lambda shape: pl.BlockSpec(shape, lambda bi, c: (0, 0))
    return pl.pallas_call(
        _rwkv_kernel,
        grid=(B, nt),
        in_specs=[spec] * 7 + [tok, tok, const((1, W)), const((1, W)), const((W, W))],
        out_specs=tok,
        out_shape=jax.ShapeDtypeStruct((B * T, W), BF16),
        scratch_shapes=[pltpu.VMEM((H, Dh, Dh), F32), pltpu.VMEM((tc, W), F32)],
        compiler_params=_cparams("parallel", "arbitrary"),
        name="rwkv_recurrence",
    )(r, lw, cum, k, v, a, b, bonus, g, ln_w.reshape(1, W), ln_b.reshape(1, W), avg)


def _rope_tables(T, heads, head_dim, rot_dim, base):
    half = rot_dim // 2
    inv_freq = base ** (-jnp.arange(half, dtype=F32) / half)
    ang = jnp.arange(T).astype(F32)[:, None] * inv_freq[None, :]
    cos, sin = jnp.cos(ang), jnp.sin(ang)
    rest = head_dim - rot_dim
    c = jnp.concatenate([cos, cos, jnp.ones((T, rest), F32)], axis=1)
    s_lo = jnp.concatenate([-sin, jnp.zeros((T, half + rest), F32)], axis=1)
    s_hi = jnp.concatenate([jnp.zeros((T, half), F32), sin, jnp.zeros((T, rest), F32)], axis=1)
    tile = lambda x: jnp.tile(x, (1, heads))
    return tile(c), tile(s_lo), tile(s_hi)


def _apply_rope(x, c, s_lo, s_hi, half):
    out = []
    for gl in range(x.shape[-1] // LANES):
        ln = slice(gl * LANES, (gl + 1) * LANES)
        xg = x[:, ln]
        out.append(xg * c[:, ln] + pltpu.roll(xg, LANES - half, 1) * s_lo[:, ln] + pltpu.roll(xg, half, 1) * s_hi[:, ln])
    return jnp.concatenate(out, axis=-1)


def _dilproj_kernel(u_ref, w_ref, c_ref, slo_ref, shi_ref, gq_ref, gk_ref, avg_ref,
                    o0_ref, o1_ref, o2_ref, tmp_ref):
    u = u_ref[...]
    c, s_lo, s_hi = c_ref[...], slo_ref[...], shi_ref[...]
    W = DIL_WIDTH
    for grp, o_ref in enumerate((o0_ref, o1_ref, o2_ref)):
        dilation = DIL_PATTERNS[grp][1]
        rows = u.shape[0] // dilation
        for which in range(3):
            col = (3 * grp + which) * W
            val = jnp.dot(u, w_ref[:, col:col + W], preferred_element_type=F32)
            if which < 2:
                gain = (gq_ref if which == 0 else gk_ref)[grp:grp + 1, :]
                ms = jnp.dot((val * val).astype(BF16), avg_ref[...], preferred_element_type=F32)
                val = _apply_rope(val * lax.rsqrt(ms + EPS) * gain, c, s_lo, s_hi, ROPE_DIM // 2)
            if dilation == 1:
                o_ref[which, 0] = val.astype(BF16)
            else:
                _store_lane_groups(tmp_ref, val)
                for res in range(dilation):
                    for gl in range(tmp_ref.shape[0]):
                        piece = tmp_ref[gl, pl.ds(res, rows, stride=dilation), :]
                        o_ref[which, res, :, gl * LANES:(gl + 1) * LANES] = piece.astype(BF16)


def _dilated_projection(u, w_dil, B, T, tabs, gq, gk, avg, tm=TILE_DIL_PROJ):
    M, D = u.shape
    tps = T // tm
    W = DIL_WIDTH
    const = lambda shape: pl.BlockSpec(shape, lambda i: (0,) * len(shape))
    tspec = pl.BlockSpec((tm, W), lambda i: (i % tps, 0))
    out_specs, out_shape = [], []
    for _, d in DIL_PATTERNS:
        out_specs.append(pl.BlockSpec((3, None, d, tm // d, W), lambda i: (0, i // tps, 0, i % tps, 0)))
        out_shape.append(jax.ShapeDtypeStruct((3, B, d, T // d, W), BF16))
    return pl.pallas_call(
        _dilproj_kernel,
        grid=(M // tm,),
        in_specs=[pl.BlockSpec((tm, D), lambda i: (i, 0)), const((D, 3 * N_DIL * W)),
                  tspec, tspec, tspec, const((N_DIL, W)), const((N_DIL, W)), const((W, W))],
        out_specs=out_specs,
        out_shape=out_shape,
        scratch_shapes=[pltpu.VMEM((W // LANES, tm, LANES), F32)],
        compiler_params=_cparams("parallel"),
        name="dilated_projection",
    )(u, w_dil, *tabs, jnp.tile(gq * Q_PRESCALE, (1, DIL_HEADS)), jnp.tile(gk, (1, DIL_HEADS)), avg)


def _dilattn_kernel(q_ref, k_ref, v_ref, kprev_ref, vprev_ref, o_ref, lse_ref, *, blocks):
    i = pl.program_id(2)
    Lb = DIL_BLOCK
    PW = 2 * HEAD_DIM
    NP = DIL_WIDTH // PW
    nres = q_ref.shape[0]
    lane = lax.broadcasted_iota(jnp.int32, (Lb, PW), 1)
    low = lane < HEAD_DIM

    qs, ks, vs = [], [], []
    for blk in range(blocks):
        rows = slice(blk * Lb, (blk + 1) * Lb)
        for rr in range(nres):
            for p in range(NP):
                ln = slice(p * PW, (p + 1) * PW)
                q2 = q_ref[rr, rows, ln]
                zero = jnp.zeros_like(q2)
                qs.append(jnp.concatenate([jnp.where(low, q2, zero), jnp.where(low, zero, q2)], axis=0))
                if blk == 0:
                    ks.append(jnp.concatenate([kprev_ref[rr, :, ln], k_ref[rr, rows, ln]], axis=0))
                    vs.append(jnp.concatenate([vprev_ref[rr, :, ln], v_ref[rr, rows, ln]], axis=0))
                else:
                    both = slice((blk - 1) * Lb, (blk + 1) * Lb)
                    ks.append(k_ref[rr, both, ln])
                    vs.append(v_ref[rr, both, ln])
    Q, K, V = jnp.stack(qs), jnp.stack(ks), jnp.stack(vs)
    first = nres * NP

    s = jnp.einsum('gqd,gkd->gqk', Q, K, preferred_element_type=F32)
    qi = lax.broadcasted_iota(jnp.int32, (2 * Lb, 2 * Lb), 0) % Lb
    kj = lax.broadcasted_iota(jnp.int32, (2 * Lb, 2 * Lb), 1)
    window = (kj >= qi) & (kj <= qi + Lb)
    window0 = window & ((kj >= Lb) | (i > 0))
    s = jnp.concatenate([jnp.where(window0[None], s[:first], -jnp.inf),
                         jnp.where(window[None], s[first:], -jnp.inf)], axis=0) if blocks > 1 else \
        jnp.where(window0[None], s, -jnp.inf)
    m = jnp.max(s, axis=-1, keepdims=True)
    e = jnp.exp2(s - m).astype(BF16)
    vx = jnp.concatenate([V, jnp.ones_like(V)], axis=-1)
    nd = jnp.einsum('gqk,gkd->gqd', e, vx, preferred_element_type=F32)
    num, den = nd[:, :, :PW], nd[:, :, PW:]
    o2 = num / den
    l2 = m * math.log(2.0) + jnp.log(den)
    head_of_lane = lane // LSE_LANES
    for blk in range(blocks):
        rows = slice(blk * Lb, (blk + 1) * Lb)
        for rr in range(nres):
            lse_c = jnp.zeros((Lb, PW), F32)
            for p in range(NP):
                g = (blk * nres + rr) * NP + p
                ln = slice(p * PW, (p + 1) * PW)
                o_ref[rr, rows, ln] = jnp.where(low, o2[g, :Lb], o2[g, Lb:]).astype(BF16)
                lse_c = jnp.where(head_of_lane == 2 * p, l2[g, :Lb],
                                  jnp.where(head_of_lane == 2 * p + 1, l2[g, Lb:], lse_c))
            lse_ref[rr, rows, :] = lse_c


def _dilated_attention(qkv):
    _, B, d, Mr, W = qkv.shape
    Lb = DIL_BLOCK
    nb = Mr // Lb
    blocks = min(DIL_BLOCKS_PER_STEP, nb)
    nres = min(DIL_BLOCKS_PER_STEP // blocks, d)
    tile = blocks * Lb

    def cur(which):
        return pl.BlockSpec((None, None, nres, tile, W), lambda b, c, i: (which, b, c, i, 0))

    def prev(which):
        return pl.BlockSpec((None, None, nres, Lb, W),
                            lambda b, c, i: (which, b, c, jnp.maximum(i * blocks - 1, 0), 0))

    ospec = pl.BlockSpec((None, nres, tile, W), lambda b, c, i: (b, c, i, 0))
    lspec = pl.BlockSpec((None, nres, tile, LANES), lambda b, c, i: (b, c, i, 0))
    return pl.pallas_call(
        functools.partial(_dilattn_kernel, blocks=blocks),
        grid=(B, d // nres, nb // blocks),
        in_specs=[cur(0), cur(1), cur(2), prev(1), prev(2)],
        out_specs=[ospec, lspec],
        out_shape=[jax.ShapeDtypeStruct((B, d, Mr, W), BF16), jax.ShapeDtypeStruct((B, d, Mr, LANES), F32)],
        compiler_params=_cparams("parallel", "parallel", "parallel"),
        name=f"dilated_attention_d{d}",
    )(qkv, qkv, qkv, qkv, qkv)


def _ret_kernel(u_ref, w_ref, c_ref, slo_ref, shi_ref, gain_ref, o_ref, st_ref):
    C = RET_CHUNK
    QK = RET_HEADS * RET_QK_DIM

    @pl.when(pl.program_id(1) == 0)
    def _():
        st_ref[...] = jnp.zeros_like(st_ref)

    p = jnp.dot(u_ref[...], w_ref[...], preferred_element_type=F32)
    c, s_lo, s_hi = c_ref[...], slo_ref[...], shi_ref[...]
    q = _apply_rope(p[:, :QK], c, s_lo, s_hi, RET_QK_DIM // 2)
    k = _apply_rope(p[:, QK:2 * QK], c, s_lo, s_hi, RET_QK_DIM // 2) * (RET_QK_DIM ** -0.5)
    v = p[:, 2 * QK:2 * QK + RET_WIDTH]
    g = p[:, 2 * QK + RET_WIDTH:]
    gain = gain_ref[...]

    ji = lax.broadcasted_iota(jnp.int32, (C, C), 0)
    jj = lax.broadcasted_iota(jnp.int32, (C, C), 1)
    diff = (ji - jj).astype(F32)
    jcol = lax.broadcasted_iota(jnp.int32, (C, 1), 0).astype(F32)

    for h in range(RET_HEADS):
        lg = math.log(1.0 - 2.0 ** (-5.0 - h))
        qs = slice(h * RET_QK_DIM, (h + 1) * RET_QK_DIM)
        vs = slice(h * RET_V_DIM, (h + 1) * RET_V_DIM)
        decay_in = jnp.where(diff >= 0, jnp.exp(lg * jnp.maximum(diff, 0.0)), 0.0)
        q_dec = jnp.exp(lg * (jcol + 1.0))
        k_dec = jnp.exp(lg * (C - 1.0 - jcol))
        S = st_ref[h]
        for ch in range(p.shape[0] // C):
            rows = slice(ch * C, (ch + 1) * C)
            qh, kh, vh = q[rows, qs], k[rows, qs], v[rows, vs]
            s = lax.dot_general(qh.astype(BF16), kh.astype(BF16), (((1,), (1,)), ((), ())),
                                preferred_element_type=F32) * decay_in
            y = _dot(s, vh) + _dot(qh * q_dec, S)
            kv = lax.dot_general((kh * k_dec).astype(BF16), vh.astype(BF16), (((0,), (0,)), ((), ())),
                                 preferred_element_type=F32)
            S = math.exp(lg * C) * S + kv
            yn = y * lax.rsqrt(jnp.mean(y * y, axis=-1, keepdims=True) + EPS) * gain[:, vs]
            gh = g[rows, vs]
            o_ref[rows, vs] = (gh * jax.nn.sigmoid(gh) * yn).astype(BF16)
        st_ref[h] = S


def _retention(u, w_ret, B, T, tabs, gain, tr=TILE_RETENTION):
    M, D = u.shape
    nt = T // tr
    qk = RET_HEADS * RET_QK_DIM
    const = lambda shape: pl.BlockSpec(shape, lambda b, t: (0,) * len(shape))
    tspec = pl.BlockSpec((tr, qk), lambda b, t: (t, 0))
    return pl.pallas_call(
        _ret_kernel,
        grid=(B, nt),
        in_specs=[pl.BlockSpec((tr, D), lambda b, t: (b * nt + t, 0)), const(w_ret.shape),
                  tspec, tspec, tspec, const((1, RET_WIDTH))],
        out_specs=pl.BlockSpec((tr, RET_WIDTH), lambda b, t: (b * nt + t, 0)),
        out_shape=jax.ShapeDtypeStruct((M, RET_WIDTH), BF16),
        scratch_shapes=[pltpu.VMEM((RET_HEADS, RET_QK_DIM, RET_V_DIM), F32)],
        compiler_params=_cparams("parallel", "arbitrary"),
        name="retention",
    )(u, w_ret, *tabs, gain.reshape(1, -1))


def _merge_kernel(h_ref, u_ref, wg_ref, ya_ref, o0_ref, o1_ref, o2_ref, l0_ref, l1_ref, l2_ref, yc_ref,
                  wa_ref, wb_ref, wc_ref, wo_ref, expand_ref, out_ref, o1_scr, o2_scr, l1_scr, l2_scr):
    h = h_ref[...]
    u = u_ref[...]

    def token_order(ref, scr):
        dil, rows = ref.shape[0], ref.shape[1]
        for c in range(dil):
            blk = ref[c].astype(F32)
            for gl in range(scr.shape[0]):
                scr[gl, pl.ds(c, rows, stride=dil), :] = blk[:, gl * LANES:(gl + 1) * LANES]
        return jnp.concatenate([scr[gl] for gl in range(scr.shape[0])], axis=-1)

    o0, l0 = o0_ref[0].astype(F32), l0_ref[0]
    o1, l1 = token_order(o1_ref, o1_scr), token_order(l1_ref, l1_scr)
    o2, l2 = token_order(o2_ref, o2_scr), token_order(l2_ref, l2_scr)

    mx = jnp.maximum(jnp.maximum(l0, l1), l2)
    e0, e1, e2 = jnp.exp(l0 - mx), jnp.exp(l1 - mx), jnp.exp(l2 - mx)
    inv = 1.0 / (e0 + e1 + e2)

    def spread(w):
        hi, lo = _split2(w)
        return (jnp.dot(hi, expand_ref[...], preferred_element_type=F32)
                + jnp.dot(lo, expand_ref[...], preferred_element_type=F32))

    yb = spread(e0 * inv) * o0 + spread(e1 * inv) * o1 + spread(e2 * inv) * o2

    D = D_MODEL
    gate = lambda i: jax.nn.sigmoid(jnp.dot(u, wg_ref[:, i * D:(i + 1) * D], preferred_element_type=F32))
    merged = (gate(0) * jnp.dot(ya_ref[...], wa_ref[...], preferred_element_type=F32)
              + gate(1) * _dot(yb, wb_ref[...])
              + gate(2) * jnp.dot(yc_ref[...], wc_ref[...], preferred_element_type=F32))
    out_ref[...] = h + _dot(merged, wo_ref[...])


def _merge(h, u, T, w_gate, y_a, o, lse, y_c, wa, wb, wc, wo, tm=TILE_MERGE):
    M, D = h.shape
    tps = T // tm
    tok = lambda w: pl.BlockSpec((tm, w), lambda i: (i, 0))
    full = lambda shape: pl.BlockSpec(shape, lambda i: (0, 0))
    W = RW_WIDTH

    def res(group, width):
        d = DIL_PATTERNS[group][1]
        return pl.BlockSpec((None, d, tm // d, width), lambda i: (i // tps, 0, i % tps, 0))

    li = lax.broadcasted_iota(jnp.int32, (LANES, W), 0)
    lj = lax.broadcasted_iota(jnp.int32, (LANES, W), 1)
    expand = jnp.where(li == (lj // HEAD_DIM) * LSE_LANES, 1.0, 0.0).astype(BF16)

    return pl.pallas_call(
        _merge_kernel,
        grid=(M // tm,),
        in_specs=[tok(D), tok(D), full((D, 3 * D)), tok(W),
                  res(0, W), res(1, W), res(2, W), res(0, LANES), res(1, LANES), res(2, LANES), tok(W),
                  full((W, D)), full((W, D)), full((W, D)), full((D, D)), full((LANES, W))],
        out_specs=tok(D),
        out_shape=jax.ShapeDtypeStruct((M, D), F32),
        scratch_shapes=[pltpu.VMEM((W // LANES, tm, LANES), F32)] * 2 + [pltpu.VMEM((1, tm, LANES), F32)] * 2,
        compiler_params=_cparams("parallel"),
        name="gated_merge",
    )(h, u, w_gate, y_a, o[0], o[1], o[2], lse[0], lse[1], lse[2], y_c, wa, wb, wc, wo, expand)


def _xattn_kernel(h_ref, gx_ref, wq_ref, kv_ref, qn_ref, kn_ref, wo_ref, out_ref, k_scr, v_scr):
    D = D_MODEL

    @pl.when(pl.program_id(1) == 0)
    def _():
        kv = kv_ref[...]
        for hd in range(XA_HEADS):
            sl = slice(hd * XA_HEAD_DIM, (hd + 1) * XA_HEAD_DIM)
            k_scr[:, sl] = _rms(kv[:, sl], kn_ref[...]).astype(BF16)
        v_scr[...] = kv[:, D:].astype(BF16)

    h = h_ref[...]
    hn = _rms(h, gx_ref[...]).astype(BF16)
    q = jnp.dot(hn, wq_ref[...], preferred_element_type=F32)
    outs = []
    for hd in range(XA_HEADS):
        sl = slice(hd * XA_HEAD_DIM, (hd + 1) * XA_HEAD_DIM)
        qh = _rms(q[:, sl], qn_ref[...]).astype(BF16)
        s = lax.dot_general(qh, k_scr[:, sl], (((1,), (1,)), ((), ())), preferred_element_type=F32)
        m = jnp.max(s, axis=-1, keepdims=True)
        e = jnp.exp2(s - m)
        pr = e / jnp.sum(e, axis=-1, keepdims=True)
        outs.append(jnp.dot(pr.astype(BF16), v_scr[:, sl], preferred_element_type=F32))
    o = jnp.concatenate(outs, axis=-1).astype(BF16)
    out_ref[...] = h + jnp.dot(o, wo_ref[...], preferred_element_type=F32)


def _cross_attention(h, B, T, g_x, wq, kv, q_norm, k_norm, wo, tm=TILE_XATTN):
    M, D = h.shape
    tps = T // tm
    full = lambda shape: pl.BlockSpec(shape, lambda b, t: (0, 0))
    return pl.pallas_call(
        _xattn_kernel,
        grid=(B, tps),
        in_specs=[
            pl.BlockSpec((tm, D), lambda b, t: (b * tps + t, 0)),
            full((1, D)), full((D, D)),
            pl.BlockSpec((MEM_LEN, 2 * D), lambda b, t: (b, 0)),
            full((1, XA_HEAD_DIM)), full((1, XA_HEAD_DIM)), full((D, D)),
        ],
        out_specs=pl.BlockSpec((tm, D), lambda b, t: (b * tps + t, 0)),
        out_shape=jax.ShapeDtypeStruct((M, D), F32),
        scratch_shapes=[pltpu.VMEM((MEM_LEN, D), BF16), pltpu.VMEM((MEM_LEN, D), BF16)],
        compiler_params=_cparams("parallel", "arbitrary"),
        name="cross_attention",
    )(h, g_x.reshape(1, D), wq, kv, (q_norm * (XA_HEAD_DIM ** -0.5 * math.log2(math.e))).reshape(1, -1),
      k_norm.reshape(1, -1), wo)


def _pad_rows(w, rows_before, total):
    return jnp.pad(w, ((rows_before, total - rows_before - w.shape[0]), (0, 0)))


def _layer(h, mem2, B, T, p, dil_tabs, ret_tabs):
    bf = lambda w: w.astype(BF16)
    ones = _block_ones(RW_WIDTH, RW_HEAD_DIM)
    avg = ones * (1.0 / RW_HEAD_DIM)
    h, u = _ffn(h, p['norm_ffn1'], bf(p['ffn1_w13']), bf(p['ffn1_w2']), g_next=p['norm_mix'])

    w_in = p['w_in']
    c_dil = RW_IN_COLS
    c_ret = c_dil + 3 * N_DIL * DIL_WIDTH
    c_gate = c_ret + 2 * RET_HEADS * RET_QK_DIM + 2 * RET_WIDTH
    rw_w = jnp.pad(w_in[:, :c_dil], ((0, 0), (0, RW_PAD_COLS - RW_IN_COLS)))
    mu = jnp.pad(p['rw_mu'], (0, RW_PAD_COLS - RW_IN_COLS))

    w2p = _pad_rows(p['rw_w2'], 0, RW_LORA_PAD)
    a2p = _pad_rows(p['rw_a2'], RW_LORA, RW_LORA_PAD)
    g2p = _pad_rows(p['rw_g2'], 0, RW_GATE_PAD)
    r, lw, cum, k, v, a, b, g, bonus = _rwkv_prep(u, bf(rw_w), B, T, mu, p['rw_w0'], w2p, p['rw_a0'], a2p, g2p,
                                             p['rw_k_k'], p['rw_k_a'], p['rw_r_k'], ones)
    y_a = _rwkv_recurrence(r, lw, cum, k, v, a, b, bonus, g, p['rw_ln_w'], p['rw_ln_b'], avg)

    qkvs = _dilated_projection(u, bf(w_in[:, c_dil:c_ret]), B, T, dil_tabs,
                               p['dil_q_norm'], p['dil_k_norm'], avg)
    o, lse = zip(*[_dilated_attention(qkv) for qkv in qkvs])

    y_c = _retention(u, bf(w_in[:, c_ret:c_gate]), B, T, ret_tabs, p['ret_norm'])

    h = _merge(h, u, T, bf(w_in[:, c_gate:]), y_a, o, lse, y_c,
               bf(p['w_branch_rwkv']), bf(p['w_branch_dil']), bf(p['w_branch_ret']), bf(p['w_out']))

    kv = _norm_matmul(mem2, p['norm_mem'], bf(p['xa_wkv']), MEM_LEN, 1024)
    h = _cross_attention(h, B, T, p['norm_xattn'], bf(p['xa_wq']), kv, p['xa_q_norm'], p['xa_k_norm'],
                         bf(p['xa_wo']))
    h = _ffn(h, p['norm_ffn2'], bf(p['ffn2_w13']), bf(p['ffn2_w2']))
    return h


_PARAM_NAMES = ('norm_ffn1', 'ffn1_w13', 'ffn1_w2', 'norm_mix', 'w_in', 'rw_mu', 'rw_w0', 'rw_w2', 'rw_a0',
                'rw_a2', 'rw_g2', 'rw_k_k', 'rw_k_a', 'rw_r_k', 'rw_ln_w', 'rw_ln_b', 'dil_q_norm',
                'dil_k_norm', 'ret_norm', 'w_branch_rwkv', 'w_branch_dil', 'w_branch_ret', 'w_out',
                'norm_xattn', 'norm_mem', 'xa_wq', 'xa_wkv', 'xa_q_norm', 'xa_k_norm', 'xa_wo',
                'norm_ffn2', 'ffn2_w13', 'ffn2_w2')


def kernel(x, mem, norm_ffn1, ffn1_w13, ffn1_w2, norm_mix, w_in, rw_mu, rw_w0, rw_w2, rw_a0, rw_a2, rw_g2, rw_k_k, rw_k_a, rw_r_k, rw_ln_w, rw_ln_b, dil_q_norm, dil_k_norm, ret_norm, w_branch_rwkv, w_branch_dil, w_branch_ret, w_out, norm_xattn, norm_mem, xa_wq, xa_wkv, xa_q_norm, xa_k_norm, xa_wo, norm_ffn2, ffn2_w13, ffn2_w2):
    params = dict(zip(_PARAM_NAMES, (norm_ffn1, ffn1_w13, ffn1_w2, norm_mix, w_in, rw_mu, rw_w0, rw_w2, rw_a0,
                                     rw_a2, rw_g2, rw_k_k, rw_k_a, rw_r_k, rw_ln_w, rw_ln_b, dil_q_norm,
                                     dil_k_norm, ret_norm, w_branch_rwkv, w_branch_dil, w_branch_ret, w_out,
                                     norm_xattn, norm_mem, xa_wq, xa_wkv, xa_q_norm, xa_k_norm, xa_wo,
                                     norm_ffn2, ffn2_w13, ffn2_w2)))
    B, T, D = x.shape
    assert D == D_MODEL and T % (DIL_BLOCK * DIL_PATTERNS[-1][1]) == 0 and mem.shape[1] == MEM_LEN
    depth = norm_ffn1.shape[0]
    dil_tabs = _rope_tables(T, DIL_HEADS, HEAD_DIM, ROPE_DIM, ROPE_THETA)
    ret_tabs = _rope_tables(T, RET_HEADS, RET_QK_DIM, RET_QK_DIM, RET_ROPE_BASE)
    h = x.reshape(B * T, D)
    mem2 = mem.reshape(B * MEM_LEN, D)
    for l in range(depth):
        h = _layer(h, mem2, B, T, {n: params[n][l] for n in _PARAM_NAMES}, dil_tabs, ret_tabs)
    return h.reshape(B, T, D)
```
